```python
import math
import jax, jax.numpy as jnp
from jax import lax
import numpy as np

D_MODEL = 2048
BATCH = 4
SEQ = 2048
DEPTH = 4
DEC_BATCH = 8
DEC_SEQ = 8
PAST_LEN = 16384
PAGE_SIZE = 128

HEAD_DIM = 128
N_HEADS_DA = 8
DA_QK = HEAD_DIM // 2
N_HEADS_RET = 4
N_HEADS_ML = 4
DA_W = N_HEADS_DA * HEAD_DIM
RET_W = N_HEADS_RET * HEAD_DIM
ML_W = N_HEADS_ML * HEAD_DIM
MIX_W = DA_W + RET_W + ML_W
ML_CONV = 4
FFN_CONV = 3
D_FF = 5632
CHUNK = 128
Q_BLOCK = 128
RMS_EPS = 1e-6
ROPE_BASE = 10000.0
NEG_INF = -1e30
IN_SIZES = (DA_W, DA_W, DA_W, RET_W, RET_W, RET_W, RET_W, ML_W, ML_W, ML_W, N_HEADS_ML, N_HEADS_ML)
IN_W = 3 * DA_W + 4 * RET_W + 3 * ML_W + 2 * N_HEADS_ML

kernel_name = "hymba_diffattn_retention_mlstm_convffn_adaln_step"


def rmsnorm(x, g):
    xf = x.astype(jnp.float32)
    y = xf * lax.rsqrt(jnp.mean(xf * xf, axis=-1, keepdims=True) + RMS_EPS)
    return (y * g).astype(x.dtype)


def head_rmsnorm(x, g):
    h, dh = x.shape[-2], x.shape[-1]
    return rmsnorm(x, g.reshape(h, dh))


def ada_modulation(c, w, b):
    mod = jnp.einsum("bd,de->be", jax.nn.silu(c), w) + b
    return jnp.split(mod[:, None, :], 6, axis=-1)


def rotary(x, pos):
    half = x.shape[-1] // 2
    inv = ROPE_BASE ** (-jnp.arange(half, dtype=jnp.float32) / half)
    ang = pos.astype(jnp.float32)[:, None] * inv[None, :]
    cos = jnp.cos(ang)[None, :, None, :]
    sin = jnp.sin(ang)[None, :, None, :]
    x1 = x[..., :half].astype(jnp.float32)
    x2 = x[..., half:].astype(jnp.float32)
    return jnp.concatenate([x1 * cos - x2 * sin, x1 * sin + x2 * cos], axis=-1).astype(x.dtype)


def causal_dwconv(x, buf, w, b):
    width = w.shape[0]
    t = x.shape[1]
    xp = jnp.concatenate([buf.astype(x.dtype), x], axis=1)
    y = b + sum(xp[:, j:j + t, :] * w[j] for j in range(width))
    return y, xp[:, xp.shape[1] - (width - 1):, :]


def chunked_scan(step, init, xs, chunk):
    b, t = xs[0].shape[0], xs[0].shape[1]
    n = t // chunk
    xs_c = tuple(jnp.moveaxis(a.reshape((b, n, chunk) + a.shape[2:]), 1, 0) for a in xs)
    final, ys = lax.scan(step, init, xs_c)
    ys = jnp.moveaxis(ys, 0, 1).reshape((b, t) + ys.shape[3:])
    return ys, final


def diff_attn_core(q, k, v, mask, lam):
    scale = DA_QK ** -0.5
    s1 = jnp.einsum("bqhd,bkhd->bhqk", q[..., :DA_QK], k[..., :DA_QK]).astype(jnp.float32) * scale
    s2 = jnp.einsum("bqhd,bkhd->bhqk", q[..., DA_QK:], k[..., DA_QK:]).astype(jnp.float32) * scale
    s1 = jnp.where(mask, s1, NEG_INF)
    s2 = jnp.where(mask, s2, NEG_INF)
    a = jax.nn.softmax(s1, axis=-1) - lam.astype(jnp.float32) * jax.nn.softmax(s2, axis=-1)
    return jnp.einsum("bhqk,bkhe->bqhe", a.astype(v.dtype), v)


def diff_attn_prompt(q, k, v, lam):
    b, s, h, e = q.shape
    nb = s // Q_BLOCK
    q_blocks = jnp.moveaxis(q.reshape(b, nb, Q_BLOCK, h, e), 1, 0)
    k_pos = jnp.arange(s)

    def one_block(args):
        qb, i = args
        q_pos = i * Q_BLOCK + jnp.arange(Q_BLOCK)
        return diff_attn_core(qb, k, v, k_pos[None, :] <= q_pos[:, None], lam)

    out = lax.map(one_block, (q_blocks, jnp.arange(nb)))
    return jnp.moveaxis(out, 0, 1).reshape(b, s, h, v.shape[-1])


def ret_log_decay(h):
    return jnp.log(1.0 - 2.0 ** (-5.0 - jnp.arange(h, dtype=jnp.float32)))


def retention_chunk(S, qkv):
    q, k, v = (a.astype(jnp.float32) for a in qkv)
    L, H = q.shape[1], q.shape[2]
    lg = ret_log_decay(H)
    idx = jnp.arange(L, dtype=jnp.float32)
    rel = idx[:, None] - idx[None, :]
    dmat = jnp.where(rel[None] >= 0, jnp.exp(lg[:, None, None] * jnp.maximum(rel, 0.0)[None]), 0.0)
    scores = jnp.einsum("bthd,bshd->bhts", q, k) * dmat[None]
    inner = jnp.einsum("bhts,bshe->bthe", scores, v)
    q_dec = jnp.exp(lg[None, :] * (idx[:, None] + 1.0))
    cross = jnp.einsum("bthd,bhde->bthe", q, S) * q_dec[None, :, :, None]
    k_dec = jnp.exp(lg[None, :] * (L - 1.0 - idx[:, None]))
    S_new = S * jnp.exp(lg * L)[None, :, None, None] + jnp.einsum("bshd,bshe->bhde", k * k_dec[None, :, :, None], v)
    return S_new, inner + cross


def mlstm_chunk(state, inp):
    C, n, m = state
    q, k, v, ig, lf = (a.astype(jnp.float32) for a in inp)
    L = q.shape[1]
    b = jnp.moveaxis(jnp.cumsum(lf, axis=1), 1, 2)
    ih = jnp.moveaxis(ig, 1, 2)
    causal = jnp.tril(jnp.ones((L, L), dtype=bool))
    dlog = jnp.where(causal, b[..., :, None] - b[..., None, :] + ih[..., None, :], -jnp.inf)
    inter = b + m[..., None]
    m_row = jnp.maximum(inter, jnp.max(dlog, axis=-1))
    w = jnp.exp(dlog - m_row[..., None])
    s_in = jnp.exp(inter - m_row)
    a = w * jnp.einsum("bthd,bshd->bhts", q, k)
    num = jnp.einsum("bhts,bshe->bthe", a, v) + jnp.einsum("bthd,bhde->bthe", q, C) * jnp.moveaxis(s_in, 1, 2)[..., None]
    den = jnp.sum(a, axis=-1) + jnp.einsum("bthd,bhd->bht", q, n) * s_in
    h = num / jnp.moveaxis(jnp.maximum(jnp.abs(den), jnp.exp(-m_row)), 1, 2)[..., None]
    b_last = b[..., -1]
    dlast = b_last[..., None] - b + ih
    m_new = jnp.maximum(b_last + m, jnp.max(dlast, axis=-1))
    ws = jnp.exp(dlast - m_new[..., None])
    s_old = jnp.exp(b_last + m - m_new)
    C_new = C * s_old[..., None, None] + jnp.einsum("bhs,bshd,bshe->bhde", ws, k, v)
    n_new = n * s_old[..., None] + jnp.einsum("bhs,bshd->bhd", ws, k)
    return (C_new, n_new, m_new), h


def trunk_layer(x, c, pos, lam_init, p, kv_past, st):
    B, T, _ = x.shape
    chunk = CHUNK if T % CHUNK == 0 else T
    sh1, sc1, g1, sh2, sc2, g2 = ada_modulation(c, p["w_ada"], p["b_ada"])
    h = rmsnorm(x, p["g_mix"]) * (1.0 + sc1) + sh1
    proj = jnp.einsum("btd,de->bte", h, p["w_in"])
    split_at = [int(s) for s in np.cumsum(IN_SIZES)[:-1]]
    (a_q, a_k, a_v, r_q, r_k, r_v, r_g, m_u, m_v, m_o, m_i, m_f) = jnp.split(proj, split_at, axis=-1)

    qa = a_q.reshape(B, T, N_HEADS_DA, HEAD_DIM)
    ka = a_k.reshape(B, T, N_HEADS_DA, HEAD_DIM)
    va = a_v.reshape(B, T, N_HEADS_DA, HEAD_DIM)
    lam = (jnp.exp(jnp.sum(p["lam_q1"] * p["lam_k1"])) - jnp.exp(jnp.sum(p["lam_q2"] * p["lam_k2"])) + lam_init)
    if kv_past is None:
        oa = diff_attn_prompt(qa, ka, va, lam)
    else:
        k_past, v_past = kv_past
        P = k_past.shape[1]
        k_all = jnp.concatenate([k_past.astype(ka.dtype), ka], axis=1)
        v_all = jnp.concatenate([v_past.astype(va.dtype), va], axis=1)
        mask = jnp.arange(P + T)[None, :] <= (P + jnp.arange(T))[:, None]
        oa = diff_attn_core(qa, k_all, v_all, mask, lam)
    oa = (head_rmsnorm(oa, p["da_norm_g"]) * (1.0 - lam_init)).reshape(B, T, DA_W).astype(x.dtype)

    rq = rotary(r_q.reshape(B, T, N_HEADS_RET, HEAD_DIM), pos)
    rk = rotary(r_k.reshape(B, T, N_HEADS_RET, HEAD_DIM), pos) * HEAD_DIM ** -0.5
    rv = r_v.reshape(B, T, N_HEADS_RET, HEAD_DIM)
    ro, ret_s = chunked_scan(retention_chunk, st["ret_s"], (rq, rk, rv), chunk)
    ro = head_rmsnorm(ro, p["ret_norm_g"]).reshape(B, T, RET_W).astype(x.dtype) * jax.nn.silu(r_g)

    uc, ml_conv = causal_dwconv(m_u, st["ml_conv"], p["ml_conv_w"], p["ml_conv_b"])
    uc = jax.nn.silu(uc)
    uch = uc.reshape(B, T, N_HEADS_ML, HEAD_DIM)
    mq = jnp.einsum("bthd,hde->bthe", uch, p["ml_wq"])
    mk = jnp.einsum("bthd,hde->bthe", uch, p["ml_wk"]) * HEAD_DIM ** -0.5
    mv = m_v.reshape(B, T, N_HEADS_ML, HEAD_DIM)
    ig = m_i + p["ml_b_i"]
    lf = jax.nn.log_sigmoid((m_f + p["ml_b_f"]).astype(jnp.float32))
    mh, (ml_c, ml_n, ml_m) = chunked_scan(mlstm_chunk, (st["ml_c"], st["ml_n"], st["ml_m"]), (mq, mk, mv, ig, lf), chunk)
    mo = jax.nn.sigmoid(m_o) * (head_rmsnorm(mh, p["ml_norm_g"]).reshape(B, T, ML_W).astype(x.dtype) + p["ml_skip"] * uc)

    mix = jnp.concatenate([oa, ro, mo], axis=-1)
    x = x + g1 * jnp.einsum("bte,ed->btd", mix, p["w_out"])

    h2 = rmsnorm(x, p["g_ffn"]) * (1.0 + sc2) + sh2
    up = jnp.einsum("btd,df->btf", h2, p["ffn_w_up"])
    upc, ffn_conv = causal_dwconv(up, st["ffn_conv"], p["ffn_conv_w"], p["ffn_conv_b"])
    gate, val = jnp.split(upc, 2, axis=-1)
    x = x + g2 * jnp.einsum("btf,fd->btd", jax.nn.silu(gate) * val, p["ffn_w_down"])

    new_st = dict(ret_s=ret_s, ml_c=ml_c, ml_n=ml_n, ml_m=ml_m, ml_conv=ml_conv, ffn_conv=ffn_conv)
    return x, ka, va, new_st


def setup_inputs(seed: int = 0) -> dict:
    key = jax.random.key(seed)
    ks = iter(jax.random.split(key, 64))

    def nrm(shape, s):
        return jax.random.normal(next(ks), shape, jnp.float32) * s

    D = D_MODEL
    n_pages = PAST_LEN // PAGE_SIZE
    n_used = DEC_BATCH * n_pages
    n_pool = n_used + max(1, n_used // 4)
    page_table = jax.random.permutation(next(ks), n_pool)[:n_used].reshape(DEC_BATCH, n_pages).astype(jnp.int32)
    return {
        "x_prompt": nrm((BATCH, SEQ, D), 1.0),
        "x_sample": nrm((DEC_BATCH, DEC_SEQ, D), 1.0),
        "c_prompt": nrm((BATCH, D), 1.0),
        "c_sample": nrm((DEC_BATCH, D), 1.0),
        "cache_k": nrm((DEPTH, n_pool, PAGE_SIZE, N_HEADS_DA, HEAD_DIM), 1.0),
        "cache_v": nrm((DEPTH, n_pool, PAGE_SIZE, N_HEADS_DA, HEAD_DIM), 1.0),
        "page_table": page_table,
        "state_ret": nrm((DEPTH, DEC_BATCH, N_HEADS_RET, HEAD_DIM, HEAD_DIM), 0.1),
        "state_ml_c": nrm((DEPTH, DEC_BATCH, N_HEADS_ML, HEAD_DIM, HEAD_DIM), 0.1),
        "state_ml_n": nrm((DEPTH, DEC_BATCH, N_HEADS_ML, HEAD_DIM), 0.1),
        "state_ml_m": nrm((DEPTH, DEC_BATCH, N_HEADS_ML), 1.0),
        "state_ml_conv": nrm((DEPTH, DEC_BATCH, ML_CONV - 1, ML_W), 1.0),
        "state_ffn_conv": nrm((DEPTH, DEC_BATCH, FFN_CONV - 1, 2 * D_FF), 1.0),
        "w_ada": nrm((DEPTH, D, 6 * D), 0.5 * D ** -0.5),
        "b_ada": nrm((DEPTH, 6 * D), 0.01),
        "g_mix": 1.0 + nrm((DEPTH, D), 0.05),
        "w_in": nrm((DEPTH, D, IN_W), D ** -0.5),
        "lam_q1": nrm((DEPTH, DA_QK), 0.1),
        "lam_k1": nrm((DEPTH, DA_QK), 0.1),
        "lam_q2": nrm((DEPTH, DA_QK), 0.1),
        "lam_k2": nrm((DEPTH, DA_QK), 0.1),
        "da_norm_g": 1.0 + nrm((DEPTH, DA_W), 0.05),
        "ret_norm_g": 1.0 + nrm((DEPTH, RET_W), 0.05),
        "ml_conv_w": nrm((DEPTH, ML_CONV, ML_W), ML_CONV ** -0.5),
        "ml_conv_b": nrm((DEPTH, ML_W), 0.01),
        "ml_wq": nrm((DEPTH, N_HEADS_ML, HEAD_DIM, HEAD_DIM), HEAD_DIM ** -0.5),
        "ml_wk": nrm((DEPTH, N_HEADS_ML, HEAD_DIM, HEAD_DIM), HEAD_DIM ** -0.5),
        "ml_b_i": nrm((DEPTH, N_HEADS_ML), 0.1),
        "ml_b_f": 3.0 + 3.0 * jax.random.uniform(next(ks), (DEPTH, N_HEADS_ML), jnp.float32),
        "ml_norm_g": 1.0 + nrm((DEPTH, ML_W), 0.05),
        "ml_skip": 1.0 + nrm((DEPTH, ML_W), 0.05),
        "w_out": nrm((DEPTH, MIX_W, D), MIX_W ** -0.5),
        "g_ffn": 1.0 + nrm((DEPTH, D), 0.05),
        "ffn_w_up": nrm((DEPTH, D, 2 * D_FF), D ** -0.5),
        "ffn_conv_w": nrm((DEPTH, FFN_CONV, 2 * D_FF), FFN_CONV ** -0.5),
        "ffn_conv_b": nrm((DEPTH, 2 * D_FF), 0.01),
        "ffn_w_down": nrm((DEPTH, D_FF, D), D_FF ** -0.5),
        "g_final": 1.0 + nrm((D,), 0.05),
    }


def reference(x_prompt, x_sample, c_prompt, c_sample, cache_k, cache_v, page_table, state_ret, state_ml_c, state_ml_n, state_ml_m, state_ml_conv, state_ffn_conv, w_ada, b_ada, g_mix, w_in, lam_q1, lam_k1, lam_q2, lam_k2, da_norm_g, ret_norm_g, ml_conv_w, ml_conv_b, ml_wq, ml_wk, ml_b_i, ml_b_f, ml_norm_g, ml_skip, w_out, g_ffn, ffn_w_up, ffn_conv_w, ffn_conv_b, ffn_w_down, g_final):
    f32 = jnp.float32
    bp, tp = x_prompt.shape[0], x_prompt.shape[1]
    bs, ts = x_sample.shape[0], x_sample.shape[1]
    n_pages = page_table.shape[1]
    past = n_pages * PAGE_SIZE
    pos_p = jnp.arange(tp)
    pos_s = past + jnp.arange(ts)
    zero_st = dict(
        ret_s=jnp.zeros((bp, N_HEADS_RET, HEAD_DIM, HEAD_DIM), f32),
        ml_c=jnp.zeros((bp, N_HEADS_ML, HEAD_DIM, HEAD_DIM), f32),
        ml_n=jnp.zeros((bp, N_HEADS_ML, HEAD_DIM), f32),
        ml_m=jnp.zeros((bp, N_HEADS_ML), f32),
        ml_conv=jnp.zeros((bp, ML_CONV - 1, ML_W), x_prompt.dtype),
        ffn_conv=jnp.zeros((bp, FFN_CONV - 1, 2 * D_FF), x_prompt.dtype))
    names = ("ret_s", "ml_c", "ml_n", "ml_m", "ml_conv", "ffn_conv")
    out_p = {n: [] for n in names}
    out_s = {n: [] for n in names}
    kp, vp, ksm, vsm = [], [], [], []
    yp, ys = x_prompt, x_sample
    for l in range(DEPTH):
        lam_init = 0.8 - 0.6 * math.exp(-0.3 * l)
        p = dict(w_ada=w_ada[l], b_ada=b_ada[l], g_mix=g_mix[l], w_in=w_in[l],
                 lam_q1=lam_q1[l], lam_k1=lam_k1[l], lam_q2=lam_q2[l], lam_k2=lam_k2[l],
                 da_norm_g=da_norm_g[l], ret_norm_g=ret_norm_g[l],
                 ml_conv_w=ml_conv_w[l], ml_conv_b=ml_conv_b[l], ml_wq=ml_wq[l], ml_wk=ml_wk[l],
                 ml_b_i=ml_b_i[l], ml_b_f=ml_b_f[l], ml_norm_g=ml_norm_g[l], ml_skip=ml_skip[l],
                 w_out=w_out[l], g_ffn=g_ffn[l], ffn_w_up=ffn_w_up[l], ffn_conv_w=ffn_conv_w[l],
                 ffn_conv_b=ffn_conv_b[l], ffn_w_down=ffn_w_down[l])
        yp, k_new, v_new, st_p = trunk_layer(yp, c_prompt, pos_p, lam_init, p, None, zero_st)
        kp.append(k_new)
        vp.append(v_new)
        k_past = cache_k[l][page_table].reshape(bs, past, N_HEADS_DA, HEAD_DIM)
        v_past = cache_v[l][page_table].reshape(bs, past, N_HEADS_DA, HEAD_DIM)
        st_in = dict(ret_s=state_ret[l].astype(f32), ml_c=state_ml_c[l].astype(f32),
                     ml_n=state_ml_n[l].astype(f32), ml_m=state_ml_m[l].astype(f32),
                     ml_conv=state_ml_conv[l], ffn_conv=state_ffn_conv[l])
        ys, k_new_s, v_new_s, st_s = trunk_layer(ys, c_sample, pos_s, lam_init, p, (k_past, v_past), st_in)
        ksm.append(k_new_s)
        vsm.append(v_new_s)
        for n in names:
            out_p[n].append(st_p[n])
            out_s[n].append(st_s[n])
    y_prompt = rmsnorm(yp, g_final)
    y_sample = rmsnorm(ys, g_final)
    return (y_prompt, y_sample,
            jnp.stack(kp), jnp.stack(vp), jnp.stack(ksm), jnp.stack(vsm),
            jnp.stack(out_p["ret_s"]), jnp.stack(out_s["ret_s"]),
            jnp.stack(out_p["ml_c"]), jnp.stack(out_s["ml_c"]),
            jnp.stack(out_p["ml_n"]), jnp.stack(out_s["ml_n"]),
            jnp.stack(out_p["ml_m"]), jnp.stack(out_s["ml_m"]),
            jnp.stack(out_p["ml_conv"]), jnp.stack(out_s["ml_conv"]),
            jnp.stack(out_p["ffn_conv"]), jnp.stack(out_s["ffn_conv"]))
```

```python
import functools
import math

import numpy as np
import jax
import jax.numpy as jnp
from jax import lax
from jax.experimental import pallas as pl
from jax.experimental.pallas import tpu as pltpu

F32 = jnp.float32
BF16 = jnp.bfloat16

HEAD_DIM = 128
DA_QK = HEAD_DIM // 2
N_DA = 8
N_RET = 4
N_ML = 4
DA_W = N_DA * HEAD_DIM
RET_W = N_RET * HEAD_DIM
ML_W = N_ML * HEAD_DIM
ML_CONV = 4
FFN_CONV = 3
CHUNK = 128
PAGE = 128
RMS_EPS = 1e-6
ROPE_BASE = 10000.0
NEG_INF = -1e30

COL_AQ, COL_AK, COL_AV = 0, DA_W, 2 * DA_W
COL_RQ = 3 * DA_W
COL_RK, COL_RV, COL_RG = COL_RQ + RET_W, COL_RQ + 2 * RET_W, COL_RQ + 3 * RET_W
COL_MU = COL_RQ + 4 * RET_W
COL_MV, COL_MO = COL_MU + ML_W, COL_MU + 2 * ML_W
PROJ_W = COL_MU + 3 * ML_W
N_GATES = 2 * N_ML
GATE_ROWS = 16

SUBLANES = 8
LANES = 128
VMEM_LIMIT = 56 * 1024 * 1024

NT_DIMS = (((1,), (1,)), ((), ()))
TN_DIMS = (((0,), (0,)), ((), ()))


def _tile(n, pref, mult=LANES):
    if n <= pref:
        return n
    t = (pref // mult) * mult
    while t >= mult:
        if n % t == 0:
            return t
        t -= mult
    return n


def _params(sem):
    return pltpu.CompilerParams(dimension_semantics=sem, vmem_limit_bytes=VMEM_LIMIT)


def _rms(x):
    return x * lax.rsqrt(jnp.mean(x * x, axis=-1, keepdims=True) + RMS_EPS)


def _ada_kernel(c_ref, w_ref, b_ref, o_ref):
    c = c_ref[...]
    a = (c * jax.nn.sigmoid(c)).astype(BF16)
    o_ref[...] = jnp.dot(a, w_ref[...].astype(BF16), preferred_element_type=F32) + b_ref[...]


def _ada(c_all, w_ada, b_ada):
    depth, d, n = w_ada.shape
    rows = c_all.shape[0]
    tn = _tile(n, 1024)
    return pl.pallas_call(
        _ada_kernel,
        grid=(depth, n // tn),
        in_specs=[
            pl.BlockSpec((rows, d), lambda l, j: (0, 0)),
            pl.BlockSpec((None, d, tn), lambda l, j: (l, 0, j)),
            pl.BlockSpec((None, 1, tn), lambda l, j: (l, 0, j)),
        ],
        out_specs=pl.BlockSpec((None, rows, tn), lambda l, j: (l, 0, j)),
        out_shape=jax.ShapeDtypeStruct((depth, rows, n), F32),
        compiler_params=_params(("arbitrary", "arbitrary")),
        name="ada",
    )(c_all, w_ada, b_ada.reshape(depth, 1, n))


def _norm_kernel(*refs, modulated):
    if modulated:
        x_ref, g_ref, sc_ref, sh_ref, o_ref = refs
    else:
        x_ref, g_ref, o_ref = refs
    y = _rms(x_ref[...]) * g_ref[...]
    if modulated:
        y = y * (1.0 + sc_ref[...]) + sh_ref[...]
    o_ref[...] = y.astype(o_ref.dtype)


def _mod_spec(mod, tm, tn, with_n):
    per_row = mod.shape[1] != 1
    rows = tm if per_row else 1
    if with_n:
        return pl.BlockSpec((None, rows, tn), lambda n, b, t: (b, t if per_row else 0, n))
    return pl.BlockSpec((None, rows, tn), lambda b, t: (b, t if per_row else 0, 0))


def _norm(x, g, l, sc=None, sh=None, out_dtype=BF16):
    bm, tmx, d = x.shape
    tm = _tile(tmx, 512, SUBLANES)
    modulated = sc is not None
    g_spec = (pl.BlockSpec((None, 1, d), lambda b, t: (l, 0, 0)) if l is not None
              else pl.BlockSpec((1, d), lambda b, t: (0, 0)))
    in_specs = [pl.BlockSpec((None, tm, d), lambda b, t: (b, t, 0)), g_spec]
    args = [x, g.reshape(g.shape[0], 1, d) if l is not None else g.reshape(1, d)]
    if modulated:
        in_specs += [_mod_spec(sc, tm, d, False), _mod_spec(sh, tm, d, False)]
        args += [sc, sh]
    return pl.pallas_call(
        functools.partial(_norm_kernel, modulated=modulated),
        grid=(bm, tmx // tm),
        in_specs=in_specs,
        out_specs=pl.BlockSpec((None, tm, d), lambda b, t: (b, t, 0)),
        out_shape=jax.ShapeDtypeStruct((bm, tmx, d), out_dtype),
        compiler_params=_params(("arbitrary", "arbitrary")),
        name="norm",
    )(*args)


def _mm_kernel(*refs, row_offs, row_sizes, has_res):
    n_in = len(row_offs)
    x_refs = refs[:n_in]
    w_ref = refs[n_in]
    if has_res:
        res_ref, gate_ref, o_ref, wbf_ref = refs[n_in + 1:]
    else:
        o_ref, wbf_ref = refs[n_in + 1:]

    @pl.when(jnp.logical_and(pl.program_id(1) == 0, pl.program_id(2) == 0))
    def _():
        wbf_ref[...] = w_ref[...].astype(BF16)

    acc = None
    for x_ref, off, size in zip(x_refs, row_offs, row_sizes):
        part = jnp.dot(x_ref[...], wbf_ref[off:off + size, :], preferred_element_type=F32)
        acc = part if acc is None else acc + part
    if has_res:
        acc = res_ref[...] + gate_ref[...] * acc
    o_ref[...] = acc.astype(o_ref.dtype)


def _matmul(xs, w, l, n_tiles, tn, tm_pref, res=None, gate=None, name="mm"):
    bm, tmx = xs[0].shape[:2]
    k = w.shape[1]
    sizes = [x.shape[2] for x in xs]
    offs = [sum(sizes[:i]) for i in range(len(xs))]
    assert sum(sizes) == k
    tm = _tile(tmx, tm_pref, SUBLANES)
    in_specs = [pl.BlockSpec((None, tm, s), lambda n, b, t: (b, t, 0)) for s in sizes]
    in_specs.append(pl.BlockSpec((None, k, tn), lambda n, b, t: (l, 0, n)))
    args = list(xs) + [w]
    if res is not None:
        in_specs.append(pl.BlockSpec((None, tm, tn), lambda n, b, t: (b, t, n)))
        in_specs.append(_mod_spec(gate, tm, tn, True))
        args += [res, gate]
    return pl.pallas_call(
        functools.partial(_mm_kernel, row_offs=tuple(offs), row_sizes=tuple(sizes),
                          has_res=res is not None),
        grid=(n_tiles, bm, tmx // tm),
        in_specs=in_specs,
        out_specs=pl.BlockSpec((None, tm, tn), lambda n, b, t: (b, t, n)),
        out_shape=jax.ShapeDtypeStruct((bm, tmx, n_tiles * tn), F32),
        scratch_shapes=[pltpu.VMEM((k, tn), BF16)],
        compiler_params=_params(("arbitrary", "arbitrary", "arbitrary")),
        name=name,
    )(*args)


def _gates_kernel(h_ref, wg_ref, o_ref):
    o_ref[...] = lax.dot_general(wg_ref[...].astype(BF16), h_ref[...], NT_DIMS,
                                 preferred_element_type=F32)


def _gates(h, wg_t, l):
    bm, tmx, d = h.shape
    tm = _tile(tmx, 1024)
    return pl.pallas_call(
        _gates_kernel,
        grid=(bm, tmx // tm),
        in_specs=[pl.BlockSpec((None, tm, d), lambda b, t: (b, t, 0)),
                  pl.BlockSpec((None, GATE_ROWS, d), lambda b, t: (l, 0, 0))],
        out_specs=pl.BlockSpec((None, GATE_ROWS, tm), lambda b, t: (b, 0, t)),
        out_shape=jax.ShapeDtypeStruct((bm, GATE_ROWS, tmx), F32),
        compiler_params=_params(("arbitrary", "arbitrary")),
        name="gates",
    )(h, wg_t)


def _lambda(lam_ref, lam_init):
    lp = lam_ref[...]
    a = jnp.exp(jnp.sum(lp[0:1] * lp[1:2], axis=1, keepdims=True))
    b = jnp.exp(jnp.sum(lp[2:3] * lp[3:4], axis=1, keepdims=True))
    return a - b + lam_init


def _online_softmax_step(s, v_bf, m_scr, l_scr, acc_scr):
    m_prev = m_scr[...]
    m_new = jnp.maximum(m_prev, jnp.max(s, axis=1, keepdims=True))
    alpha = jnp.exp(m_prev - m_new)
    p = jnp.exp(s - m_new)
    l_scr[...] = alpha * l_scr[...] + jnp.sum(p, axis=1, keepdims=True)
    acc_scr[...] = alpha * acc_scr[...] + jnp.dot(p.astype(BF16), v_bf, preferred_element_type=F32)
    m_scr[...] = m_new


def _attn_prompt_kernel(q_ref, k_ref, v_ref, lam_ref, g_ref, o_ref,
                        kbf, vbf, qst, m_scr, l_scr, acc_scr, *, tq, lam_init):
    qi = pl.program_id(2)

    @pl.when(qi == 0)
    def _():
        kbf[...] = k_ref[...].astype(BF16)
        vbf[...] = v_ref[...].astype(BF16)

    q = q_ref[...] * (DA_QK ** -0.5)
    lane = lax.broadcasted_iota(jnp.int32, q.shape, 1)
    qst[0:tq, :] = jnp.where(lane < DA_QK, q, 0.0).astype(BF16)
    qst[tq:2 * tq, :] = jnp.where(lane >= DA_QK, q, 0.0).astype(BF16)
    m_scr[...] = jnp.full(m_scr.shape, -jnp.inf, F32)
    l_scr[...] = jnp.zeros(l_scr.shape, F32)
    acc_scr[...] = jnp.zeros(acc_scr.shape, F32)

    def step(j, masked):
        r0 = pl.multiple_of(j * tq, tq)
        s = lax.dot_general(qst[...], kbf[pl.ds(r0, tq), :], NT_DIMS, preferred_element_type=F32)
        if masked:
            row = lax.broadcasted_iota(jnp.int32, s.shape, 0)
            col = lax.broadcasted_iota(jnp.int32, s.shape, 1)
            row = jnp.where(row >= tq, row - tq, row)
            s = jnp.where(col <= row, s, NEG_INF)
        _online_softmax_step(s, vbf[pl.ds(r0, tq), :], m_scr, l_scr, acc_scr)

    def body(j, carry):
        step(j, False)
        return carry

    lax.fori_loop(0, qi, body, 0)
    step(qi, True)

    o = acc_scr[...] / l_scr[...]
    oa = o[0:tq] - _lambda(lam_ref, lam_init) * o[tq:2 * tq]
    o_ref[...] = (_rms(oa) * g_ref[...] * (1.0 - lam_init)).astype(o_ref.dtype)


def _attn_prompt(proj, lam_p, da_g, l, lam_init):
    b, t, _ = proj.shape
    tq = _tile(t, 256)
    hq, hk, hv = COL_AQ // HEAD_DIM, COL_AK // HEAD_DIM, COL_AV // HEAD_DIM
    return pl.pallas_call(
        functools.partial(_attn_prompt_kernel, tq=tq, lam_init=lam_init),
        grid=(b, N_DA, t // tq),
        in_specs=[
            pl.BlockSpec((None, tq, HEAD_DIM), lambda b, h, i: (b, i, hq + h)),
            pl.BlockSpec((None, t, HEAD_DIM), lambda b, h, i: (b, 0, hk + h)),
            pl.BlockSpec((None, t, HEAD_DIM), lambda b, h, i: (b, 0, hv + h)),
            pl.BlockSpec((None, 4, DA_QK), lambda b, h, i: (l, 0, 0)),
            pl.BlockSpec((None, 1, HEAD_DIM), lambda b, h, i: (l, 0, h)),
        ],
        out_specs=pl.BlockSpec((None, tq, HEAD_DIM), lambda b, h, i: (b, i, h)),
        out_shape=jax.ShapeDtypeStruct((b, t, DA_W), BF16),
        scratch_shapes=[
            pltpu.VMEM((t, HEAD_DIM), BF16), pltpu.VMEM((t, HEAD_DIM), BF16),
            pltpu.VMEM((2 * tq, HEAD_DIM), BF16),
            pltpu.VMEM((2 * tq, 1), F32), pltpu.VMEM((2 * tq, 1), F32),
            pltpu.VMEM((2 * tq, HEAD_DIM), F32),
        ],
        compiler_params=_params(("arbitrary", "arbitrary", "arbitrary")),
        name="attn_prompt",
    )(proj, proj, proj, lam_p, da_g)


DEC_ROWS = N_DA * 2 * SUBLANES


def _attn_decode_kernel(pt_ref, q_ref, kn_ref, vn_ref, lam_ref, g_ref, *rest,
                        pages, n_steps, t_new, lam_init):
    k_refs = rest[:pages]
    v_refs = rest[pages:2 * pages]
    o_ref, qbd, m_scr, l_scr, acc_scr, kpad, vpad = rest[2 * pages:]
    step = pl.program_id(1)

    @pl.when(step == 0)
    def _():
        q = q_ref[...] * (DA_QK ** -0.5)
        qt = jnp.concatenate([q] * (DEC_ROWS // SUBLANES), axis=0)
        row = lax.broadcasted_iota(jnp.int32, qt.shape, 0)
        col = lax.broadcasted_iota(jnp.int32, qt.shape, 1)
        same_head = (col >> 7) == (row >> 4)
        same_map = ((col >> 6) & 1) == ((row >> 3) & 1)
        qbd[...] = jnp.where(jnp.logical_and(same_head, same_map), qt, 0.0).astype(BF16)
        m_scr[...] = jnp.full(m_scr.shape, -jnp.inf, F32)
        l_scr[...] = jnp.zeros(l_scr.shape, F32)
        acc_scr[...] = jnp.zeros(acc_scr.shape, F32)

    qb = qbd[...]
    scores = [lax.dot_general(qb, k_refs[p][...].astype(BF16), NT_DIMS, preferred_element_type=F32)
              for p in range(pages)]
    m_prev = m_scr[...]
    m_new = m_prev
    for s in scores:
        m_new = jnp.maximum(m_new, jnp.max(s, axis=1, keepdims=True))
    alpha = jnp.exp(m_prev - m_new)
    l_new = alpha * l_scr[...]
    pv = None
    for p in range(pages):
        e = jnp.exp(scores[p] - m_new)
        l_new = l_new + jnp.sum(e, axis=1, keepdims=True)
        d = jnp.dot(e.astype(BF16), v_refs[p][...].astype(BF16), preferred_element_type=F32)
        pv = d if pv is None else pv + d
    l_scr[...] = l_new
    acc_scr[...] = alpha * acc_scr[...] + pv
    m_scr[...] = m_new

    @pl.when(step == n_steps - 1)
    def _():
        kpad[...] = jnp.zeros(kpad.shape, F32)
        vpad[...] = jnp.zeros(vpad.shape, F32)
        kpad[0:t_new, :] = kn_ref[...]
        vpad[0:t_new, :] = vn_ref[...]
        s = lax.dot_general(qb, kpad[...].astype(BF16), NT_DIMS, preferred_element_type=F32)
        row = lax.broadcasted_iota(jnp.int32, s.shape, 0)
        col = lax.broadcasted_iota(jnp.int32, s.shape, 1)
        s = jnp.where(col <= (row & (SUBLANES - 1)), s, NEG_INF)
        _online_softmax_step(s, vpad[...].astype(BF16), m_scr, l_scr, acc_scr)

        lam = _lambda(lam_ref, lam_init)
        for h in range(N_DA):
            r0 = h * 2 * SUBLANES
            cols = slice(h * HEAD_DIM, (h + 1) * HEAD_DIM)
            o1 = acc_scr[r0:r0 + SUBLANES, cols] / l_scr[r0:r0 + SUBLANES, :]
            o2 = acc_scr[r0 + SUBLANES:r0 + 2 * SUBLANES, cols] / l_scr[r0 + SUBLANES:r0 + 2 * SUBLANES, :]
            oa = o1 - lam * o2
            o_ref[:, cols] = _rms(oa) * g_ref[:, cols] * (1.0 - lam_init)


def _attn_decode(proj, cache_k, cache_v, page_table, lam_p, da_g, l, lam_init):
    b, t_new, _ = proj.shape
    assert t_new == SUBLANES
    n_pages = page_table.shape[1]
    pages = _tile(n_pages, 8, 1)
    n_steps = n_pages // pages
    n_pool = cache_k.shape[1]
    ck = cache_k.reshape(cache_k.shape[0], n_pool, PAGE, DA_W)
    cv = cache_v.reshape(cache_v.shape[0], n_pool, PAGE, DA_W)

    def page_spec(p):
        return pl.BlockSpec((None, None, PAGE, DA_W),
                            lambda b, s, pt: (l, pt[b * n_pages + s * pages + p], 0, 0))

    grid_spec = pltpu.PrefetchScalarGridSpec(
        num_scalar_prefetch=1,
        grid=(b, n_steps),
        in_specs=[
            pl.BlockSpec((None, t_new, DA_W), lambda b, s, pt: (b, 0, COL_AQ // DA_W)),
            pl.BlockSpec((None, t_new, DA_W), lambda b, s, pt: (b, 0, COL_AK // DA_W)),
            pl.BlockSpec((None, t_new, DA_W), lambda b, s, pt: (b, 0, COL_AV // DA_W)),
            pl.BlockSpec((None, 4, DA_QK), lambda b, s, pt: (l, 0, 0)),
            pl.BlockSpec((None, 1, DA_W), lambda b, s, pt: (l, 0, 0)),
        ] + [page_spec(p) for p in range(pages)] * 2,
        out_specs=pl.BlockSpec((None, t_new, DA_W), lambda b, s, pt: (b, 0, 0)),
        scratch_shapes=[
            pltpu.VMEM((DEC_ROWS, DA_W), BF16),
            pltpu.VMEM((DEC_ROWS, 1), F32), pltpu.VMEM((DEC_ROWS, 1), F32),
            pltpu.VMEM((DEC_ROWS, DA_W), F32),
            pltpu.VMEM((PAGE, DA_W), F32), pltpu.VMEM((PAGE, DA_W), F32),
        ],
    )
    return pl.pallas_call(
        functools.partial(_attn_decode_kernel, pages=pages, n_steps=n_steps, t_new=t_new,
                          lam_init=lam_init),
        grid_spec=grid_spec,
        out_shape=jax.ShapeDtypeStruct((b, t_new, DA_W), F32),
        compiler_params=_params(("arbitrary", "arbitrary")),
        name="attn_decode",
    )(page_table.reshape(-1), proj, proj, proj, lam_p, da_g,
      *([ck] * pages), *([cv] * pages))


def _rotary(x, cos2, sin2):
    return x * cos2 + pltpu.roll(x, DA_QK, 1) * sin2


def _ret_kernel(q_ref, k_ref, v_ref, gate_ref, cos_ref, sin_ref, lg_ref, s0_ref, gn_ref,
                o_ref, sout_ref, s_scr, *, n_chunks, l_true):
    L = CHUNK
    lg = lg_ref[...][:, 0:1]
    ti = lax.broadcasted_iota(jnp.int32, (L, L), 0)
    si = lax.broadcasted_iota(jnp.int32, (L, L), 1)
    rel = (ti - si).astype(F32)
    dmat = jnp.where(rel >= 0, jnp.exp(lg * jnp.maximum(rel, 0.0)), 0.0)
    idx = lax.broadcasted_iota(jnp.int32, (L, 1), 0).astype(F32)
    q_dec = jnp.exp(lg * (idx + 1.0))
    k_dec = jnp.exp(lg * (l_true - 1.0 - idx))
    s_dec = jnp.exp(lg * float(l_true))
    s_scr[...] = s0_ref[...]

    def body(c, carry):
        r0 = pl.multiple_of(c * L, L)
        rows = pl.ds(r0, L)
        cos2, sin2 = cos_ref[rows, :], sin_ref[rows, :]
        q = _rotary(q_ref[rows, :], cos2, sin2)
        k = _rotary(k_ref[rows, :], cos2, sin2) * (HEAD_DIM ** -0.5)
        q_bf, k_bf, v_bf = q.astype(BF16), k.astype(BF16), v_ref[rows, :].astype(BF16)
        state = s_scr[...]
        scores = lax.dot_general(q_bf, k_bf, NT_DIMS, preferred_element_type=F32) * dmat
        inner = jnp.dot(scores.astype(BF16), v_bf, preferred_element_type=F32)
        cross = jnp.dot(q_bf, state.astype(BF16), preferred_element_type=F32) * q_dec
        kd_t = (k * k_dec).T.astype(BF16)
        s_scr[...] = state * s_dec + jnp.dot(kd_t, v_bf, preferred_element_type=F32)
        g = gate_ref[rows, :]
        y = _rms(inner + cross) * gn_ref[...]
        o_ref[rows, :] = (y.astype(F32) * (g * jax.nn.sigmoid(g))).astype(o_ref.dtype)
        return carry

    lax.fori_loop(0, n_chunks, body, 0)
    sout_ref[...] = s_scr[...]


def _retention(proj, cos2, sin2, lg_tab, s0, gn, l, l_true):
    b, t, _ = proj.shape
    n_chunks = t // CHUNK

    def col(c0):
        return pl.BlockSpec((None, t, HEAD_DIM), lambda b, h: (b, 0, c0 // HEAD_DIM + h))

    st_spec = pl.BlockSpec((None, None, HEAD_DIM, HEAD_DIM), lambda b, h: (b, h, 0, 0))
    return pl.pallas_call(
        functools.partial(_ret_kernel, n_chunks=n_chunks, l_true=l_true),
        grid=(b, N_RET),
        in_specs=[
            col(COL_RQ), col(COL_RK), col(COL_RV), col(COL_RG),
            pl.BlockSpec((t, HEAD_DIM), lambda b, h: (0, 0)),
            pl.BlockSpec((t, HEAD_DIM), lambda b, h: (0, 0)),
            pl.BlockSpec((None, 1, HEAD_DIM), lambda b, h: (h, 0, 0)),
            st_spec,
            pl.BlockSpec((None, 1, HEAD_DIM), lambda b, h: (l, 0, h)),
        ],
        out_specs=[pl.BlockSpec((None, t, HEAD_DIM), lambda b, h: (b, 0, h)), st_spec],
        out_shape=[jax.ShapeDtypeStruct((b, t, RET_W), BF16),
                   jax.ShapeDtypeStruct((b, N_RET, HEAD_DIM, HEAD_DIM), F32)],
        scratch_shapes=[pltpu.VMEM((HEAD_DIM, HEAD_DIM), F32)],
        compiler_params=_params(("arbitrary", "arbitrary")),
        name="retention",
    )(proj, proj, proj, proj, cos2, sin2, lg_tab, s0, gn)


def _row_to_col(row, eye):
    return jnp.sum(jnp.where(eye, row, 0.0), axis=1, keepdims=True)


def _cumsum_lanes(x, lane):
    n = x.shape[1]
    shift = 1
    while shift < n:
        x = x + jnp.where(lane >= shift, pltpu.roll(x, shift, 1), 0.0)
        shift *= 2
    return x


def _mlstm_kernel(u_ref, v_ref, og_ref, prev_ref, ig_ref, fg_ref, ib_ref, fb_ref, cw_ref, cb_ref,
                  wq_ref, wk_ref, c0_ref, n0_ref, m0_ref, gn_ref, skip_ref,
                  o_ref, cout_ref, nout_ref, mout_ref,
                  xp_scr, uc_scr, c_scr, n_scr, m_scr, *, t, n_chunks, l_true):
    L = CHUNK
    halo = SUBLANES

    xp_scr[0:halo, :] = prev_ref[...]
    xp_scr[halo:halo + t, :] = u_ref[...]
    conv = cb_ref[...]
    for j in range(ML_CONV):
        start = halo - (ML_CONV - 1) + j
        conv = conv + xp_scr[start:start + t, :] * cw_ref[j:j + 1, :]
    uc_scr[...] = conv * jax.nn.sigmoid(conv)

    wq_bf = wq_ref[...].astype(BF16)
    wk_bf = wk_ref[...].astype(BF16)
    c_scr[...] = c0_ref[...]
    n_scr[...] = n0_ref[...]
    m_scr[...] = m0_ref[...]

    ti = lax.broadcasted_iota(jnp.int32, (L, L), 0)
    si = lax.broadcasted_iota(jnp.int32, (L, L), 1)
    eye = ti == si
    causal = si <= ti
    lane = lax.broadcasted_iota(jnp.int32, (1, L), 1)

    def body(c, carry):
        r0 = pl.multiple_of(c * L, L)
        rows = pl.ds(r0, L)
        uc = uc_scr[rows, :]
        uc_bf = uc.astype(BF16)
        q = jnp.dot(uc_bf, wq_bf, preferred_element_type=F32)
        k = jnp.dot(uc_bf, wk_bf, preferred_element_type=F32) * (HEAD_DIM ** -0.5)
        q_bf, k_bf, v_bf = q.astype(BF16), k.astype(BF16), v_ref[rows, :].astype(BF16)

        ig = ig_ref[:, rows] + ib_ref[...]
        fg = fg_ref[:, rows] + fb_ref[...]
        lf = -(jnp.maximum(-fg, 0.0) + jnp.log(1.0 + jnp.exp(-jnp.abs(fg))))
        b_row = _cumsum_lanes(lf, lane)
        b_col = _row_to_col(b_row, eye)
        m_old = m_scr[...]
        c_old = c_scr[...]
        n_old = n_scr[...]

        dlog = jnp.where(causal, b_col + (ig - b_row), -jnp.inf)
        inter = b_col + m_old
        m_row = jnp.maximum(inter, jnp.max(dlog, axis=1, keepdims=True))
        w = jnp.exp(dlog - m_row)
        s_in = jnp.exp(inter - m_row)
        a = w * lax.dot_general(q_bf, k_bf, NT_DIMS, preferred_element_type=F32)
        num = (jnp.dot(a.astype(BF16), v_bf, preferred_element_type=F32)
               + jnp.dot(q_bf, c_old.astype(BF16), preferred_element_type=F32) * s_in)
        qn = jnp.sum(q_bf.astype(F32) * n_old.astype(BF16).astype(F32), axis=1, keepdims=True)
        den = jnp.sum(a, axis=1, keepdims=True) + qn * s_in
        hh = num / jnp.maximum(jnp.abs(den), jnp.exp(-m_row))

        og = og_ref[rows, :]
        y = _rms(hh) * gn_ref[...] + skip_ref[...] * uc
        o_ref[rows, :] = (jax.nn.sigmoid(og) * y).astype(o_ref.dtype)

        b_last = jnp.sum(jnp.where(lane == l_true - 1, b_row, 0.0), axis=1, keepdims=True)
        dlast = jnp.where(lane < l_true, b_last - b_row + ig, -jnp.inf)
        m_new = jnp.maximum(b_last + m_old, jnp.max(dlast, axis=1, keepdims=True))
        ws_col = _row_to_col(jnp.exp(dlast - m_new), eye)
        s_old = jnp.exp(b_last + m_old - m_new)
        kw = k * ws_col
        c_scr[...] = c_old * s_old + jnp.dot(kw.T.astype(BF16), v_bf, preferred_element_type=F32)
        n_scr[...] = n_old * s_old + jnp.sum(kw, axis=0, keepdims=True)
        m_scr[...] = m_new
        return carry

    lax.fori_loop(0, n_chunks, body, 0)
    cout_ref[...] = c_scr[...]
    nout_ref[...] = n_scr[...]
    mout_ref[...] = m_scr[...]


def _mlstm(proj, prev, gates, gbias, cw, cb, wq, wk, c0, n0, m0, gn, skip, l, l_true):
    b, t, _ = proj.shape
    n_chunks = t // CHUNK

    def col(c0_):
        return pl.BlockSpec((None, t, HEAD_DIM), lambda b, h: (b, 0, c0_ // HEAD_DIM + h))

    def per_head(rows):
        return pl.BlockSpec((None, rows, HEAD_DIM), lambda b, h: (l, 0, h))

    c_spec = pl.BlockSpec((None, None, HEAD_DIM, HEAD_DIM), lambda b, h: (b, h, 0, 0))
    n_spec = pl.BlockSpec((None, None, 1, HEAD_DIM), lambda b, h: (b, h, 0, 0))
    m_spec = pl.BlockSpec((None, None, 1, 1), lambda b, h: (b, h, 0, 0))
    w_spec = pl.BlockSpec((None, None, HEAD_DIM, HEAD_DIM), lambda b, h: (l, h, 0, 0))
    return pl.pallas_call(
        functools.partial(_mlstm_kernel, t=t, n_chunks=n_chunks, l_true=l_true),
        grid=(b, N_ML),
        in_specs=[
            col(COL_MU), col(COL_MV), col(COL_MO),
            pl.BlockSpec((None, SUBLANES, HEAD_DIM), lambda b, h: (b, 0, h)),
            pl.BlockSpec((None, None, 1, t), lambda b, h: (b, h, 0, 0)),
            pl.BlockSpec((None, None, 1, t), lambda b, h: (b, N_ML + h, 0, 0)),
            pl.BlockSpec((None, None, 1, 1), lambda b, h: (l, h, 0, 0)),
            pl.BlockSpec((None, None, 1, 1), lambda b, h: (l, N_ML + h, 0, 0)),
            per_head(ML_CONV), per_head(1),
            w_spec, w_spec, c_spec, n_spec, m_spec,
            per_head(1), per_head(1),
        ],
        out_specs=[pl.BlockSpec((None, t, HEAD_DIM), lambda b, h: (b, 0, h)),
                   c_spec, n_spec, m_spec],
        out_shape=[jax.ShapeDtypeStruct((b, t, ML_W), BF16),
                   jax.ShapeDtypeStruct((b, N_ML, HEAD_DIM, HEAD_DIM), F32),
                   jax.ShapeDtypeStruct((b, N_ML, 1, HEAD_DIM), F32),
                   jax.ShapeDtypeStruct((b, N_ML, 1, 1), F32)],
        scratch_shapes=[
            pltpu.VMEM((SUBLANES + t, HEAD_DIM), F32), pltpu.VMEM((t, HEAD_DIM), F32),
            pltpu.VMEM((HEAD_DIM, HEAD_DIM), F32), pltpu.VMEM((1, HEAD_DIM), F32),
            pltpu.VMEM((1, 1), F32),
        ],
        compiler_params=_params(("arbitrary", "arbitrary")),
        name="mlstm",
    )(proj, proj, proj, prev, gates[:, :, None, :], gates[:, :, None, :], gbias, gbias,
      cw, cb, wq, wk, c0, n0, m0, gn, skip)


def _conv_gate_kernel(ug_ref, uv_ref, hg_ref, hv_ref, pg_ref, pv_ref, wg_ref, wv_ref,
                      bg_ref, bv_ref, o_ref, sg, sv, *, tt):
    first = pl.program_id(1) == 0
    halo = SUBLANES

    def conv(u_ref, h_ref, p_ref, w_ref, b_ref, scr):
        @pl.when(first)
        def _():
            scr[0:halo, :] = p_ref[...]

        @pl.when(jnp.logical_not(first))
        def _():
            scr[0:halo, :] = h_ref[...]

        scr[halo:halo + tt, :] = u_ref[...]
        y = b_ref[...]
        for j in range(FFN_CONV):
            start = halo - (FFN_CONV - 1) + j
            y = y + scr[start:start + tt, :] * w_ref[j:j + 1, :]
        return y

    gate = conv(ug_ref, hg_ref, pg_ref, wg_ref, bg_ref, sg)
    val = conv(uv_ref, hv_ref, pv_ref, wv_ref, bv_ref, sv)
    o_ref[...] = (gate * jax.nn.sigmoid(gate) * val).astype(o_ref.dtype)


def _conv_gate(up, prev, cw, cb, l):
    b, t, f2 = up.shape
    f = f2 // 2
    tt = _tile(t, 512, SUBLANES)
    tn = _tile(f, 512)
    nf = f // tn
    rb = tt // SUBLANES

    def main(off):
        return pl.BlockSpec((None, tt, tn), lambda b, i, n: (b, i, n + off))

    def halo(off):
        return pl.BlockSpec((None, SUBLANES, tn),
                            lambda b, i, n: (b, jnp.maximum(i * rb - 1, 0), n + off))

    def prev_spec(off):
        return pl.BlockSpec((None, SUBLANES, tn), lambda b, i, n: (b, 0, n + off))

    def w_spec(rows, off):
        return pl.BlockSpec((None, rows, tn), lambda b, i, n: (l, 0, n + off))

    return pl.pallas_call(
        functools.partial(_conv_gate_kernel, tt=tt),
        grid=(b, t // tt, nf),
        in_specs=[main(0), main(nf), halo(0), halo(nf), prev_spec(0), prev_spec(nf),
                  w_spec(FFN_CONV, 0), w_spec(FFN_CONV, nf), w_spec(1, 0), w_spec(1, nf)],
        out_specs=pl.BlockSpec((None, tt, tn), lambda b, i, n: (b, i, n)),
        out_shape=jax.ShapeDtypeStruct((b, t, f), BF16),
        scratch_shapes=[pltpu.VMEM((SUBLANES + tt, tn), F32), pltpu.VMEM((SUBLANES + tt, tn), F32)],
        compiler_params=_params(("arbitrary", "arbitrary", "arbitrary")),
        name="conv_gate",
    )(up, up, up, up, prev, prev, cw, cw, cb, cb)


def _rope_tables(pos, rows):
    half = HEAD_DIM // 2
    inv = ROPE_BASE ** (-jnp.arange(half, dtype=F32) / half)
    ang = pos.astype(F32)[:, None] * inv[None, :]
    cos, sin = jnp.cos(ang), jnp.sin(ang)
    cos2 = jnp.concatenate([cos, cos], axis=-1)
    sin2 = jnp.concatenate([-sin, sin], axis=-1)
    pad = rows - pos.shape[0]
    if pad:
        cos2 = jnp.pad(cos2, ((0, pad), (0, 0)))
        sin2 = jnp.pad(sin2, ((0, pad), (0, 0)))
    return cos2, sin2


def _prev_rows(buf):
    return jnp.pad(buf, ((0, 0), (SUBLANES - buf.shape[1], 0), (0, 0)))


def _layer(l, lam_init, x, mods, seq_shape, st, w, attn_fn, rope, lg_tab):
    b, t = seq_shape
    bm, tmx, d = x.shape
    sh1, sc1, g1, sh2, sc2, g2 = mods
    t_pad = -(-t // CHUNK) * CHUNK
    l_true = CHUNK if t % CHUNK == 0 else t

    h = _norm(x, w["g_mix"], l, sc1, sh1)
    proj = _matmul([h], w["w_in"], l, PROJ_W // 512, 512, 1024, name="in_proj")
    gates = _gates(h, w["wg_t"], l)

    proj_seq = proj.reshape(b, t, PROJ_W)
    oa = attn_fn(proj_seq)
    if (bm, tmx) != (b, t):
        gates = gates.reshape(GATE_ROWS, b, t).transpose(1, 0, 2)
    proj_pad = proj_seq
    if t_pad != t:
        proj_pad = jnp.pad(proj_seq, ((0, 0), (0, t_pad - t), (0, 0)))
        gates = jnp.pad(gates, ((0, 0), (0, 0), (0, t_pad - t)))

    ro, ret_s = _retention(proj_pad, rope[0], rope[1], lg_tab, st["ret_s"], w["ret_norm_g"], l, l_true)
    mo, ml_c, ml_n, ml_m = _mlstm(
        proj_pad, _prev_rows(st["ml_conv"]), gates, w["gbias"], w["ml_conv_w"], w["ml_conv_b"],
        w["ml_wq"], w["ml_wk"], st["ml_c"], st["ml_n"][:, :, None, :], st["ml_m"][:, :, None, None],
        w["ml_norm_g"], w["ml_skip"], l, l_true)
    if t_pad != t:
        ro, mo = ro[:, :t], mo[:, :t]
    mix = [a.astype(BF16).reshape(bm, tmx, a.shape[-1]) for a in (oa, ro, mo)]
    x = _matmul(mix, w["w_out"], l, d // _tile(d, 512), _tile(d, 512), 1024,
                res=x, gate=g1, name="out_proj")

    h2 = _norm(x, w["g_ffn"], l, sc2, sh2)
    f2 = w["ffn_w_up"].shape[2]
    up = _matmul([h2], w["ffn_w_up"], l, f2 // _tile(f2, 512), _tile(f2, 512), 1024, name="ffn_up")
    up_seq = up.reshape(b, t, f2)
    act = _conv_gate(up_seq, _prev_rows(st["ffn_conv"]), w["ffn_conv_w"], w["ffn_conv_b"], l)
    x = _matmul([act.reshape(bm, tmx, f2 // 2)], w["ffn_w_down"], l, d // _tile(d, 512),
                _tile(d, 512), 512, res=x, gate=g2, name="ffn_down")

    new_st = dict(
        ret_s=ret_s, ml_c=ml_c, ml_n=ml_n[:, :, 0, :], ml_m=ml_m[:, :, 0, 0],
        ml_conv=proj_seq[:, t - (ML_CONV - 1):, COL_MU:COL_MU + ML_W],
        ffn_conv=up_seq[:, t - (FFN_CONV - 1):, :])
    k_new = proj_seq[:, :, COL_AK:COL_AK + DA_W].reshape(b, t, N_DA, HEAD_DIM)
    v_new = proj_seq[:, :, COL_AV:COL_AV + DA_W].reshape(b, t, N_DA, HEAD_DIM)
    return x, k_new, v_new, new_st


def kernel(x_prompt, x_sample, c_prompt, c_sample, cache_k, cache_v, page_table, state_ret, state_ml_c, state_ml_n, state_ml_m, state_ml_conv, state_ffn_conv, w_ada, b_ada, g_mix, w_in, lam_q1, lam_k1, lam_q2, lam_k2, da_norm_g, ret_norm_g, ml_conv_w, ml_conv_b, ml_wq, ml_wk, ml_b_i, ml_b_f, ml_norm_g, ml_skip, w_out, g_ffn, ffn_w_up, ffn_conv_w, ffn_conv_b, ffn_w_down, g_final):
    bp, tp, d = x_prompt.shape
    bs, ts, _ = x_sample.shape
    depth = w_in.shape[0]
    f2 = ffn_w_up.shape[2]
    past = page_table.shape[1] * PAGE

    def per_layer_rows(a):
        return a.reshape(depth, 1, a.shape[-1])

    zeros8 = jnp.zeros((depth, GATE_ROWS - N_GATES), F32)
    w = dict(
        g_mix=g_mix, g_ffn=g_ffn, w_in=w_in, w_out=w_out, ffn_w_up=ffn_w_up, ffn_w_down=ffn_w_down,
        wg_t=jnp.pad(jnp.swapaxes(w_in[:, :, PROJ_W:], 1, 2), ((0, 0), (0, GATE_ROWS - N_GATES), (0, 0))),
        gbias=jnp.concatenate([ml_b_i, ml_b_f, zeros8], axis=1)[:, :, None, None],
        lam_p=jnp.stack([lam_q1, lam_k1, lam_q2, lam_k2], axis=1),
        da_norm_g=per_layer_rows(da_norm_g), ret_norm_g=per_layer_rows(ret_norm_g),
        ml_conv_w=ml_conv_w, ml_conv_b=per_layer_rows(ml_conv_b), ml_wq=ml_wq, ml_wk=ml_wk,
        ml_norm_g=per_layer_rows(ml_norm_g), ml_skip=per_layer_rows(ml_skip),
        ffn_conv_w=ffn_conv_w, ffn_conv_b=per_layer_rows(ffn_conv_b))

    n_c = bp + bs
    c_rows = -(-n_c // SUBLANES) * SUBLANES
    c_all = jnp.pad(jnp.concatenate([c_prompt, c_sample], axis=0), ((0, c_rows - n_c), (0, 0)))
    mod = _ada(c_all, w_ada, b_ada)

    lg_np = np.log(1.0 - 2.0 ** (-5.0 - np.arange(N_RET, dtype=np.float64)))
    lg_tab = jnp.asarray(np.broadcast_to(lg_np[:, None, None], (N_RET, 1, HEAD_DIM)), F32)
    tp_pad = -(-tp // CHUNK) * CHUNK
    ts_pad = -(-ts // CHUNK) * CHUNK
    rope_p = _rope_tables(jnp.arange(tp), tp_pad)
    rope_s = _rope_tables(past + jnp.arange(ts), ts_pad)

    zero_st = dict(
        ret_s=jnp.zeros((bp, N_RET, HEAD_DIM, HEAD_DIM), F32),
        ml_c=jnp.zeros((bp, N_ML, HEAD_DIM, HEAD_DIM), F32),
        ml_n=jnp.zeros((bp, N_ML, HEAD_DIM), F32),
        ml_m=jnp.zeros((bp, N_ML), F32),
        ml_conv=jnp.zeros((bp, ML_CONV - 1, ML_W), F32),
        ffn_conv=jnp.zeros((bp, FFN_CONV - 1, f2), F32))
    names = ("ret_s", "ml_c", "ml_n", "ml_m", "ml_conv", "ffn_conv")
    out_p = {n: [] for n in names}
    out_s = {n: [] for n in names}
    kp, vp, ksm, vsm = [], [], [], []
    yp = x_prompt
    ys = x_sample.reshape(1, bs * ts, d)
    for l in range(depth):
        lam_init = 0.8 - 0.6 * math.exp(-0.3 * l)
        parts = jnp.split(mod[l], 6, axis=-1)
        mods_p = [m[:bp, None, :] for m in parts]
        mods_s = [jnp.repeat(m[bp:n_c], ts, axis=0)[None] for m in parts]

        attn_p = functools.partial(_attn_prompt, lam_p=w["lam_p"], da_g=w["da_norm_g"], l=l,
                                   lam_init=lam_init)
        yp, k_new, v_new, st_p = _layer(l, lam_init, yp, mods_p, (bp, tp), zero_st, w, attn_p,
                                        rope_p, lg_tab)
        kp.append(k_new)
        vp.append(v_new)

        attn_s = functools.partial(_attn_decode, cache_k=cache_k, cache_v=cache_v,
                                   page_table=page_table, lam_p=w["lam_p"], da_g=w["da_norm_g"],
                                   l=l, lam_init=lam_init)
        st_in = dict(ret_s=state_ret[l], ml_c=state_ml_c[l], ml_n=state_ml_n[l], ml_m=state_ml_m[l],
                     ml_conv=state_ml_conv[l], ffn_conv=state_ffn_conv[l])
        ys, k_new_s, v_new_s, st_s = _layer(l, lam_init, ys, mods_s, (bs, ts), st_in, w, attn_s,
                                            rope_s, lg_tab)
        ksm.append(k_new_s)
        vsm.append(v_new_s)
        for n in names:
            out_p[n].append(st_p[n])
            out_s[n].append(st_s[n])

    y_prompt = _norm(yp, g_final, None, out_dtype=F32)
    y_sample = _norm(ys, g_final, None, out_dtype=F32).reshape(bs, ts, d)
    return (y_prompt, y_sample,
            jnp.stack(kp), jnp.stack(vp), jnp.stack(ksm), jnp.stack(vsm),
            jnp.stack(out_p["ret_s"]), jnp.stack(out_s["ret_s"]),
            jnp.stack(out_p["ml_c"]), jnp.stack(out_s["ml_c"]),
            jnp.stack(out_p["ml_n"]), jnp.stack(out_s["ml_n"]),
            jnp.stack(out_p["ml_m"]), jnp.stack(out_s["ml_m"]),
            jnp.stack(out_p["ml_conv"]), jnp.stack(out_s["ml_conv"]),
            jnp.stack(out_p["ffn_conv"]), jnp.stack(out_s["ffn_conv"]))
```

```python
import functools
import math

import numpy as np
import jax
import jax.numpy as jnp
from jax import lax
from jax.experimental import pallas as pl
from jax.experimental.pallas import tpu as pltpu

F32 = jnp.float32
BF16 = jnp.bfloat16

HEAD_DIM = 128
DA_QK = HEAD_DIM // 2
N_DA = 8
N_RET = 4
N_ML = 4
DA_W = N_DA * HEAD_DIM
RET_W = N_RET * HEAD_DIM
ML_W = N_ML * HEAD_DIM
ML_CONV = 4
FFN_CONV = 3
CHUNK = 128
PAGE = 128
RMS_EPS = 1e-6
ROPE_BASE = 10000.0
NEG_INF = -1e30

W_COL_Q, W_COL_K, W_COL_V = 0, DA_W, 2 * DA_W
W_COL_RET = 3 * DA_W
W_COL_ML = W_COL_RET + 4 * RET_W
W_COL_GATES = W_COL_ML + 3 * ML_W
COL_RQ, COL_RK, COL_RV, COL_RG = 0, RET_W, 2 * RET_W, 3 * RET_W
COL_MU, COL_MV, COL_MO = 0, ML_W, 2 * ML_W
N_GATES = 2 * N_ML
GATE_ROWS = 16

SUBLANES = 8
LANES = 128
VMEM_LIMIT = 56 * 1024 * 1024

NT_DIMS = (((1,), (1,)), ((), ()))
TN_DIMS = (((0,), (0,)), ((), ()))


def _tile(n, pref, mult=LANES):
    if n <= pref:
        return n
    t = (pref // mult) * mult
    while t >= mult:
        if n % t == 0:
            return t
        t -= mult
    return n


def _params(sem):
    return pltpu.CompilerParams(dimension_semantics=sem, vmem_limit_bytes=VMEM_LIMIT)


def _rms(x):
    return x * lax.rsqrt(jnp.mean(x * x, axis=-1, keepdims=True) + RMS_EPS)


def _ada_kernel(c_ref, w_ref, b_ref, o_ref):
    c = c_ref[...]
    a = (c * jax.nn.sigmoid(c)).astype(BF16)
    o_ref[...] = jnp.dot(a, w_ref[...].astype(BF16), preferred_element_type=F32) + b_ref[...]


def _ada(c_all, w_ada, b_ada):
    depth, d, n = w_ada.shape
    rows = c_all.shape[0]
    tn = _tile(n, 1024)
    return pl.pallas_call(
        _ada_kernel,
        grid=(depth, n // tn),
        in_specs=[
            pl.BlockSpec((rows, d), lambda l, j: (0, 0)),
            pl.BlockSpec((None, d, tn), lambda l, j: (l, 0, j)),
            pl.BlockSpec((None, 1, tn), lambda l, j: (l, 0, j)),
        ],
        out_specs=pl.BlockSpec((None, rows, tn), lambda l, j: (l, 0, j)),
        out_shape=jax.ShapeDtypeStruct((depth, rows, n), F32),
        compiler_params=_params(("arbitrary", "arbitrary")),
        name="ada",
    )(c_all, w_ada, b_ada.reshape(depth, 1, n))


def _norm_kernel(*refs, modulated):
    if modulated:
        x_ref, g_ref, sc_ref, sh_ref, o_ref = refs
    else:
        x_ref, g_ref, o_ref = refs
    y = _rms(x_ref[...]) * g_ref[...]
    if modulated:
        y = y * (1.0 + sc_ref[...]) + sh_ref[...]
    o_ref[...] = y.astype(o_ref.dtype)


def _mod_spec(mod, tm, tn, with_n):
    per_row = mod.shape[1] != 1
    rows = tm if per_row else 1
    if with_n:
        return pl.BlockSpec((None, rows, tn), lambda n, b, t: (b, t if per_row else 0, n))
    return pl.BlockSpec((None, rows, tn), lambda b, t: (b, t if per_row else 0, 0))


def _norm(x, g, l, sc=None, sh=None, out_dtype=BF16):
    bm, tmx, d = x.shape
    tm = _tile(tmx, 512, SUBLANES)
    modulated = sc is not None
    g_spec = (pl.BlockSpec((None, 1, d), lambda b, t: (l, 0, 0)) if l is not None
              else pl.BlockSpec((1, d), lambda b, t: (0, 0)))
    in_specs = [pl.BlockSpec((None, tm, d), lambda b, t: (b, t, 0)), g_spec]
    args = [x, g.reshape(g.shape[0], 1, d) if l is not None else g.reshape(1, d)]
    if modulated:
        in_specs += [_mod_spec(sc, tm, d, False), _mod_spec(sh, tm, d, False)]
        args += [sc, sh]
    return pl.pallas_call(
        functools.partial(_norm_kernel, modulated=modulated),
        grid=(bm, tmx // tm),
        in_specs=in_specs,
        out_specs=pl.BlockSpec((None, tm, d), lambda b, t: (b, t, 0)),
        out_shape=jax.ShapeDtypeStruct((bm, tmx, d), out_dtype),
        compiler_params=_params(("arbitrary", "arbitrary")),
        name="norm",
    )(*args)


def _mm_kernel(*refs, row_offs, row_sizes, has_res, w_transposed):
    n_in = len(row_offs)
    x_refs = refs[:n_in]
    w_ref = refs[n_in]
    o_ref, wbf_ref = refs[-2:]
    if has_res:
        res_ref, gate_ref = refs[n_in + 1:n_in + 3]

    @pl.when(jnp.logical_and(pl.program_id(1) == 0, pl.program_id(2) == 0))
    def _():
        wbf_ref[...] = w_ref[...].astype(BF16)

    acc = None
    for x_ref, off, size in zip(x_refs, row_offs, row_sizes):
        if w_transposed:
            part = lax.dot_general(x_ref[...], wbf_ref[:, off:off + size], NT_DIMS,
                                   preferred_element_type=F32)
        else:
            part = jnp.dot(x_ref[...], wbf_ref[off:off + size, :], preferred_element_type=F32)
        acc = part if acc is None else acc + part
    if has_res:
        acc = res_ref[...] + gate_ref[...] * acc
    o_ref[...] = acc.astype(o_ref.dtype)


def _matmul(xs, w, l, n_tiles, tn, tm_pref, res=None, gate=None, w_tile=lambda n: n,
            w_transposed=False, stack=None, name="mm"):
    bm, tmx = xs[0].shape[:2]
    k = w.shape[2] if w_transposed else w.shape[1]
    sizes = [x.shape[2] for x in xs]
    offs = [sum(sizes[:i]) for i in range(len(xs))]
    assert sum(sizes) == k
    tm = _tile(tmx, tm_pref, SUBLANES)
    in_specs = [pl.BlockSpec((None, tm, s), lambda n, b, t: (b, t, 0)) for s in sizes]
    if w_transposed:
        in_specs.append(pl.BlockSpec((None, tn, k), lambda n, b, t: (l, w_tile(n), 0)))
    else:
        in_specs.append(pl.BlockSpec((None, k, tn), lambda n, b, t: (l, 0, w_tile(n))))
    args = list(xs) + [w]
    if res is not None:
        in_specs.append(pl.BlockSpec((None, tm, tn), lambda n, b, t: (b, t, n)))
        in_specs.append(_mod_spec(gate, tm, tn, True))
        args += [res, gate]
    out_specs = pl.BlockSpec((None, tm, tn), lambda n, b, t: (b, t, n))
    out_shape = jax.ShapeDtypeStruct((bm, tmx, n_tiles * tn), F32)
    aliases = {}
    if stack is not None:
        depth, buf = stack
        out_specs = pl.BlockSpec((None, None, tm, tn), lambda n, b, t: (l, b, t, n))
        out_shape = jax.ShapeDtypeStruct((depth, bm, tmx, n_tiles * tn), F32)
        if buf is not None:
            in_specs.append(pl.BlockSpec(memory_space=pl.ANY))
            args.append(buf)
            aliases = {len(args) - 1: 0}
    return pl.pallas_call(
        functools.partial(_mm_kernel, row_offs=tuple(offs), row_sizes=tuple(sizes),
                          has_res=res is not None, w_transposed=w_transposed),
        grid=(n_tiles, bm, tmx // tm),
        in_specs=in_specs,
        out_specs=out_specs,
        out_shape=out_shape,
        input_output_aliases=aliases,
        scratch_shapes=[pltpu.VMEM((tn, k) if w_transposed else (k, tn), BF16)],
        compiler_params=_params(("arbitrary", "arbitrary", "arbitrary")),
        name=name,
    )(*args)


def _gates_kernel(h_ref, wg_ref, o_ref):
    row = lax.broadcasted_iota(jnp.int32, wg_ref.shape, 0)
    wg = jnp.where(row < N_GATES, wg_ref[...], 0.0).astype(BF16)
    o_ref[...] = lax.dot_general(wg, h_ref[...], NT_DIMS, preferred_element_type=F32)


def _gates(h, w_in_t, l):
    bm, tmx, d = h.shape
    tm = _tile(tmx, 1024)
    assert W_COL_GATES % GATE_ROWS == 0 and w_in_t.shape[1] == W_COL_GATES + N_GATES
    return pl.pallas_call(
        _gates_kernel,
        grid=(bm, tmx // tm),
        in_specs=[pl.BlockSpec((None, tm, d), lambda b, t: (b, t, 0)),
                  pl.BlockSpec((None, GATE_ROWS, d), lambda b, t: (l, W_COL_GATES // GATE_ROWS, 0))],
        out_specs=pl.BlockSpec((None, GATE_ROWS, tm), lambda b, t: (b, 0, t)),
        out_shape=jax.ShapeDtypeStruct((bm, GATE_ROWS, tmx), F32),
        compiler_params=_params(("arbitrary", "arbitrary")),
        name="gates",
    )(h, w_in_t)


def _lambda(lam_ref, lam_init):
    lp = lam_ref[...]
    a = jnp.exp(jnp.sum(lp[0:1] * lp[1:2], axis=1, keepdims=True))
    b = jnp.exp(jnp.sum(lp[2:3] * lp[3:4], axis=1, keepdims=True))
    return a - b + lam_init


def _online_softmax_step(s, v_bf, m_scr, l_scr, acc_scr):
    m_prev = m_scr[...]
    m_new = jnp.maximum(m_prev, jnp.max(s, axis=1, keepdims=True))
    alpha = jnp.exp(m_prev - m_new)
    p = jnp.exp(s - m_new)
    l_scr[...] = alpha * l_scr[...] + jnp.sum(p, axis=1, keepdims=True)
    acc_scr[...] = alpha * acc_scr[...] + jnp.dot(p.astype(BF16), v_bf, preferred_element_type=F32)
    m_scr[...] = m_new


def _attn_prompt_block(qi, q_ref, kbf, vbf, lam, g_ref, o_ref, *, tq, lam_init):
    n0 = qi * tq
    q = q_ref[...] * (DA_QK ** -0.5)
    lane = lax.broadcasted_iota(jnp.int32, q.shape, 1)
    row = lax.broadcasted_iota(jnp.int32, (tq, tq), 0)
    col = lax.broadcasted_iota(jnp.int32, (tq, tq), 1)
    outs = []
    for first_map in (True, False):
        qm = jnp.where((lane < DA_QK) if first_map else (lane >= DA_QK), q, 0.0).astype(BF16)
        s_diag = lax.dot_general(qm, kbf[n0:n0 + tq, :], NT_DIMS, preferred_element_type=F32)
        s_diag = jnp.where(col <= row, s_diag, NEG_INF)
        m = jnp.max(s_diag, axis=1, keepdims=True)
        if qi > 0:
            s_past = lax.dot_general(qm, kbf[0:n0, :], NT_DIMS, preferred_element_type=F32)
            m = jnp.maximum(m, jnp.max(s_past, axis=1, keepdims=True))
        p_diag = jnp.exp(s_diag - m)
        denom = jnp.sum(p_diag, axis=1, keepdims=True)
        o = jnp.dot(p_diag.astype(BF16), vbf[n0:n0 + tq, :], preferred_element_type=F32)
        if qi > 0:
            p_past = jnp.exp(s_past - m)
            denom = denom + jnp.sum(p_past, axis=1, keepdims=True)
            o = o + jnp.dot(p_past.astype(BF16), vbf[0:n0, :], preferred_element_type=F32)
        outs.append(o / denom)
    oa = outs[0] - lam * outs[1]
    o_ref[...] = (_rms(oa) * g_ref[...] * (1.0 - lam_init)).astype(o_ref.dtype)


def _attn_prompt_kernel(q_ref, k_ref, v_ref, lam_ref, g_ref, o_ref, kbf, vbf, *, tq, n_q, lam_init):
    qi = pl.program_id(2)

    @pl.when(qi == 0)
    def _():
        kbf[...] = k_ref[...].astype(BF16)
        vbf[...] = v_ref[...].astype(BF16)

    lam = _lambda(lam_ref, lam_init)
    for i in range(n_q):
        pl.when(qi == i)(functools.partial(
            _attn_prompt_block, i, q_ref, kbf, vbf, lam, g_ref, o_ref, tq=tq, lam_init=lam_init))


def _attn_prompt(q_new, k_stack, v_stack, lam_p, da_g, l, lam_init):
    b, t, _ = q_new.shape
    tq = _tile(t, 256)
    kv_spec = pl.BlockSpec((None, None, t, HEAD_DIM), lambda b, h, i: (l, b, 0, h))
    return pl.pallas_call(
        functools.partial(_attn_prompt_kernel, tq=tq, n_q=t // tq, lam_init=lam_init),
        grid=(b, N_DA, t // tq),
        in_specs=[
            pl.BlockSpec((None, tq, HEAD_DIM), lambda b, h, i: (b, i, h)),
            kv_spec, kv_spec,
            pl.BlockSpec((None, 4, DA_QK), lambda b, h, i: (l, 0, 0)),
            pl.BlockSpec((None, 1, HEAD_DIM), lambda b, h, i: (l, 0, h)),
        ],
        out_specs=pl.BlockSpec((None, tq, HEAD_DIM), lambda b, h, i: (b, i, h)),
        out_shape=jax.ShapeDtypeStruct((b, t, DA_W), BF16),
        scratch_shapes=[pltpu.VMEM((t, HEAD_DIM), BF16), pltpu.VMEM((t, HEAD_DIM), BF16)],
        compiler_params=_params(("arbitrary", "arbitrary", "arbitrary")),
        name="attn_prompt",
    )(q_new, k_stack, v_stack, lam_p, da_g)


DEC_ROWS = N_DA * 2 * SUBLANES
PAGE_ROWS = PAGE * N_DA


def _attn_decode_kernel(pt_ref, q_ref, kn_ref, vn_ref, lam_ref, g_ref, *rest,
                        pages, n_steps, t_new, lam_init):
    k_refs = rest[:pages]
    v_refs = rest[pages:2 * pages]
    o_ref, qm, bias, m_scr, l_scr, acc_scr, kpad, vpad = rest[2 * pages:]
    step = pl.program_id(1)

    @pl.when(step == 0)
    def _():
        q = q_ref[...] * (DA_QK ** -0.5)
        lane = lax.broadcasted_iota(jnp.int32, (SUBLANES, HEAD_DIM), 1)
        for h in range(N_DA):
            qh = q[:, h * HEAD_DIM:(h + 1) * HEAD_DIM]
            r0 = h * 2 * SUBLANES
            qm[r0:r0 + SUBLANES, :] = jnp.where(lane < DA_QK, qh, 0.0)
            qm[r0 + SUBLANES:r0 + 2 * SUBLANES, :] = jnp.where(lane >= DA_QK, qh, 0.0)
        row = lax.broadcasted_iota(jnp.int32, bias.shape, 0)
        col = lax.broadcasted_iota(jnp.int32, bias.shape, 1)
        bias[...] = jnp.where((col & (N_DA - 1)) == (row >> 4), 0.0, NEG_INF)
        m_scr[...] = jnp.full(m_scr.shape, -jnp.inf, F32)
        l_scr[...] = jnp.zeros(l_scr.shape, F32)
        acc_scr[...] = jnp.zeros(acc_scr.shape, F32)

    qb = qm[...].astype(BF16)
    head_bias = bias[...]
    scores = [lax.dot_general(qb, k_refs[p][...].astype(BF16), NT_DIMS, preferred_element_type=F32)
              + head_bias for p in range(pages)]
    m_prev = m_scr[...]
    m_new = m_prev
    for s in scores:
        m_new = jnp.maximum(m_new, jnp.max(s, axis=1, keepdims=True))
    alpha = jnp.exp(m_prev - m_new)
    l_new = alpha * l_scr[...]
    pv = None
    for p in range(pages):
        e = jnp.exp(scores[p] - m_new)
        l_new = l_new + jnp.sum(e, axis=1, keepdims=True)
        d = jnp.dot(e.astype(BF16), v_refs[p][...].astype(BF16), preferred_element_type=F32)
        pv = d if pv is None else pv + d
    l_scr[...] = l_new
    acc_scr[...] = alpha * acc_scr[...] + pv
    m_scr[...] = m_new

    @pl.when(step == n_steps - 1)
    def _():
        kpad[...] = jnp.zeros(kpad.shape, F32)
        vpad[...] = jnp.zeros(vpad.shape, F32)
        kpad[0:t_new * N_DA, :] = kn_ref[...]
        vpad[0:t_new * N_DA, :] = vn_ref[...]
        s = lax.dot_general(qb, kpad[...].astype(BF16), NT_DIMS, preferred_element_type=F32)
        row = lax.broadcasted_iota(jnp.int32, s.shape, 0)
        col = lax.broadcasted_iota(jnp.int32, s.shape, 1)
        causal = (col >> 3) <= (row & (SUBLANES - 1))
        s = jnp.where(causal, s + head_bias, NEG_INF)
        _online_softmax_step(s, vpad[...].astype(BF16), m_scr, l_scr, acc_scr)

        lam = _lambda(lam_ref, lam_init)
        o = acc_scr[...] / l_scr[...]
        for h in range(N_DA):
            r0 = h * 2 * SUBLANES
            cols = slice(h * HEAD_DIM, (h + 1) * HEAD_DIM)
            oa = o[r0:r0 + SUBLANES] - lam * o[r0 + SUBLANES:r0 + 2 * SUBLANES]
            o_ref[:, cols] = _rms(oa) * g_ref[:, cols] * (1.0 - lam_init)


def _attn_decode(q_new, k_stack, v_stack, cache_k, cache_v, page_table, lam_p, da_g, l, lam_init):
    b, t_new, _ = q_new.shape
    k_new, v_new = k_stack[l], v_stack[l]
    assert t_new == SUBLANES and cache_k.shape[2:] == (PAGE, N_DA, HEAD_DIM)
    n_pages = page_table.shape[1]
    pages = _tile(n_pages, 8, 1)
    n_steps = n_pages // pages
    n_pool = cache_k.shape[1]
    ck = cache_k.reshape(cache_k.shape[0], n_pool, PAGE_ROWS, HEAD_DIM)
    cv = cache_v.reshape(cache_v.shape[0], n_pool, PAGE_ROWS, HEAD_DIM)
    kn = k_new.reshape(b, t_new * N_DA, HEAD_DIM)
    vn = v_new.reshape(b, t_new * N_DA, HEAD_DIM)

    def page_spec(p):
        return pl.BlockSpec((None, None, PAGE_ROWS, HEAD_DIM),
                            lambda b, s, pt: (l, pt[b * n_pages + s * pages + p], 0, 0))

    new_spec = pl.BlockSpec((None, t_new * N_DA, HEAD_DIM), lambda b, s, pt: (b, 0, 0))
    grid_spec = pltpu.PrefetchScalarGridSpec(
        num_scalar_prefetch=1,
        grid=(b, n_steps),
        in_specs=[
            pl.BlockSpec((None, t_new, DA_W), lambda b, s, pt: (b, 0, 0)),
            new_spec, new_spec,
            pl.BlockSpec((None, 4, DA_QK), lambda b, s, pt: (l, 0, 0)),
            pl.BlockSpec((None, 1, DA_W), lambda b, s, pt: (l, 0, 0)),
        ] + [page_spec(p) for p in range(pages)] * 2,
        out_specs=pl.BlockSpec((None, t_new, DA_W), lambda b, s, pt: (b, 0, 0)),
        scratch_shapes=[
            pltpu.VMEM((DEC_ROWS, HEAD_DIM), F32),
            pltpu.VMEM((DEC_ROWS, PAGE_ROWS), F32),
            pltpu.VMEM((DEC_ROWS, 1), F32), pltpu.VMEM((DEC_ROWS, 1), F32),
            pltpu.VMEM((DEC_ROWS, HEAD_DIM), F32),
            pltpu.VMEM((PAGE_ROWS, HEAD_DIM), F32), pltpu.VMEM((PAGE_ROWS, HEAD_DIM), F32),
        ],
    )
    return pl.pallas_call(
        functools.partial(_attn_decode_kernel, pages=pages, n_steps=n_steps, t_new=t_new,
                          lam_init=lam_init),
        grid_spec=grid_spec,
        out_shape=jax.ShapeDtypeStruct((b, t_new, DA_W), F32),
        compiler_params=_params(("arbitrary", "arbitrary")),
        name="attn_decode",
    )(page_table.reshape(-1), q_new, kn, vn, lam_p, da_g, *([ck] * pages), *([cv] * pages))


def _rotary(x, cos2, sin2):
    return x * cos2 + pltpu.roll(x, DA_QK, 1) * sin2


def _ret_kernel(q_ref, k_ref, v_ref, gate_ref, cos_ref, sin_ref, lg_ref, s0_ref, gn_ref,
                o_ref, sout_ref, s_scr, *, n_chunks, l_true):
    L = CHUNK
    lg = lg_ref[...][:, 0:1]
    ti = lax.broadcasted_iota(jnp.int32, (L, L), 0)
    si = lax.broadcasted_iota(jnp.int32, (L, L), 1)
    rel = (ti - si).astype(F32)
    dmat = jnp.where(rel >= 0, jnp.exp(lg * jnp.maximum(rel, 0.0)), 0.0)
    idx = lax.broadcasted_iota(jnp.int32, (L, 1), 0).astype(F32)
    q_dec = jnp.exp(lg * (idx + 1.0))
    k_dec = jnp.exp(lg * (l_true - 1.0 - idx))
    s_dec = jnp.exp(lg * float(l_true))
    s_scr[...] = s0_ref[...]

    def body(c, carry):
        r0 = pl.multiple_of(c * L, L)
        rows = pl.ds(r0, L)
        cos2, sin2 = cos_ref[rows, :], sin_ref[rows, :]
        q = _rotary(q_ref[rows, :], cos2, sin2)
        k = _rotary(k_ref[rows, :], cos2, sin2) * (HEAD_DIM ** -0.5)
        q_bf, k_bf, v_bf = q.astype(BF16), k.astype(BF16), v_ref[rows, :].astype(BF16)
        state = s_scr[...]
        scores = lax.dot_general(q_bf, k_bf, NT_DIMS, preferred_element_type=F32) * dmat
        inner = jnp.dot(scores.astype(BF16), v_bf, preferred_element_type=F32)
        cross = jnp.dot(q_bf, state.astype(BF16), preferred_element_type=F32) * q_dec
        kd_t = (k * k_dec).T.astype(BF16)
        s_scr[...] = state * s_dec + jnp.dot(kd_t, v_bf, preferred_element_type=F32)
        g = gate_ref[rows, :]
        y = _rms(inner + cross) * gn_ref[...]
        o_ref[rows, :] = (y.astype(F32) * (g * jax.nn.sigmoid(g))).astype(o_ref.dtype)
        return carry

    lax.fori_loop(0, n_chunks, body, 0, unroll=min(2, n_chunks))
    sout_ref[...] = s_scr[...]


def _retention(proj, cos2, sin2, lg_tab, s0, gn, l, l_true):
    b, t, _ = proj.shape
    n_chunks = t // CHUNK

    def col(c0):
        return pl.BlockSpec((None, t, HEAD_DIM), lambda b, h: (b, 0, c0 // HEAD_DIM + h))

    st_spec = pl.BlockSpec((None, None, HEAD_DIM, HEAD_DIM), lambda b, h: (b, h, 0, 0))
    return pl.pallas_call(
        functools.partial(_ret_kernel, n_chunks=n_chunks, l_true=l_true),
        grid=(b, N_RET),
        in_specs=[
            col(COL_RQ), col(COL_RK), col(COL_RV), col(COL_RG),
            pl.BlockSpec((t, HEAD_DIM), lambda b, h: (0, 0)),
            pl.BlockSpec((t, HEAD_DIM), lambda b, h: (0, 0)),
            pl.BlockSpec((None, 1, HEAD_DIM), lambda b, h: (h, 0, 0)),
            st_spec,
            pl.BlockSpec((None, 1, HEAD_DIM), lambda b, h: (l, 0, h)),
        ],
        out_specs=[pl.BlockSpec((None, t, HEAD_DIM), lambda b, h: (b, 0, h)), st_spec],
        out_shape=[jax.ShapeDtypeStruct((b, t, RET_W), BF16),
                   jax.ShapeDtypeStruct((b, N_RET, HEAD_DIM, HEAD_DIM), F32)],
        scratch_shapes=[pltpu.VMEM((HEAD_DIM, HEAD_DIM), F32)],
        compiler_params=_params(("arbitrary", "arbitrary")),
        name="retention",
    )(proj, proj, proj, proj, cos2, sin2, lg_tab, s0, gn)


def _row_to_col(row, eye):
    return jnp.sum(jnp.where(eye, row, 0.0), axis=1, keepdims=True)


def _cumsum_lanes(x, lane):
    n = x.shape[1]
    shift = 1
    while shift < n:
        x = x + jnp.where(lane >= shift, pltpu.roll(x, shift, 1), 0.0)
        shift *= 2
    return x


def _mlstm_kernel(u_ref, v_ref, og_ref, prev_ref, ig_ref, fg_ref, ib_ref, fb_ref, cw_ref, cb_ref,
                  wq_ref, wk_ref, c0_ref, n0_ref, m0_ref, gn_ref, skip_ref,
                  o_ref, cout_ref, nout_ref, mout_ref,
                  xp_scr, uc_scr, c_scr, n_scr, m_scr, *, t, n_chunks, l_true):
    L = CHUNK
    halo = SUBLANES

    xp_scr[0:halo, :] = prev_ref[...]
    xp_scr[halo:halo + t, :] = u_ref[...]
    conv = cb_ref[...]
    for j in range(ML_CONV):
        start = halo - (ML_CONV - 1) + j
        conv = conv + xp_scr[start:start + t, :] * cw_ref[j:j + 1, :]
    uc_scr[...] = conv * jax.nn.sigmoid(conv)

    wq_bf = wq_ref[...].astype(BF16)
    wk_bf = wk_ref[...].astype(BF16)
    c_scr[...] = c0_ref[...]
    n_scr[...] = n0_ref[...]
    m_scr[...] = m0_ref[...]

    ti = lax.broadcasted_iota(jnp.int32, (L, L), 0)
    si = lax.broadcasted_iota(jnp.int32, (L, L), 1)
    eye = ti == si
    causal = si <= ti
    lane = lax.broadcasted_iota(jnp.int32, (1, L), 1)

    def body(c, carry):
        r0 = pl.multiple_of(c * L, L)
        rows = pl.ds(r0, L)
        uc = uc_scr[rows, :]
        uc_bf = uc.astype(BF16)
        q = jnp.dot(uc_bf, wq_bf, preferred_element_type=F32)
        k = jnp.dot(uc_bf, wk_bf, preferred_element_type=F32) * (HEAD_DIM ** -0.5)
        q_bf, k_bf, v_bf = q.astype(BF16), k.astype(BF16), v_ref[rows, :].astype(BF16)

        ig = ig_ref[:, rows] + ib_ref[...]
        fg = fg_ref[:, rows] + fb_ref[...]
        lf = -(jnp.maximum(-fg, 0.0) + jnp.log(1.0 + jnp.exp(-jnp.abs(fg))))
        b_row = _cumsum_lanes(lf, lane)
        b_col = _row_to_col(b_row, eye)
        m_old = m_scr[...]
        c_old = c_scr[...]
        n_old = n_scr[...]

        dlog = jnp.where(causal, b_col + (ig - b_row), -jnp.inf)
        inter = b_col + m_old
        m_row = jnp.maximum(inter, jnp.max(dlog, axis=1, keepdims=True))
        w = jnp.exp(dlog - m_row)
        s_in = jnp.exp(inter - m_row)
        a = w * lax.dot_general(q_bf, k_bf, NT_DIMS, preferred_element_type=F32)
        num = (jnp.dot(a.astype(BF16), v_bf, preferred_element_type=F32)
               + jnp.dot(q_bf, c_old.astype(BF16), preferred_element_type=F32) * s_in)
        qn = jnp.sum(q_bf.astype(F32) * n_old.astype(BF16).astype(F32), axis=1, keepdims=True)
        den = jnp.sum(a, axis=1, keepdims=True) + qn * s_in
        hh = num / jnp.maximum(jnp.abs(den), jnp.exp(-m_row))

        og = og_ref[rows, :]
        y = _rms(hh) * gn_ref[...] + skip_ref[...] * uc
        o_ref[rows, :] = (jax.nn.sigmoid(og) * y).astype(o_ref.dtype)

        b_last = jnp.sum(jnp.where(lane == l_true - 1, b_row, 0.0), axis=1, keepdims=True)
        dlast = jnp.where(lane < l_true, b_last - b_row + ig, -jnp.inf)
        m_new = jnp.maximum(b_last + m_old, jnp.max(dlast, axis=1, keepdims=True))
        ws_col = _row_to_col(jnp.exp(dlast - m_new), eye)
        s_old = jnp.exp(b_last + m_old - m_new)
        kw = k * ws_col
        c_scr[...] = c_old * s_old + jnp.dot(kw.T.astype(BF16), v_bf, preferred_element_type=F32)
        n_scr[...] = n_old * s_old + jnp.sum(kw, axis=0, keepdims=True)
        m_scr[...] = m_new
        return carry

    lax.fori_loop(0, n_chunks, body, 0, unroll=min(2, n_chunks))
    cout_ref[...] = c_scr[...]
    nout_ref[...] = n_scr[...]
    mout_ref[...] = m_scr[...]


def _mlstm(proj, prev, gates, gbias, cw, cb, wq, wk, c0, n0, m0, gn, skip, l, l_true):
    b, t, _ = proj.shape
    n_chunks = t // CHUNK

    def col(c0_):
        return pl.BlockSpec((None, t, HEAD_DIM), lambda b, h: (b, 0, c0_ // HEAD_DIM + h))

    def per_head(rows):
        return pl.BlockSpec((None, rows, HEAD_DIM), lambda b, h: (l, 0, h))

    c_spec = pl.BlockSpec((None, None, HEAD_DIM, HEAD_DIM), lambda b, h: (b, h, 0, 0))
    n_spec = pl.BlockSpec((None, None, 1, HEAD_DIM), lambda b, h: (b, h, 0, 0))
    m_spec = pl.BlockSpec((None, None, 1, 1), lambda b, h: (b, h, 0, 0))
    w_spec = pl.BlockSpec((None, None, HEAD_DIM, HEAD_DIM), lambda b, h: (l, h, 0, 0))
    return pl.pallas_call(
        functools.partial(_mlstm_kernel, t=t, n_chunks=n_chunks, l_true=l_true),
        grid=(b, N_ML),
        in_specs=[
            col(COL_MU), col(COL_MV), col(COL_MO),
            pl.BlockSpec((None, SUBLANES, HEAD_DIM), lambda b, h: (b, 0, h)),
            pl.BlockSpec((None, None, 1, t), lambda b, h: (b, h, 0, 0)),
            pl.BlockSpec((None, None, 1, t), lambda b, h: (b, N_ML + h, 0, 0)),
            pl.BlockSpec((None, None, 1, 1), lambda b, h: (l, h, 0, 0)),
            pl.BlockSpec((None, None, 1, 1), lambda b, h: (l, N_ML + h, 0, 0)),
            per_head(ML_CONV), per_head(1),
            w_spec, w_spec, c_spec, n_spec, m_spec,
            per_head(1), per_head(1),
        ],
        out_specs=[pl.BlockSpec((None, t, HEAD_DIM), lambda b, h: (b, 0, h)),
                   c_spec, n_spec, m_spec],
        out_shape=[jax.ShapeDtypeStruct((b, t, ML_W), BF16),
                   jax.ShapeDtypeStruct((b, N_ML, HEAD_DIM, HEAD_DIM), F32),
                   jax.ShapeDtypeStruct((b, N_ML, 1, HEAD_DIM), F32),
                   jax.ShapeDtypeStruct((b, N_ML, 1, 1), F32)],
        scratch_shapes=[
            pltpu.VMEM((SUBLANES + t, HEAD_DIM), F32), pltpu.VMEM((t, HEAD_DIM), F32),
            pltpu.VMEM((HEAD_DIM, HEAD_DIM), F32), pltpu.VMEM((1, HEAD_DIM), F32),
            pltpu.VMEM((1, 1), F32),
        ],
        compiler_params=_params(("arbitrary", "arbitrary")),
        name="mlstm",
    )(proj, proj, proj, prev, gates[:, :, None, :], gates[:, :, None, :], gbias, gbias,
      cw, cb, wq, wk, c0, n0, m0, gn, skip)


def _conv_gate_kernel(ug_ref, uv_ref, hg_ref, hv_ref, pg_ref, pv_ref, wg_ref, wv_ref,
                      bg_ref, bv_ref, o_ref, sg, sv, *, tt):
    first = pl.program_id(1) == 0
    halo = SUBLANES

    def conv(u_ref, h_ref, p_ref, w_ref, b_ref, scr):
        @pl.when(first)
        def _():
            scr[0:halo, :] = p_ref[...]

        @pl.when(jnp.logical_not(first))
        def _():
            scr[0:halo, :] = h_ref[...]

        scr[halo:halo + tt, :] = u_ref[...]
        y = b_ref[...]
        for j in range(FFN_CONV):
            start = halo - (FFN_CONV - 1) + j
            y = y + scr[start:start + tt, :] * w_ref[j:j + 1, :]
        return y

    gate = conv(ug_ref, hg_ref, pg_ref, wg_ref, bg_ref, sg)
    val = conv(uv_ref, hv_ref, pv_ref, wv_ref, bv_ref, sv)
    o_ref[...] = (gate * jax.nn.sigmoid(gate) * val).astype(o_ref.dtype)


def _conv_gate(up, prev, cw, cb, l):
    b, t, f2 = up.shape
    f = f2 // 2
    tt = _tile(t, 512, SUBLANES)
    tn = _tile(f, max(512, 512 * 512 // tt))
    nf = f // tn
    rb = tt // SUBLANES

    def main(off):
        return pl.BlockSpec((None, tt, tn), lambda b, i, n: (b, i, n + off))

    def halo(off):
        return pl.BlockSpec((None, SUBLANES, tn),
                            lambda b, i, n: (b, jnp.maximum(i * rb - 1, 0), n + off))

    def prev_spec(off):
        return pl.BlockSpec((None, SUBLANES, tn), lambda b, i, n: (b, 0, n + off))

    def w_spec(rows, off):
        return pl.BlockSpec((None, rows, tn), lambda b, i, n: (l, 0, n + off))

    return pl.pallas_call(
        functools.partial(_conv_gate_kernel, tt=tt),
        grid=(b, t // tt, nf),
        in_specs=[main(0), main(nf), halo(0), halo(nf), prev_spec(0), prev_spec(nf),
                  w_spec(FFN_CONV, 0), w_spec(FFN_CONV, nf), w_spec(1, 0), w_spec(1, nf)],
        out_specs=pl.BlockSpec((None, tt, tn), lambda b, i, n: (b, i, n)),
        out_shape=jax.ShapeDtypeStruct((b, t, f), BF16),
        scratch_shapes=[pltpu.VMEM((SUBLANES + tt, tn), F32), pltpu.VMEM((SUBLANES + tt, tn), F32)],
        compiler_params=_params(("arbitrary", "arbitrary", "arbitrary")),
        name="conv_gate",
    )(up, up, up, up, prev, prev, cw, cw, cb, cb)


def _ffn_up_kernel(x_ref, wg_ref, wv_ref, pg_ref, pv_ref, cwg_ref, cwv_ref, cbg_ref, cbv_ref,
                   o_ref, sg_ref, sv_ref, wbf_g, wbf_v, scr_g, scr_v, *, tm):
    t = pl.program_id(2)
    halo = SUBLANES

    @pl.when(jnp.logical_and(pl.program_id(1) == 0, t == 0))
    def _():
        wbf_g[...] = wg_ref[...].astype(BF16)
        wbf_v[...] = wv_ref[...].astype(BF16)

    x = x_ref[...]

    def conv(wbf, p_ref, cw_ref, cb_ref, scr, s_ref):
        @pl.when(t == 0)
        def _():
            scr[0:halo, :] = p_ref[...]

        scr[halo:halo + tm, :] = jnp.dot(x, wbf[...], preferred_element_type=F32)
        y = cb_ref[...]
        for j in range(FFN_CONV):
            start = halo - (FFN_CONV - 1) + j
            y = y + scr[start:start + tm, :] * cw_ref[j:j + 1, :]
        tail = scr[tm:tm + halo, :]
        s_ref[...] = tail
        scr[0:halo, :] = tail
        return y

    gate = conv(wbf_g, pg_ref, cwg_ref, cbg_ref, scr_g, sg_ref)
    val = conv(wbf_v, pv_ref, cwv_ref, cbv_ref, scr_v, sv_ref)
    o_ref[...] = (gate * jax.nn.sigmoid(gate) * val).astype(o_ref.dtype)


def _ffn_up_fused(x, w_up, prev, cw, cb, l):
    b, t, d = x.shape
    f = w_up.shape[2] // 2
    tm = _tile(t, 1024, SUBLANES)
    tn = _tile(f, 512)
    nf = f // tn

    def cols(rows, off, arr_l):
        return pl.BlockSpec((None, rows, tn), lambda n, b, i: (arr_l(b), 0, n + off))

    layer = lambda b: l
    batch = lambda b: b
    tail_spec = pl.BlockSpec((None, SUBLANES, tn), lambda n, b, i: (b, 0, n))
    return pl.pallas_call(
        functools.partial(_ffn_up_kernel, tm=tm),
        grid=(nf, b, t // tm),
        in_specs=[
            pl.BlockSpec((None, tm, d), lambda n, b, i: (b, i, 0)),
            cols(d, 0, layer), cols(d, nf, layer),
            cols(SUBLANES, 0, batch), cols(SUBLANES, nf, batch),
            cols(FFN_CONV, 0, layer), cols(FFN_CONV, nf, layer),
            cols(1, 0, layer), cols(1, nf, layer),
        ],
        out_specs=[pl.BlockSpec((None, tm, tn), lambda n, b, i: (b, i, n)), tail_spec, tail_spec],
        out_shape=[jax.ShapeDtypeStruct((b, t, f), BF16),
                   jax.ShapeDtypeStruct((b, SUBLANES, f), F32),
                   jax.ShapeDtypeStruct((b, SUBLANES, f), F32)],
        scratch_shapes=[pltpu.VMEM((d, tn), BF16), pltpu.VMEM((d, tn), BF16),
                        pltpu.VMEM((SUBLANES + tm, tn), F32), pltpu.VMEM((SUBLANES + tm, tn), F32)],
        compiler_params=_params(("arbitrary", "arbitrary", "arbitrary")),
        name="ffn_up_conv",
    )(x, w_up, w_up, prev, prev, cw, cw, cb, cb)


def _rope_tables(pos, rows):
    half = HEAD_DIM // 2
    inv = ROPE_BASE ** (-jnp.arange(half, dtype=F32) / half)
    ang = pos.astype(F32)[:, None] * inv[None, :]
    cos, sin = jnp.cos(ang), jnp.sin(ang)
    cos2 = jnp.concatenate([cos, cos], axis=-1)
    sin2 = jnp.concatenate([-sin, sin], axis=-1)
    pad = rows - pos.shape[0]
    if pad:
        cos2 = jnp.pad(cos2, ((0, pad), (0, 0)))
        sin2 = jnp.pad(sin2, ((0, pad), (0, 0)))
    return cos2, sin2


def _prev_rows(buf):
    return jnp.pad(buf, ((0, 0), (SUBLANES - buf.shape[1], 0), (0, 0)))


def _layer(l, lam_init, x, mods, seq_shape, st, w, attn_fn, rope, lg_tab, kv_stacks):
    b, t = seq_shape
    bm, tmx, d = x.shape
    sh1, sc1, g1, sh2, sc2, g2 = mods
    t_pad = -(-t // CHUNK) * CHUNK
    l_true = CHUNK if t % CHUNK == 0 else t

    h = _norm(x, w["g_mix"], l, sc1, sh1)

    def in_proj(col0, width, tn, name, stack=None):
        assert col0 % tn == 0 and width % tn == 0
        return _matmul([h], w["w_in_t"], l, width // tn, tn, 1024, name=name, w_transposed=True,
                       w_tile=lambda n: n + col0 // tn, stack=stack)

    depth = w["w_in_t"].shape[0]
    q_new = in_proj(W_COL_Q, DA_W, 1024, "q_proj").reshape(b, t, DA_W)
    k_stack = in_proj(W_COL_K, DA_W, 1024, "k_proj", stack=(depth, kv_stacks[0]))
    v_stack = in_proj(W_COL_V, DA_W, 1024, "v_proj", stack=(depth, kv_stacks[1]))
    ret_in = in_proj(W_COL_RET, 4 * RET_W, 1024, "ret_proj").reshape(b, t, 4 * RET_W)
    ml_in = in_proj(W_COL_ML, 3 * ML_W, 512, "ml_proj").reshape(b, t, 3 * ML_W)
    gates = _gates(h, w["w_in_t"], l)

    oa = attn_fn(q_new, k_stack, v_stack)
    if (bm, tmx) != (b, t):
        gates = gates.reshape(GATE_ROWS, b, t).transpose(1, 0, 2)
    ret_pad, ml_pad = ret_in, ml_in
    if t_pad != t:
        ret_pad = jnp.pad(ret_in, ((0, 0), (0, t_pad - t), (0, 0)))
        ml_pad = jnp.pad(ml_in, ((0, 0), (0, t_pad - t), (0, 0)))
        gates = jnp.pad(gates, ((0, 0), (0, 0), (0, t_pad - t)))

    ro, ret_s = _retention(ret_pad, rope[0], rope[1], lg_tab, st["ret_s"], w["ret_norm_g"], l, l_true)
    mo, ml_c, ml_n, ml_m = _mlstm(
        ml_pad, _prev_rows(st["ml_conv"]), gates, w["gbias"], w["ml_conv_w"], w["ml_conv_b"],
        w["ml_wq"], w["ml_wk"], st["ml_c"], st["ml_n"][:, :, None, :], st["ml_m"][:, :, None, None],
        w["ml_norm_g"], w["ml_skip"], l, l_true)
    if t_pad != t:
        ro, mo = ro[:, :t], mo[:, :t]
    mix = [a.astype(BF16).reshape(bm, tmx, a.shape[-1]) for a in (oa, ro, mo)]
    x = _matmul(mix, w["w_out"], l, d // _tile(d, 1024), _tile(d, 1024), 1024,
                res=x, gate=g1, name="out_proj")

    h2 = _norm(x, w["g_ffn"], l, sc2, sh2)
    f2 = w["ffn_w_up"].shape[2]
    prev_ffn = _prev_rows(st["ffn_conv"])
    if (bm, tmx) == (b, t) and t % SUBLANES == 0:
        act, tail_g, tail_v = _ffn_up_fused(h2, w["ffn_w_up"], prev_ffn, w["ffn_conv_w"],
                                            w["ffn_conv_b"], l)
        ffn_conv = jnp.concatenate([tail_g, tail_v], axis=-1)[:, SUBLANES - (FFN_CONV - 1):, :]
    else:
        up = _matmul([h2], w["ffn_w_up"], l, f2 // _tile(f2, 1024), _tile(f2, 1024), 1024,
                     name="ffn_up")
        up_seq = up.reshape(b, t, f2)
        act = _conv_gate(up_seq, prev_ffn, w["ffn_conv_w"], w["ffn_conv_b"], l)
        ffn_conv = up_seq[:, t - (FFN_CONV - 1):, :]
    x = _matmul([act.reshape(bm, tmx, f2 // 2)], w["ffn_w_down"], l, d // _tile(d, 512),
                _tile(d, 512), 512, res=x, gate=g2, name="ffn_down")

    new_st = dict(
        ret_s=ret_s, ml_c=ml_c, ml_n=ml_n[:, :, 0, :], ml_m=ml_m[:, :, 0, 0],
        ml_conv=ml_in[:, t - (ML_CONV - 1):, COL_MU:COL_MU + ML_W], ffn_conv=ffn_conv)
    return x, (k_stack, v_stack), new_st


def kernel(x_prompt, x_sample, c_prompt, c_sample, cache_k, cache_v, page_table, state_ret, state_ml_c, state_ml_n, state_ml_m, state_ml_conv, state_ffn_conv, w_ada, b_ada, g_mix, w_in, lam_q1, lam_k1, lam_q2, lam_k2, da_norm_g, ret_norm_g, ml_conv_w, ml_conv_b, ml_wq, ml_wk, ml_b_i, ml_b_f, ml_norm_g, ml_skip, w_out, g_ffn, ffn_w_up, ffn_conv_w, ffn_conv_b, ffn_w_down, g_final):
    bp, tp, d = x_prompt.shape
    bs, ts, _ = x_sample.shape
    depth = w_in.shape[0]
    f2 = ffn_w_up.shape[2]
    past = page_table.shape[1] * PAGE

    def per_layer_rows(a):
        return a.reshape(depth, 1, a.shape[-1])

    zeros8 = jnp.zeros((depth, GATE_ROWS - N_GATES), F32)
    w = dict(
        g_mix=g_mix, g_ffn=g_ffn, w_out=w_out, ffn_w_up=ffn_w_up, ffn_w_down=ffn_w_down,
        w_in_t=jnp.swapaxes(w_in, 1, 2),
        gbias=jnp.concatenate([ml_b_i, ml_b_f, zeros8], axis=1)[:, :, None, None],
        lam_p=jnp.stack([lam_q1, lam_k1, lam_q2, lam_k2], axis=1),
        da_norm_g=per_layer_rows(da_norm_g), ret_norm_g=per_layer_rows(ret_norm_g),
        ml_conv_w=ml_conv_w, ml_conv_b=per_layer_rows(ml_conv_b), ml_wq=ml_wq, ml_wk=ml_wk,
        ml_norm_g=per_layer_rows(ml_norm_g), ml_skip=per_layer_rows(ml_skip),
        ffn_conv_w=ffn_conv_w, ffn_conv_b=per_layer_rows(ffn_conv_b))

    n_c = bp + bs
    c_rows = -(-n_c // SUBLANES) * SUBLANES
    c_all = jnp.pad(jnp.concatenate([c_prompt, c_sample], axis=0), ((0, c_rows - n_c), (0, 0)))
    mod = _ada(c_all, w_ada, b_ada)

    lg_np = np.log(1.0 - 2.0 ** (-5.0 - np.arange(N_RET, dtype=np.float64)))
    lg_tab = jnp.asarray(np.broadcast_to(lg_np[:, None, None], (N_RET, 1, HEAD_DIM)), F32)
    tp_pad = -(-tp // CHUNK) * CHUNK
    ts_pad = -(-ts // CHUNK) * CHUNK
    rope_p = _rope_tables(jnp.arange(tp), tp_pad)
    rope_s = _rope_tables(past + jnp.arange(ts), ts_pad)

    zero_st = dict(
        ret_s=jnp.zeros((bp, N_RET, HEAD_DIM, HEAD_DIM), F32),
        ml_c=jnp.zeros((bp, N_ML, HEAD_DIM, HEAD_DIM), F32),
        ml_n=jnp.zeros((bp, N_ML, HEAD_DIM), F32),
        ml_m=jnp.zeros((bp, N_ML), F32),
        ml_conv=jnp.zeros((bp, ML_CONV - 1, ML_W), F32),
        ffn_conv=jnp.zeros((bp, FFN_CONV - 1, f2), F32))
    names = ("ret_s", "ml_c", "ml_n", "ml_m", "ml_conv", "ffn_conv")
    out_p = {n: [] for n in names}
    out_s = {n: [] for n in names}
    kv_p = kv_s = (None, None)
    yp = x_prompt
    ys = x_sample.reshape(1, bs * ts, d)
    for l in range(depth):
        lam_init = 0.8 - 0.6 * math.exp(-0.3 * l)
        parts = jnp.split(mod[l], 6, axis=-1)
        mods_p = [m[:bp, None, :] for m in parts]
        mods_s = [jnp.repeat(m[bp:n_c], ts, axis=0)[None] for m in parts]

        attn_p = functools.partial(_attn_prompt, lam_p=w["lam_p"], da_g=w["da_norm_g"], l=l,
                                   lam_init=lam_init)
        yp, kv_p, st_p = _layer(l, lam_init, yp, mods_p, (bp, tp), zero_st, w, attn_p,
                                rope_p, lg_tab, kv_p)

        attn_s = functools.partial(_attn_decode, cache_k=cache_k, cache_v=cache_v,
                                   page_table=page_table, lam_p=w["lam_p"], da_g=w["da_norm_g"],
                                   l=l, lam_init=lam_init)
        st_in = dict(ret_s=state_ret[l], ml_c=state_ml_c[l], ml_n=state_ml_n[l], ml_m=state_ml_m[l],
                     ml_conv=state_ml_conv[l], ffn_conv=state_ffn_conv[l])
        ys, kv_s, st_s = _layer(l, lam_init, ys, mods_s, (bs, ts), st_in, w, attn_s,
                                rope_s, lg_tab, kv_s)
        for n in names:
            out_p[n].append(st_p[n])
            out_s[n].append(st_s[n])

    y_prompt = _norm(yp, g_final, None, out_dtype=F32)
    y_sample = _norm(ys, g_final, None, out_dtype=F32).reshape(bs, ts, d)
    kv_shape_p = (depth, bp, tp, N_DA, HEAD_DIM)
    kv_shape_s = (depth, bs, ts, N_DA, HEAD_DIM)
    return (y_prompt, y_sample,
            kv_p[0].reshape(kv_shape_p), kv_p[1].reshape(kv_shape_p),
            kv_s[0].reshape(kv_shape_s), kv_s[1].reshape(kv_shape_s),
            jnp.stack(out_p["ret_s"]), jnp.stack(out_s["ret_s"]),
            jnp.stack(out_p["ml_c"]), jnp.stack(out_s["ml_c"]),
            jnp.stack(out_p["ml_n"]), jnp.stack(out_s["ml_n"]),
            jnp.stack(out_p["ml_m"]), jnp.stack(out_s["ml_m"]),
            jnp.stack(out_p["ml_conv"]), jnp.stack(out_s["ml_conv"]),
            jnp.stack(out_p["ffn_conv"]), jnp.stack(out_s["ffn_conv"]))
```

```python
import functools
import math

import numpy as np
import jax
import jax.numpy as jnp
from jax import lax
from jax.experimental import pallas as pl
from jax.experimental.pallas import tpu as pltpu

F32 = jnp.float32
BF16 = jnp.bfloat16

HEAD_DIM = 128
DA_QK = HEAD_DIM // 2
N_DA = 8
N_RET = 4
N_ML = 4
DA_W = N_DA * HEAD_DIM
RET_W = N_RET * HEAD_DIM
ML_W = N_ML * HEAD_DIM
ML_CONV = 4
FFN_CONV = 3
CHUNK = 128
PAGE = 128
RMS_EPS = 1e-6
ROPE_BASE = 10000.0
NEG_INF = -1e30

W_COL_Q, W_COL_K, W_COL_V = 0, DA_W, 2 * DA_W
W_COL_RET = 3 * DA_W
W_COL_ML = W_COL_RET + 4 * RET_W
W_COL_GATES = W_COL_ML + 3 * ML_W
COL_RQ, COL_RK, COL_RV, COL_RG = 0, RET_W, 2 * RET_W, 3 * RET_W
COL_MU, COL_MV, COL_MO = 0, ML_W, 2 * ML_W
N_GATES = 2 * N_ML
GATE_ROWS = 16

SUBLANES = 8
LANES = 128
VMEM_LIMIT = 56 * 1024 * 1024
FFN_SUB = 4
CHUNK_UNROLL = 16

NT_DIMS = (((1,), (1,)), ((), ()))
TN_DIMS = (((0,), (0,)), ((), ()))


def _tile(n, pref, mult=LANES):
    if n <= pref:
        return n
    t = (pref // mult) * mult
    while t >= mult:
        if n % t == 0:
            return t
        t -= mult
    return n


def _params(sem):
    return pltpu.CompilerParams(dimension_semantics=sem, vmem_limit_bytes=VMEM_LIMIT)


def _rms(x):
    return x * lax.rsqrt(jnp.mean(x * x, axis=-1, keepdims=True) + RMS_EPS)


def _ada_kernel(c_ref, w_ref, b_ref, o_ref):
    c = c_ref[...]
    a = (c * jax.nn.sigmoid(c)).astype(BF16)
    o_ref[...] = jnp.dot(a, w_ref[...].astype(BF16), preferred_element_type=F32) + b_ref[...]


def _ada(c_all, w_ada, b_ada):
    depth, d, n = w_ada.shape
    rows = c_all.shape[0]
    tn = _tile(n, 1024)
    return pl.pallas_call(
        _ada_kernel,
        grid=(depth, n // tn),
        in_specs=[
            pl.BlockSpec((rows, d), lambda l, j: (0, 0)),
            pl.BlockSpec((None, d, tn), lambda l, j: (l, 0, j)),
            pl.BlockSpec((None, 1, tn), lambda l, j: (l, 0, j)),
        ],
        out_specs=pl.BlockSpec((None, rows, tn), lambda l, j: (l, 0, j)),
        out_shape=jax.ShapeDtypeStruct((depth, rows, n), F32),
        compiler_params=_params(("arbitrary", "arbitrary")),
        name="ada",
    )(c_all, w_ada, b_ada.reshape(depth, 1, n))


def _norm_kernel(*refs, modulated):
    if modulated:
        x_ref, g_ref, sc_ref, sh_ref, o_ref = refs
    else:
        x_ref, g_ref, o_ref = refs
    y = _rms(x_ref[...]) * g_ref[...]
    if modulated:
        y = y * (1.0 + sc_ref[...]) + sh_ref[...]
    o_ref[...] = y.astype(o_ref.dtype)


def _mod_spec(mod, tm, tn, with_n):
    per_row = mod.shape[1] != 1
    rows = tm if per_row else 1
    if with_n:
        return pl.BlockSpec((None, rows, tn), lambda n, b, t: (b, t if per_row else 0, n))
    return pl.BlockSpec((None, rows, tn), lambda b, t: (b, t if per_row else 0, 0))


def _norm(x, g, l, sc=None, sh=None, out_dtype=BF16):
    bm, tmx, d = x.shape
    tm = _tile(tmx, 512, SUBLANES)
    modulated = sc is not None
    g_spec = (pl.BlockSpec((None, 1, d), lambda b, t: (l, 0, 0)) if l is not None
              else pl.BlockSpec((1, d), lambda b, t: (0, 0)))
    in_specs = [pl.BlockSpec((None, tm, d), lambda b, t: (b, t, 0)), g_spec]
    args = [x, g.reshape(g.shape[0], 1, d) if l is not None else g.reshape(1, d)]
    if modulated:
        in_specs += [_mod_spec(sc, tm, d, False), _mod_spec(sh, tm, d, False)]
        args += [sc, sh]
    return pl.pallas_call(
        functools.partial(_norm_kernel, modulated=modulated),
        grid=(bm, tmx // tm),
        in_specs=in_specs,
        out_specs=pl.BlockSpec((None, tm, d), lambda b, t: (b, t, 0)),
        out_shape=jax.ShapeDtypeStruct((bm, tmx, d), out_dtype),
        compiler_params=_params(("arbitrary", "arbitrary")),
        name="norm",
    )(*args)


def _mm_kernel(*refs, row_offs, row_sizes, has_res, w_transposed):
    n_in = len(row_offs)
    x_refs = refs[:n_in]
    w_ref = refs[n_in]
    o_ref, wbf_ref = refs[-2:]
    if has_res:
        res_ref, gate_ref = refs[n_in + 1:n_in + 3]

    @pl.when(jnp.logical_and(pl.program_id(1) == 0, pl.program_id(2) == 0))
    def _():
        wbf_ref[...] = w_ref[...].astype(BF16)

    acc = None
    for x_ref, off, size in zip(x_refs, row_offs, row_sizes):
        if w_transposed:
            part = lax.dot_general(x_ref[...], wbf_ref[:, off:off + size], NT_DIMS,
                                   preferred_element_type=F32)
        else:
            part = jnp.dot(x_ref[...], wbf_ref[off:off + size, :], preferred_element_type=F32)
        acc = part if acc is None else acc + part
    if has_res:
        acc = res_ref[...] + gate_ref[...] * acc
    o_ref[...] = acc.astype(o_ref.dtype)


def _matmul(xs, w, l, n_tiles, tn, tm_pref, res=None, gate=None, w_tile=lambda n: n,
            w_transposed=False, stack=None, name="mm"):
    bm, tmx = xs[0].shape[:2]
    k = w.shape[2] if w_transposed else w.shape[1]
    sizes = [x.shape[2] for x in xs]
    offs = [sum(sizes[:i]) for i in range(len(xs))]
    assert sum(sizes) == k
    tm = _tile(tmx, tm_pref, SUBLANES)
    in_specs = [pl.BlockSpec((None, tm, s), lambda n, b, t: (b, t, 0)) for s in sizes]
    if w_transposed:
        in_specs.append(pl.BlockSpec((None, tn, k), lambda n, b, t: (l, w_tile(n), 0)))
    else:
        in_specs.append(pl.BlockSpec((None, k, tn), lambda n, b, t: (l, 0, w_tile(n))))
    args = list(xs) + [w]
    if res is not None:
        in_specs.append(pl.BlockSpec((None, tm, tn), lambda n, b, t: (b, t, n)))
        in_specs.append(_mod_spec(gate, tm, tn, True))
        args += [res, gate]
    out_specs = pl.BlockSpec((None, tm, tn), lambda n, b, t: (b, t, n))
    out_shape = jax.ShapeDtypeStruct((bm, tmx, n_tiles * tn), F32)
    aliases = {}
    if stack is not None:
        depth, buf = stack
        out_specs = pl.BlockSpec((None, None, tm, tn), lambda n, b, t: (l, b, t, n))
        out_shape = jax.ShapeDtypeStruct((depth, bm, tmx, n_tiles * tn), F32)
        if buf is not None:
            in_specs.append(pl.BlockSpec(memory_space=pl.ANY))
            args.append(buf)
            aliases = {len(args) - 1: 0}
    return pl.pallas_call(
        functools.partial(_mm_kernel, row_offs=tuple(offs), row_sizes=tuple(sizes),
                          has_res=res is not None, w_transposed=w_transposed),
        grid=(n_tiles, bm, tmx // tm),
        in_specs=in_specs,
        out_specs=out_specs,
        out_shape=out_shape,
        input_output_aliases=aliases,
        scratch_shapes=[pltpu.VMEM((tn, k) if w_transposed else (k, tn), BF16)],
        compiler_params=_params(("arbitrary", "arbitrary", "arbitrary")),
        name=name,
    )(*args)


def _gates_kernel(h_ref, wg_ref, o_ref):
    row = lax.broadcasted_iota(jnp.int32, wg_ref.shape, 0)
    wg = jnp.where(row < N_GATES, wg_ref[...], 0.0).astype(BF16)
    o_ref[...] = lax.dot_general(wg, h_ref[...], NT_DIMS, preferred_element_type=F32)


def _gates(h, w_in_t, l):
    bm, tmx, d = h.shape
    tm = _tile(tmx, 1024)
    assert W_COL_GATES % GATE_ROWS == 0 and w_in_t.shape[1] == W_COL_GATES + N_GATES
    return pl.pallas_call(
        _gates_kernel,
        grid=(bm, tmx // tm),
        in_specs=[pl.BlockSpec((None, tm, d), lambda b, t: (b, t, 0)),
                  pl.BlockSpec((None, GATE_ROWS, d), lambda b, t: (l, W_COL_GATES // GATE_ROWS, 0))],
        out_specs=pl.BlockSpec((None, GATE_ROWS, tm), lambda b, t: (b, 0, t)),
        out_shape=jax.ShapeDtypeStruct((bm, GATE_ROWS, tmx), F32),
        compiler_params=_params(("arbitrary", "arbitrary")),
        name="gates",
    )(h, w_in_t)


def _lambda(lam_ref, lam_init):
    lp = lam_ref[...]
    a = jnp.exp(jnp.sum(lp[0:1] * lp[1:2], axis=1, keepdims=True))
    b = jnp.exp(jnp.sum(lp[2:3] * lp[3:4], axis=1, keepdims=True))
    return a - b + lam_init


def _online_softmax_step(s, v_bf, m_scr, l_scr, acc_scr):
    m_prev = m_scr[...]
    m_new = jnp.maximum(m_prev, jnp.max(s, axis=1, keepdims=True))
    alpha = jnp.exp(m_prev - m_new)
    p = jnp.exp(s - m_new)
    l_scr[...] = alpha * l_scr[...] + jnp.sum(p, axis=1, keepdims=True)
    acc_scr[...] = alpha * acc_scr[...] + jnp.dot(p.astype(BF16), v_bf, preferred_element_type=F32)
    m_scr[...] = m_new


def _attn_prompt_block(qi, q_ref, kbf, vbf, lam, g_ref, o_ref, *, tq, lam_init):
    n0 = qi * tq
    q = q_ref[n0:n0 + tq, :] * (DA_QK ** -0.5)
    lane = lax.broadcasted_iota(jnp.int32, q.shape, 1)
    row = lax.broadcasted_iota(jnp.int32, (tq, tq), 0)
    col = lax.broadcasted_iota(jnp.int32, (tq, tq), 1)
    outs = []
    for first_map in (True, False):
        qm = jnp.where((lane < DA_QK) if first_map else (lane >= DA_QK), q, 0.0).astype(BF16)
        s_diag = lax.dot_general(qm, kbf[n0:n0 + tq, :], NT_DIMS, preferred_element_type=F32)
        s_diag = jnp.where(col <= row, s_diag, NEG_INF)
        m = jnp.max(s_diag, axis=1, keepdims=True)
        if qi > 0:
            s_past = lax.dot_general(qm, kbf[0:n0, :], NT_DIMS, preferred_element_type=F32)
            m = jnp.maximum(m, jnp.max(s_past, axis=1, keepdims=True))
        p_diag = jnp.exp(s_diag - m)
        denom = jnp.sum(p_diag, axis=1, keepdims=True)
        o = jnp.dot(p_diag.astype(BF16), vbf[n0:n0 + tq, :], preferred_element_type=F32)
        if qi > 0:
            p_past = jnp.exp(s_past - m)
            denom = denom + jnp.sum(p_past, axis=1, keepdims=True)
            o = o + jnp.dot(p_past.astype(BF16), vbf[0:n0, :], preferred_element_type=F32)
        outs.append(o / denom)
    oa = outs[0] - lam * outs[1]
    o_ref[n0:n0 + tq, :] = (_rms(oa) * g_ref[...] * (1.0 - lam_init)).astype(o_ref.dtype)


def _attn_prompt_kernel(q_ref, k_ref, v_ref, lam_ref, g_ref, o_ref, kbf, vbf, *, tq, n_q, lam_init):
    kbf[...] = k_ref[...].astype(BF16)
    vbf[...] = v_ref[...].astype(BF16)
    lam = _lambda(lam_ref, lam_init)
    for qi in range(n_q):
        _attn_prompt_block(qi, q_ref, kbf, vbf, lam, g_ref, o_ref, tq=tq, lam_init=lam_init)


def _attn_prompt(q_new, k_stack, v_stack, lam_p, da_g, l, lam_init):
    b, t, _ = q_new.shape
    tq = _tile(t, 256)
    kv_spec = pl.BlockSpec((None, None, t, HEAD_DIM), lambda b, h: (l, b, 0, h))
    return pl.pallas_call(
        functools.partial(_attn_prompt_kernel, tq=tq, n_q=t // tq, lam_init=lam_init),
        grid=(b, N_DA),
        in_specs=[
            pl.BlockSpec((None, t, HEAD_DIM), lambda b, h: (b, 0, h)),
            kv_spec, kv_spec,
            pl.BlockSpec((None, 4, DA_QK), lambda b, h: (l, 0, 0)),
            pl.BlockSpec((None, 1, HEAD_DIM), lambda b, h: (l, 0, h)),
        ],
        out_specs=pl.BlockSpec((None, t, HEAD_DIM), lambda b, h: (b, 0, h)),
        out_shape=jax.ShapeDtypeStruct((b, t, DA_W), BF16),
        scratch_shapes=[pltpu.VMEM((t, HEAD_DIM), BF16), pltpu.VMEM((t, HEAD_DIM), BF16)],
        compiler_params=_params(("arbitrary", "arbitrary")),
        name="attn_prompt",
    )(q_new, k_stack, v_stack, lam_p, da_g)


DEC_ROWS = N_DA * 2 * SUBLANES
PAGE_ROWS = PAGE * N_DA


def _attn_decode_kernel(pt_ref, q_ref, kn_ref, vn_ref, lam_ref, g_ref, *rest,
                        pages, n_steps, t_new, lam_init):
    k_refs = rest[:pages]
    v_refs = rest[pages:2 * pages]
    o_ref, qm, bias, m_scr, l_scr, acc_scr, kpad, vpad = rest[2 * pages:]
    step = pl.program_id(1)

    @pl.when(step == 0)
    def _():
        q = q_ref[...] * (DA_QK ** -0.5)
        lane = lax.broadcasted_iota(jnp.int32, (SUBLANES, HEAD_DIM), 1)
        for h in range(N_DA):
            qh = q[:, h * HEAD_DIM:(h + 1) * HEAD_DIM]
            r0 = h * 2 * SUBLANES
            qm[r0:r0 + SUBLANES, :] = jnp.where(lane < DA_QK, qh, 0.0)
            qm[r0 + SUBLANES:r0 + 2 * SUBLANES, :] = jnp.where(lane >= DA_QK, qh, 0.0)
        row = lax.broadcasted_iota(jnp.int32, bias.shape, 0)
        col = lax.broadcasted_iota(jnp.int32, bias.shape, 1)
        bias[...] = jnp.where((col & (N_DA - 1)) == (row >> 4), 0.0, NEG_INF)
        m_scr[...] = jnp.full(m_scr.shape, -jnp.inf, F32)
        l_scr[...] = jnp.zeros(l_scr.shape, F32)
        acc_scr[...] = jnp.zeros(acc_scr.shape, F32)

    qb = qm[...].astype(BF16)
    head_bias = bias[...]
    group = 2 if pages % 2 == 0 else 1

    def scores_of(g):
        return [lax.dot_general(qb, k_refs[p][...].astype(BF16), NT_DIMS,
                                preferred_element_type=F32) + head_bias
                for p in range(g * group, (g + 1) * group)]

    m_run, l_run, acc = m_scr[...], l_scr[...], acc_scr[...]
    upcoming = scores_of(0)
    for g in range(pages // group):
        scores = upcoming
        if (g + 1) * group < pages:
            upcoming = scores_of(g + 1)
        m_new = m_run
        for s in scores:
            m_new = jnp.maximum(m_new, jnp.max(s, axis=1, keepdims=True))
        alpha = jnp.exp(m_run - m_new)
        l_run = alpha * l_run
        pv = None
        for i, s in enumerate(scores):
            e = jnp.exp(s - m_new)
            l_run = l_run + jnp.sum(e, axis=1, keepdims=True)
            d = jnp.dot(e.astype(BF16), v_refs[g * group + i][...].astype(BF16),
                        preferred_element_type=F32)
            pv = d if pv is None else pv + d
        acc = alpha * acc + pv
        m_run = m_new
    m_scr[...] = m_run
    l_scr[...] = l_run
    acc_scr[...] = acc

    @pl.when(step == n_steps - 1)
    def _():
        kpad[...] = jnp.zeros(kpad.shape, F32)
        vpad[...] = jnp.zeros(vpad.shape, F32)
        kpad[0:t_new * N_DA, :] = kn_ref[...]
        vpad[0:t_new * N_DA, :] = vn_ref[...]
        s = lax.dot_general(qb, kpad[...].astype(BF16), NT_DIMS, preferred_element_type=F32)
        row = lax.broadcasted_iota(jnp.int32, s.shape, 0)
        col = lax.broadcasted_iota(jnp.int32, s.shape, 1)
        causal = (col >> 3) <= (row & (SUBLANES - 1))
        s = jnp.where(causal, s + head_bias, NEG_INF)
        _online_softmax_step(s, vpad[...].astype(BF16), m_scr, l_scr, acc_scr)

        lam = _lambda(lam_ref, lam_init)
        o = acc_scr[...] / l_scr[...]
        for h in range(N_DA):
            r0 = h * 2 * SUBLANES
            cols = slice(h * HEAD_DIM, (h + 1) * HEAD_DIM)
            oa = o[r0:r0 + SUBLANES] - lam * o[r0 + SUBLANES:r0 + 2 * SUBLANES]
            o_ref[:, cols] = _rms(oa) * g_ref[:, cols] * (1.0 - lam_init)


def _attn_decode(q_new, k_stack, v_stack, cache_k, cache_v, page_table, lam_p, da_g, l, lam_init):
    b, t_new, _ = q_new.shape
    k_new, v_new = k_stack[l], v_stack[l]
    assert t_new == SUBLANES and cache_k.shape[2:] == (PAGE, N_DA, HEAD_DIM)
    n_pages = page_table.shape[1]
    pages = _tile(n_pages, 8, 1)
    n_steps = n_pages // pages
    n_pool = cache_k.shape[1]
    ck = cache_k.reshape(cache_k.shape[0], n_pool, PAGE_ROWS, HEAD_DIM)
    cv = cache_v.reshape(cache_v.shape[0], n_pool, PAGE_ROWS, HEAD_DIM)
    kn = k_new.reshape(b, t_new * N_DA, HEAD_DIM)
    vn = v_new.reshape(b, t_new * N_DA, HEAD_DIM)

    def page_spec(p):
        return pl.BlockSpec((None, None, PAGE_ROWS, HEAD_DIM),
                            lambda b, s, pt: (l, pt[b * n_pages + s * pages + p], 0, 0))

    new_spec = pl.BlockSpec((None, t_new * N_DA, HEAD_DIM), lambda b, s, pt: (b, 0, 0))
    grid_spec = pltpu.PrefetchScalarGridSpec(
        num_scalar_prefetch=1,
        grid=(b, n_steps),
        in_specs=[
            pl.BlockSpec((None, t_new, DA_W), lambda b, s, pt: (b, 0, 0)),
            new_spec, new_spec,
            pl.BlockSpec((None, 4, DA_QK), lambda b, s, pt: (l, 0, 0)),
            pl.BlockSpec((None, 1, DA_W), lambda b, s, pt: (l, 0, 0)),
        ] + [page_spec(p) for p in range(pages)] * 2,
        out_specs=pl.BlockSpec((None, t_new, DA_W), lambda b, s, pt: (b, 0, 0)),
        scratch_shapes=[
            pltpu.VMEM((DEC_ROWS, HEAD_DIM), F32),
            pltpu.VMEM((DEC_ROWS, PAGE_ROWS), F32),
            pltpu.VMEM((DEC_ROWS, 1), F32), pltpu.VMEM((DEC_ROWS, 1), F32),
            pltpu.VMEM((DEC_ROWS, HEAD_DIM), F32),
            pltpu.VMEM((PAGE_ROWS, HEAD_DIM), F32), pltpu.VMEM((PAGE_ROWS, HEAD_DIM), F32),
        ],
    )
    return pl.pallas_call(
        functools.partial(_attn_decode_kernel, pages=pages, n_steps=n_steps, t_new=t_new,
                          lam_init=lam_init),
        grid_spec=grid_spec,
        out_shape=jax.ShapeDtypeStruct((b, t_new, DA_W), F32),
        compiler_params=_params(("arbitrary", "arbitrary")),
        name="attn_decode",
    )(page_table.reshape(-1), q_new, kn, vn, lam_p, da_g, *([ck] * pages), *([cv] * pages))


def _rotary(x, cos2, sin2):
    return x * cos2 + pltpu.roll(x, DA_QK, 1) * sin2


def _ret_parts(q_ref, k_ref, v_ref, gate_ref, cos_ref, sin_ref, lg_ref, s0_ref, gn_ref,
               o_ref, sout_ref, s_scr, *, l_true):
    L = CHUNK
    lg = lg_ref[...][:, 0:1]
    ti = lax.broadcasted_iota(jnp.int32, (L, L), 0)
    si = lax.broadcasted_iota(jnp.int32, (L, L), 1)
    rel = (ti - si).astype(F32)
    dmat = jnp.where(rel >= 0, jnp.exp(lg * jnp.maximum(rel, 0.0)), 0.0)
    idx = lax.broadcasted_iota(jnp.int32, (L, 1), 0).astype(F32)
    q_dec = jnp.exp(lg * (idx + 1.0))
    k_dec = jnp.exp(lg * (l_true - 1.0 - idx))
    s_dec = jnp.exp(lg * float(l_true))
    s_scr[...] = s0_ref[...]

    def body(c):
        rows = pl.ds(c * L if isinstance(c, int) else pl.multiple_of(c * L, L), L)
        cos2, sin2 = cos_ref[rows, :], sin_ref[rows, :]
        q = _rotary(q_ref[rows, :], cos2, sin2)
        k = _rotary(k_ref[rows, :], cos2, sin2) * (HEAD_DIM ** -0.5)
        q_bf, k_bf, v_bf = q.astype(BF16), k.astype(BF16), v_ref[rows, :].astype(BF16)
        state = s_scr[...]
        scores = lax.dot_general(q_bf, k_bf, NT_DIMS, preferred_element_type=F32) * dmat
        inner = jnp.dot(scores.astype(BF16), v_bf, preferred_element_type=F32)
        cross = jnp.dot(q_bf, state.astype(BF16), preferred_element_type=F32) * q_dec
        kd_t = (k * k_dec).T.astype(BF16)
        s_scr[...] = state * s_dec + jnp.dot(kd_t, v_bf, preferred_element_type=F32)
        g = gate_ref[rows, :]
        y = _rms(inner + cross) * gn_ref[...]
        o_ref[rows, :] = (y.astype(F32) * (g * jax.nn.sigmoid(g))).astype(o_ref.dtype)

    def finalize():
        sout_ref[...] = s_scr[...]

    return body, finalize


N_RET_IN, N_RET_OUT, N_RET_SCR = 9, 2, 1


def _retention_call(proj, cos2, sin2, lg_tab, s0, gn, l):
    b, t, _ = proj.shape

    def col(c0):
        return pl.BlockSpec((None, t, HEAD_DIM), lambda b, h: (b, 0, c0 // HEAD_DIM + h))

    st_spec = pl.BlockSpec((None, None, HEAD_DIM, HEAD_DIM), lambda b, h: (b, h, 0, 0))
    in_specs = [
        col(COL_RQ), col(COL_RK), col(COL_RV), col(COL_RG),
        pl.BlockSpec((t, HEAD_DIM), lambda b, h: (0, 0)),
        pl.BlockSpec((t, HEAD_DIM), lambda b, h: (0, 0)),
        pl.BlockSpec((None, 1, HEAD_DIM), lambda b, h: (h, 0, 0)),
        st_spec,
        pl.BlockSpec((None, 1, HEAD_DIM), lambda b, h: (l, 0, h)),
    ]
    out_specs = [pl.BlockSpec((None, t, HEAD_DIM), lambda b, h: (b, 0, h)), st_spec]
    out_shape = [jax.ShapeDtypeStruct((b, t, RET_W), BF16),
                 jax.ShapeDtypeStruct((b, N_RET, HEAD_DIM, HEAD_DIM), F32)]
    scratch = [pltpu.VMEM((HEAD_DIM, HEAD_DIM), F32)]
    args = (proj, proj, proj, proj, cos2, sin2, lg_tab, s0, gn)
    return in_specs, args, out_specs, out_shape, scratch


def _row_to_col(row, eye):
    return jnp.sum(jnp.where(eye, row, 0.0), axis=1, keepdims=True)


def _cumsum_lanes(x, lane):
    n = x.shape[1]
    shift = 1
    while shift < n:
        x = x + jnp.where(lane >= shift, pltpu.roll(x, shift, 1), 0.0)
        shift *= 2
    return x


def _mlstm_parts(u_ref, v_ref, og_ref, prev_ref, ig_ref, fg_ref, ib_ref, fb_ref, cw_ref, cb_ref,
                 wq_ref, wk_ref, c0_ref, n0_ref, m0_ref, gn_ref, skip_ref,
                 o_ref, cout_ref, nout_ref, mout_ref,
                 xp_scr, uc_scr, c_scr, n_scr, m_scr, *, t, l_true):
    L = CHUNK
    halo = SUBLANES

    xp_scr[0:halo, :] = prev_ref[...]
    xp_scr[halo:halo + t, :] = u_ref[...]
    conv = cb_ref[...]
    for j in range(ML_CONV):
        start = halo - (ML_CONV - 1) + j
        conv = conv + xp_scr[start:start + t, :] * cw_ref[j:j + 1, :]
    uc_scr[...] = conv * jax.nn.sigmoid(conv)

    wq_bf = wq_ref[...].astype(BF16)
    wk_bf = wk_ref[...].astype(BF16)
    c_scr[...] = c0_ref[...]
    n_scr[...] = n0_ref[...]
    m_scr[...] = m0_ref[...]

    ti = lax.broadcasted_iota(jnp.int32, (L, L), 0)
    si = lax.broadcasted_iota(jnp.int32, (L, L), 1)
    eye = ti == si
    causal = si <= ti
    lane = lax.broadcasted_iota(jnp.int32, (1, L), 1)

    def body(c):
        rows = pl.ds(c * L if isinstance(c, int) else pl.multiple_of(c * L, L), L)
        uc = uc_scr[rows, :]
        uc_bf = uc.astype(BF16)
        q = jnp.dot(uc_bf, wq_bf, preferred_element_type=F32)
        k = jnp.dot(uc_bf, wk_bf, preferred_element_type=F32) * (HEAD_DIM ** -0.5)
        q_bf, k_bf, v_bf = q.astype(BF16), k.astype(BF16), v_ref[rows, :].astype(BF16)

        ig = ig_ref[:, rows] + ib_ref[...]
        fg = fg_ref[:, rows] + fb_ref[...]
        lf = -(jnp.maximum(-fg, 0.0) + jnp.log(1.0 + jnp.exp(-jnp.abs(fg))))
        b_row = _cumsum_lanes(lf, lane)
        b_col = _row_to_col(b_row, eye)
        m_old = m_scr[...]
        c_old = c_scr[...]
        n_old = n_scr[...]

        dlog = jnp.where(causal, b_col + (ig - b_row), -jnp.inf)
        inter = b_col + m_old
        m_row = jnp.maximum(inter, jnp.max(dlog, axis=1, keepdims=True))
        w = jnp.exp(dlog - m_row)
        s_in = jnp.exp(inter - m_row)
        a = w * lax.dot_general(q_bf, k_bf, NT_DIMS, preferred_element_type=F32)
        num = (jnp.dot(a.astype(BF16), v_bf, preferred_element_type=F32)
               + jnp.dot(q_bf, c_old.astype(BF16), preferred_element_type=F32) * s_in)
        qn = jnp.sum(q_bf.astype(F32) * n_old.astype(BF16).astype(F32), axis=1, keepdims=True)
        den = jnp.sum(a, axis=1, keepdims=True) + qn * s_in
        hh = num / jnp.maximum(jnp.abs(den), jnp.exp(-m_row))

        og = og_ref[rows, :]
        y = _rms(hh) * gn_ref[...] + skip_ref[...] * uc
        o_ref[rows, :] = (jax.nn.sigmoid(og) * y).astype(o_ref.dtype)

        b_last = jnp.sum(jnp.where(lane == l_true - 1, b_row, 0.0), axis=1, keepdims=True)
        dlast = jnp.where(lane < l_true, b_last - b_row + ig, -jnp.inf)
        m_new = jnp.maximum(b_last + m_old, jnp.max(dlast, axis=1, keepdims=True))
        ws_col = _row_to_col(jnp.exp(dlast - m_new), eye)
        s_old = jnp.exp(b_last + m_old - m_new)
        kw = k * ws_col
        c_scr[...] = c_old * s_old + jnp.dot(kw.T.astype(BF16), v_bf, preferred_element_type=F32)
        n_scr[...] = n_old * s_old + jnp.sum(kw, axis=0, keepdims=True)
        m_scr[...] = m_new

    def finalize():
        cout_ref[...] = c_scr[...]
        nout_ref[...] = n_scr[...]
        mout_ref[...] = m_scr[...]

    return body, finalize


N_ML_IN, N_ML_OUT, N_ML_SCR = 17, 4, 5


def _mlstm_call(proj, prev, gates, gbias, cw, cb, wq, wk, c0, n0, m0, gn, skip, l):
    b, t, _ = proj.shape

    def col(c0_):
        return pl.BlockSpec((None, t, HEAD_DIM), lambda b, h: (b, 0, c0_ // HEAD_DIM + h))

    def per_head(rows):
        return pl.BlockSpec((None, rows, HEAD_DIM), lambda b, h: (l, 0, h))

    c_spec = pl.BlockSpec((None, None, HEAD_DIM, HEAD_DIM), lambda b, h: (b, h, 0, 0))
    n_spec = pl.BlockSpec((None, None, 1, HEAD_DIM), lambda b, h: (b, h, 0, 0))
    m_spec = pl.BlockSpec((None, None, 1, 1), lambda b, h: (b, h, 0, 0))
    w_spec = pl.BlockSpec((None, None, HEAD_DIM, HEAD_DIM), lambda b, h: (l, h, 0, 0))
    in_specs = [
        col(COL_MU), col(COL_MV), col(COL_MO),
        pl.BlockSpec((None, SUBLANES, HEAD_DIM), lambda b, h: (b, 0, h)),
        pl.BlockSpec((None, None, 1, t), lambda b, h: (b, h, 0, 0)),
        pl.BlockSpec((None, None, 1, t), lambda b, h: (b, N_ML + h, 0, 0)),
        pl.BlockSpec((None, None, 1, 1), lambda b, h: (l, h, 0, 0)),
        pl.BlockSpec((None, None, 1, 1), lambda b, h: (l, N_ML + h, 0, 0)),
        per_head(ML_CONV), per_head(1),
        w_spec, w_spec, c_spec, n_spec, m_spec,
        per_head(1), per_head(1),
    ]
    out_specs = [pl.BlockSpec((None, t, HEAD_DIM), lambda b, h: (b, 0, h)), c_spec, n_spec, m_spec]
    out_shape = [jax.ShapeDtypeStruct((b, t, ML_W), BF16),
                 jax.ShapeDtypeStruct((b, N_ML, HEAD_DIM, HEAD_DIM), F32),
                 jax.ShapeDtypeStruct((b, N_ML, 1, HEAD_DIM), F32),
                 jax.ShapeDtypeStruct((b, N_ML, 1, 1), F32)]
    scratch = [pltpu.VMEM((SUBLANES + t, HEAD_DIM), F32), pltpu.VMEM((t, HEAD_DIM), F32),
               pltpu.VMEM((HEAD_DIM, HEAD_DIM), F32), pltpu.VMEM((1, HEAD_DIM), F32),
               pltpu.VMEM((1, 1), F32)]
    args = (proj, proj, proj, prev, gates[:, :, None, :], gates[:, :, None, :], gbias, gbias,
            cw, cb, wq, wk, c0, n0, m0, gn, skip)
    return in_specs, args, out_specs, out_shape, scratch


def _seqmix_kernel(*refs, t, n_chunks, l_true):
    assert N_RET == N_ML
    n_in, n_out = N_RET_IN + N_ML_IN, N_RET_OUT + N_ML_OUT
    ins, outs, scr = refs[:n_in], refs[n_in:n_in + n_out], refs[n_in + n_out:]
    ret_body, ret_fin = _ret_parts(*ins[:N_RET_IN], *outs[:N_RET_OUT], *scr[:N_RET_SCR],
                                   l_true=l_true)
    ml_body, ml_fin = _mlstm_parts(*ins[N_RET_IN:], *outs[N_RET_OUT:], *scr[N_RET_SCR:],
                                   t=t, l_true=l_true)

    if n_chunks <= CHUNK_UNROLL:
        for c in range(n_chunks):
            ret_body(c)
            ml_body(c)
    else:
        def body(c, carry):
            ret_body(c)
            ml_body(c)
            return carry

        lax.fori_loop(0, n_chunks, body, 0, unroll=CHUNK_UNROLL)
    ret_fin()
    ml_fin()


def _seqmix(ret_call, ml_call, b, t, l_true):
    in_specs, args, out_specs, out_shape, scratch = (r + m for r, m in zip(ret_call, ml_call))
    return pl.pallas_call(
        functools.partial(_seqmix_kernel, t=t, n_chunks=t // CHUNK, l_true=l_true),
        grid=(b, N_RET),
        in_specs=in_specs,
        out_specs=out_specs,
        out_shape=out_shape,
        scratch_shapes=scratch,
        compiler_params=_params(("arbitrary", "arbitrary")),
        name="seqmix",
    )(*args)


def _conv_gate_kernel(ug_ref, uv_ref, hg_ref, hv_ref, pg_ref, pv_ref, wg_ref, wv_ref,
                      bg_ref, bv_ref, o_ref, sg, sv, *, tt):
    first = pl.program_id(1) == 0
    halo = SUBLANES

    def conv(u_ref, h_ref, p_ref, w_ref, b_ref, scr):
        @pl.when(first)
        def _():
            scr[0:halo, :] = p_ref[...]

        @pl.when(jnp.logical_not(first))
        def _():
            scr[0:halo, :] = h_ref[...]

        scr[halo:halo + tt, :] = u_ref[...]
        y = b_ref[...]
        for j in range(FFN_CONV):
            start = halo - (FFN_CONV - 1) + j
            y = y + scr[start:start + tt, :] * w_ref[j:j + 1, :]
        return y

    gate = conv(ug_ref, hg_ref, pg_ref, wg_ref, bg_ref, sg)
    val = conv(uv_ref, hv_ref, pv_ref, wv_ref, bv_ref, sv)
    o_ref[...] = (gate * jax.nn.sigmoid(gate) * val).astype(o_ref.dtype)


def _conv_gate(up, prev, cw, cb, l):
    b, t, f2 = up.shape
    f = f2 // 2
    tt = _tile(t, 512, SUBLANES)
    tn = _tile(f, max(512, 512 * 512 // tt))
    nf = f // tn
    rb = tt // SUBLANES

    def main(off):
        return pl.BlockSpec((None, tt, tn), lambda b, i, n: (b, i, n + off))

    def halo(off):
        return pl.BlockSpec((None, SUBLANES, tn),
                            lambda b, i, n: (b, jnp.maximum(i * rb - 1, 0), n + off))

    def prev_spec(off):
        return pl.BlockSpec((None, SUBLANES, tn), lambda b, i, n: (b, 0, n + off))

    def w_spec(rows, off):
        return pl.BlockSpec((None, rows, tn), lambda b, i, n: (l, 0, n + off))

    return pl.pallas_call(
        functools.partial(_conv_gate_kernel, tt=tt),
        grid=(b, t // tt, nf),
        in_specs=[main(0), main(nf), halo(0), halo(nf), prev_spec(0), prev_spec(nf),
                  w_spec(FFN_CONV, 0), w_spec(FFN_CONV, nf), w_spec(1, 0), w_spec(1, nf)],
        out_specs=pl.BlockSpec((None, tt, tn), lambda b, i, n: (b, i, n)),
        out_shape=jax.ShapeDtypeStruct((b, t, f), BF16),
        scratch_shapes=[pltpu.VMEM((SUBLANES + tt, tn), F32), pltpu.VMEM((SUBLANES + tt, tn), F32)],
        compiler_params=_params(("arbitrary", "arbitrary", "arbitrary")),
        name="conv_gate",
    )(up, up, up, up, prev, prev, cw, cw, cb, cb)


def _ffn_up_kernel(x_ref, wg_ref, wv_ref, pg_ref, pv_ref, cwg_ref, cwv_ref, cbg_ref, cbv_ref,
                   o_ref, sg_ref, sv_ref, wbf_g, wbf_v, scr_g, scr_v, *, tm, n_sub):
    t = pl.program_id(2)
    halo = SUBLANES

    @pl.when(jnp.logical_and(pl.program_id(1) == 0, t == 0))
    def _():
        wbf_g[...] = wg_ref[...].astype(BF16)
        wbf_v[...] = wv_ref[...].astype(BF16)

    @pl.when(t == 0)
    def _():
        scr_g[0:halo, :] = pg_ref[...]
        scr_v[0:halo, :] = pv_ref[...]

    sub = tm // n_sub

    def project(c):
        rows = slice(c * sub, (c + 1) * sub)
        x = x_ref[rows, :]
        dst = slice(halo + c * sub, halo + (c + 1) * sub)
        scr_g[dst, :] = jnp.dot(x, wbf_g[...], preferred_element_type=F32)
        scr_v[dst, :] = jnp.dot(x, wbf_v[...], preferred_element_type=F32)

    def conv(c, scr, cw_ref, cb_ref):
        y = cb_ref[...]
        for j in range(FFN_CONV):
            start = halo - (FFN_CONV - 1) + j + c * sub
            y = y + scr[start:start + sub, :] * cw_ref[j:j + 1, :]
        return y

    project(0)
    for c in range(n_sub):
        if c + 1 < n_sub:
            project(c + 1)
        gate = conv(c, scr_g, cwg_ref, cbg_ref)
        val = conv(c, scr_v, cwv_ref, cbv_ref)
        o_ref[c * sub:(c + 1) * sub, :] = (gate * jax.nn.sigmoid(gate) * val).astype(o_ref.dtype)

    for scr, s_ref in ((scr_g, sg_ref), (scr_v, sv_ref)):
        tail = scr[tm:tm + halo, :]
        s_ref[...] = tail
        scr[0:halo, :] = tail


def _ffn_up_fused(x, w_up, prev, cw, cb, l):
    b, t, d = x.shape
    f = w_up.shape[2] // 2
    tm = _tile(t, 1024, SUBLANES)
    tn = _tile(f, 512)
    nf = f // tn

    def cols(rows, off, arr_l):
        return pl.BlockSpec((None, rows, tn), lambda n, b, i: (arr_l(b), 0, n + off))

    layer = lambda b: l
    batch = lambda b: b
    tail_spec = pl.BlockSpec((None, SUBLANES, tn), lambda n, b, i: (b, 0, n))
    return pl.pallas_call(
        functools.partial(_ffn_up_kernel, tm=tm, n_sub=FFN_SUB if tm % (FFN_SUB * 16) == 0 else 1),
        grid=(nf, b, t // tm),
        in_specs=[
            pl.BlockSpec((None, tm, d), lambda n, b, i: (b, i, 0)),
            cols(d, 0, layer), cols(d, nf, layer),
            cols(SUBLANES, 0, batch), cols(SUBLANES, nf, batch),
            cols(FFN_CONV, 0, layer), cols(FFN_CONV, nf, layer),
            cols(1, 0, layer), cols(1, nf, layer),
        ],
        out_specs=[pl.BlockSpec((None, tm, tn), lambda n, b, i: (b, i, n)), tail_spec, tail_spec],
        out_shape=[jax.ShapeDtypeStruct((b, t, f), BF16),
                   jax.ShapeDtypeStruct((b, SUBLANES, f), F32),
                   jax.ShapeDtypeStruct((b, SUBLANES, f), F32)],
        scratch_shapes=[pltpu.VMEM((d, tn), BF16), pltpu.VMEM((d, tn), BF16),
                        pltpu.VMEM((SUBLANES + tm, tn), F32), pltpu.VMEM((SUBLANES + tm, tn), F32)],
        compiler_params=_params(("arbitrary", "arbitrary", "arbitrary")),
        name="ffn_up_conv",
    )(x, w_up, w_up, prev, prev, cw, cw, cb, cb)


def _rope_tables(pos, rows):
    half = HEAD_DIM // 2
    inv = ROPE_BASE ** (-jnp.arange(half, dtype=F32) / half)
    ang = pos.astype(F32)[:, None] * inv[None, :]
    cos, sin = jnp.cos(ang), jnp.sin(ang)
    cos2 = jnp.concatenate([cos, cos], axis=-1)
    sin2 = jnp.concatenate([-sin, sin], axis=-1)
    pad = rows - pos.shape[0]
    if pad:
        cos2 = jnp.pad(cos2, ((0, pad), (0, 0)))
        sin2 = jnp.pad(sin2, ((0, pad), (0, 0)))
    return cos2, sin2


def _prev_rows(buf):
    return jnp.pad(buf, ((0, 0), (SUBLANES - buf.shape[1], 0), (0, 0)))


def _layer(l, lam_init, x, mods, seq_shape, st, w, attn_fn, rope, lg_tab, kv_stacks):
    b, t = seq_shape
    bm, tmx, d = x.shape
    sh1, sc1, g1, sh2, sc2, g2 = mods
    t_pad = -(-t // CHUNK) * CHUNK
    l_true = CHUNK if t % CHUNK == 0 else t

    h = _norm(x, w["g_mix"], l, sc1, sh1)

    def in_proj(col0, width, tn, name, stack=None):
        assert col0 % tn == 0 and width % tn == 0
        return _matmul([h], w["w_in_t"], l, width // tn, tn, 1024, name=name, w_transposed=True,
                       w_tile=lambda n: n + col0 // tn, stack=stack)

    depth = w["w_in_t"].shape[0]
    q_new = in_proj(W_COL_Q, DA_W, 1024, "q_proj").reshape(b, t, DA_W)
    k_stack = in_proj(W_COL_K, DA_W, 1024, "k_proj", stack=(depth, kv_stacks[0]))
    v_stack = in_proj(W_COL_V, DA_W, 1024, "v_proj", stack=(depth, kv_stacks[1]))
    ret_in = in_proj(W_COL_RET, 4 * RET_W, 1024, "ret_proj").reshape(b, t, 4 * RET_W)
    ml_in = in_proj(W_COL_ML, 3 * ML_W, 512, "ml_proj").reshape(b, t, 3 * ML_W)
    gates = _gates(h, w["w_in_t"], l)

    oa = attn_fn(q_new, k_stack, v_stack)
    if (bm, tmx) != (b, t):
        gates = gates.reshape(GATE_ROWS, b, t).transpose(1, 0, 2)
    ret_pad, ml_pad = ret_in, ml_in
    if t_pad != t:
        ret_pad = jnp.pad(ret_in, ((0, 0), (0, t_pad - t), (0, 0)))
        ml_pad = jnp.pad(ml_in, ((0, 0), (0, t_pad - t), (0, 0)))
        gates = jnp.pad(gates, ((0, 0), (0, 0), (0, t_pad - t)))

    ret_call = _retention_call(ret_pad, rope[0], rope[1], lg_tab, st["ret_s"], w["ret_norm_g"], l)
    ml_call = _mlstm_call(
        ml_pad, _prev_rows(st["ml_conv"]), gates, w["gbias"], w["ml_conv_w"], w["ml_conv_b"],
        w["ml_wq"], w["ml_wk"], st["ml_c"], st["ml_n"][:, :, None, :], st["ml_m"][:, :, None, None],
        w["ml_norm_g"], w["ml_skip"], l)
    ro, ret_s, mo, ml_c, ml_n, ml_m = _seqmix(ret_call, ml_call, b, t_pad, l_true)
    if t_pad != t:
        ro, mo = ro[:, :t], mo[:, :t]
    mix = [a.astype(BF16).reshape(bm, tmx, a.shape[-1]) for a in (oa, ro, mo)]
    x = _matmul(mix, w["w_out"], l, d // _tile(d, 1024), _tile(d, 1024), 1024,
                res=x, gate=g1, name="out_proj")

    h2 = _norm(x, w["g_ffn"], l, sc2, sh2)
    f2 = w["ffn_w_up"].shape[2]
    prev_ffn = _prev_rows(st["ffn_conv"])
    if (bm, tmx) == (b, t) and t % SUBLANES == 0:
        act, tail_g, tail_v = _ffn_up_fused(h2, w["ffn_w_up"], prev_ffn, w["ffn_conv_w"],
                                            w["ffn_conv_b"], l)
        ffn_conv = jnp.concatenate([tail_g, tail_v], axis=-1)[:, SUBLANES - (FFN_CONV - 1):, :]
    else:
        up = _matmul([h2], w["ffn_w_up"], l, f2 // _tile(f2, 1024), _tile(f2, 1024), 1024,
                     name="ffn_up")
        up_seq = up.reshape(b, t, f2)
        act = _conv_gate(up_seq, prev_ffn, w["ffn_conv_w"], w["ffn_conv_b"], l)
        ffn_conv = up_seq[:, t - (FFN_CONV - 1):, :]
    x = _matmul([act.reshape(bm, tmx, f2 // 2)], w["ffn_w_down"], l, d // _tile(d, 512),
                _tile(d, 512), 512, res=x, gate=g2, name="ffn_down")

    new_st = dict(
        ret_s=ret_s, ml_c=ml_c, ml_n=ml_n[:, :, 0, :], ml_m=ml_m[:, :, 0, 0],
        ml_conv=ml_in[:, t - (ML_CONV - 1):, COL_MU:COL_MU + ML_W], ffn_conv=ffn_conv)
    return x, (k_stack, v_stack), new_st


def kernel(x_prompt, x_sample, c_prompt, c_sample, cache_k, cache_v, page_table, state_ret, state_ml_c, state_ml_n, state_ml_m, state_ml_conv, state_ffn_conv, w_ada, b_ada, g_mix, w_in, lam_q1, lam_k1, lam_q2, lam_k2, da_norm_g, ret_norm_g, ml_conv_w, ml_conv_b, ml_wq, ml_wk, ml_b_i, ml_b_f, ml_norm_g, ml_skip, w_out, g_ffn, ffn_w_up, ffn_conv_w, ffn_conv_b, ffn_w_down, g_final):
    bp, tp, d = x_prompt.shape
    bs, ts, _ = x_sample.shape
    depth = w_in.shape[0]
    f2 = ffn_w_up.shape[2]
    past = page_table.shape[1] * PAGE

    def per_layer_rows(a):
        return a.reshape(depth, 1, a.shape[-1])

    zeros8 = jnp.zeros((depth, GATE_ROWS - N_GATES), F32)
    w = dict(
        g_mix=g_mix, g_ffn=g_ffn, w_out=w_out, ffn_w_up=ffn_w_up, ffn_w_down=ffn_w_down,
        w_in_t=jnp.swapaxes(w_in, 1, 2),
        gbias=jnp.concatenate([ml_b_i, ml_b_f, zeros8], axis=1)[:, :, None, None],
        lam_p=jnp.stack([lam_q1, lam_k1, lam_q2, lam_k2], axis=1),
        da_norm_g=per_layer_rows(da_norm_g), ret_norm_g=per_layer_rows(ret_norm_g),
        ml_conv_w=ml_conv_w, ml_conv_b=per_layer_rows(ml_conv_b), ml_wq=ml_wq, ml_wk=ml_wk,
        ml_norm_g=per_layer_rows(ml_norm_g), ml_skip=per_layer_rows(ml_skip),
        ffn_conv_w=ffn_conv_w, ffn_conv_b=per_layer_rows(ffn_conv_b))

    n_c = bp + bs
    c_rows = -(-n_c // SUBLANES) * SUBLANES
    c_all = jnp.pad(jnp.concatenate([c_prompt, c_sample], axis=0), ((0, c_rows - n_c), (0, 0)))
    mod = _ada(c_all, w_ada, b_ada)

    lg_np = np.log(1.0 - 2.0 ** (-5.0 - np.arange(N_RET, dtype=np.float64)))
    lg_tab = jnp.asarray(np.broadcast_to(lg_np[:, None, None], (N_RET, 1, HEAD_DIM)), F32)
    tp_pad = -(-tp // CHUNK) * CHUNK
    ts_pad = -(-ts // CHUNK) * CHUNK
    rope_p = _rope_tables(jnp.arange(tp), tp_pad)
    rope_s = _rope_tables(past + jnp.arange(ts), ts_pad)

    zero_st = dict(
        ret_s=jnp.zeros((bp, N_RET, HEAD_DIM, HEAD_DIM), F32),
        ml_c=jnp.zeros((bp, N_ML, HEAD_DIM, HEAD_DIM), F32),
        ml_n=jnp.zeros((bp, N_ML, HEAD_DIM), F32),
        ml_m=jnp.zeros((bp, N_ML), F32),
        ml_conv=jnp.zeros((bp, ML_CONV - 1, ML_W), F32),
        ffn_conv=jnp.zeros((bp, FFN_CONV - 1, f2), F32))
    names = ("ret_s", "ml_c", "ml_n", "ml_m", "ml_conv", "ffn_conv")
    out_p = {n: [] for n in names}
    out_s = {n: [] for n in names}
    kv_p = kv_s = (None, None)
    yp = x_prompt
    ys = x_sample.reshape(1, bs * ts, d)
    for l in range(depth):
        lam_init = 0.8 - 0.6 * math.exp(-0.3 * l)
        parts = jnp.split(mod[l], 6, axis=-1)
        mods_p = [m[:bp, None, :] for m in parts]
        mods_s = [jnp.repeat(m[bp:n_c], ts, axis=0)[None] for m in parts]

        attn_p = functools.partial(_attn_prompt, lam_p=w["lam_p"], da_g=w["da_norm_g"], l=l,
                                   lam_init=lam_init)
        yp, kv_p, st_p = _layer(l, lam_init, yp, mods_p, (bp, tp), zero_st, w, attn_p,
                                rope_p, lg_tab, kv_p)

        attn_s = functools.partial(_attn_decode, cache_k=cache_k, cache_v=cache_v,
                                   page_table=page_table, lam_p=w["lam_p"], da_g=w["da_norm_g"],
                                   l=l, lam_init=lam_init)
        st_in = dict(ret_s=state_ret[l], ml_c=state_ml_c[l], ml_n=state_ml_n[l], ml_m=state_ml_m[l],
                     ml_conv=state_ml_conv[l], ffn_conv=state_ffn_conv[l])
        ys, kv_s, st_s = _layer(l, lam_init, ys, mods_s, (bs, ts), st_in, w, attn_s,
                                rope_s, lg_tab, kv_s)
        for n in names:
            out_p[n].append(st_p[n])
            out_s[n].append(st_s[n])

    y_prompt = _norm(yp, g_final, None, out_dtype=F32)
    y_sample = _norm(ys, g_final, None, out_dtype=F32).reshape(bs, ts, d)
    kv_shape_p = (depth, bp, tp, N_DA, HEAD_DIM)
    kv_shape_s = (depth, bs, ts, N_DA, HEAD_DIM)
    return (y_prompt, y_sample,
            kv_p[0].reshape(kv_shape_p), kv_p[1].reshape(kv_shape_p),
            kv_s[0].reshape(kv_shape_s), kv_s[1].reshape(kv_shape_s),
            jnp.stack(out_p["ret_s"]), jnp.stack(out_s["ret_s"]),
            jnp.stack(out_p["ml_c"]), jnp.stack(out_s["ml_c"]),
            jnp.stack(out_p["ml_n"]), jnp.stack(out_s["ml_n"]),
            jnp.stack(out_p["ml_m"]), jnp.stack(out_s["ml_m"]),
            jnp.stack(out_p["ml_conv"]), jnp.stack(out_s["ml_conv"]),
            jnp.stack(out_p["ffn_conv"]), jnp.stack(out_s["ffn_conv"]))
```

```python
import functools
import math

import numpy as np
import jax
import jax.numpy as jnp
from jax import lax
from jax.experimental import pallas as pl
from jax.experimental.pallas import tpu as pltpu

F32 = jnp.float32
BF16 = jnp.bfloat16

HEAD_DIM = 128
DA_QK = HEAD_DIM // 2
N_DA = 8
N_RET = 4
N_ML = 4
DA_W = N_DA * HEAD_DIM
RET_W = N_RET * HEAD_DIM
ML_W = N_ML * HEAD_DIM
ML_CONV = 4
FFN_CONV = 3
CHUNK = 128
PAGE = 128
RMS_EPS = 1e-6
ROPE_BASE = 10000.0
NEG_INF = -1e30

W_COL_Q, W_COL_K, W_COL_V = 0, DA_W, 2 * DA_W
W_COL_RET = 3 * DA_W
W_COL_ML = W_COL_RET + 4 * RET_W
W_COL_GATES = W_COL_ML + 3 * ML_W
COL_RQ, COL_RK, COL_RV, COL_RG = 0, RET_W, 2 * RET_W, 3 * RET_W
COL_MU, COL_MV, COL_MO = 0, ML_W, 2 * ML_W
N_GATES = 2 * N_ML
GATE_ROWS = 16

SUBLANES = 8
LANES = 128
VMEM_LIMIT = 56 * 1024 * 1024
FFN_SUB = 4
CHUNK_UNROLL = 16

NT_DIMS = (((1,), (1,)), ((), ()))
TN_DIMS = (((0,), (0,)), ((), ()))


def _tile(n, pref, mult=LANES):
    if n <= pref:
        return n
    t = (pref // mult) * mult
    while t >= mult:
        if n % t == 0:
            return t
        t -= mult
    return n


def _params(sem):
    return pltpu.CompilerParams(dimension_semantics=sem, vmem_limit_bytes=VMEM_LIMIT)


def _rms(x):
    return x * lax.rsqrt(jnp.mean(x * x, axis=-1, keepdims=True) + RMS_EPS)


def _ada_kernel(c_ref, w_ref, b_ref, o_ref):
    c = c_ref[...]
    a = (c * jax.nn.sigmoid(c)).astype(BF16)
    o_ref[...] = jnp.dot(a, w_ref[...].astype(BF16), preferred_element_type=F32) + b_ref[...]


def _ada(c_all, w_ada, b_ada):
    depth, d, n = w_ada.shape
    rows = c_all.shape[0]
    tn = _tile(n, 1024)
    return pl.pallas_call(
        _ada_kernel,
        grid=(depth, n // tn),
        in_specs=[
            pl.BlockSpec((rows, d), lambda l, j: (0, 0)),
            pl.BlockSpec((None, d, tn), lambda l, j: (l, 0, j)),
            pl.BlockSpec((None, 1, tn), lambda l, j: (l, 0, j)),
        ],
        out_specs=pl.BlockSpec((None, rows, tn), lambda l, j: (l, 0, j)),
        out_shape=jax.ShapeDtypeStruct((depth, rows, n), F32),
        compiler_params=_params(("arbitrary", "arbitrary")),
        name="ada",
    )(c_all, w_ada, b_ada.reshape(depth, 1, n))


def _norm_kernel(*refs, modulated):
    if modulated:
        x_ref, g_ref, sc_ref, sh_ref, o_ref = refs
    else:
        x_ref, g_ref, o_ref = refs
    y = _rms(x_ref[...]) * g_ref[...]
    if modulated:
        y = y * (1.0 + sc_ref[...]) + sh_ref[...]
    o_ref[...] = y.astype(o_ref.dtype)


def _mod_spec(mod, tm, tn, with_n):
    per_row = mod.shape[1] != 1
    rows = tm if per_row else 1
    if with_n:
        return pl.BlockSpec((None, rows, tn), lambda n, b, t: (b, t if per_row else 0, n))
    return pl.BlockSpec((None, rows, tn), lambda b, t: (b, t if per_row else 0, 0))


def _norm(x, g, l, sc=None, sh=None, out_dtype=BF16):
    bm, tmx, d = x.shape
    tm = _tile(tmx, 512, SUBLANES)
    modulated = sc is not None
    g_spec = (pl.BlockSpec((None, 1, d), lambda b, t: (l, 0, 0)) if l is not None
              else pl.BlockSpec((1, d), lambda b, t: (0, 0)))
    in_specs = [pl.BlockSpec((None, tm, d), lambda b, t: (b, t, 0)), g_spec]
    args = [x, g.reshape(g.shape[0], 1, d) if l is not None else g.reshape(1, d)]
    if modulated:
        in_specs += [_mod_spec(sc, tm, d, False), _mod_spec(sh, tm, d, False)]
        args += [sc, sh]
    return pl.pallas_call(
        functools.partial(_norm_kernel, modulated=modulated),
        grid=(bm, tmx // tm),
        in_specs=in_specs,
        out_specs=pl.BlockSpec((None, tm, d), lambda b, t: (b, t, 0)),
        out_shape=jax.ShapeDtypeStruct((bm, tmx, d), out_dtype),
        compiler_params=_params(("arbitrary", "arbitrary")),
        name="norm",
    )(*args)


def _mm_kernel(*refs, row_offs, row_sizes, has_res, has_side, w_transposed):
    n_in = len(row_offs)
    it = iter(refs)
    main = [[next(it) for _ in range(n_in)]]
    w_ref = next(it)
    main += [next(it), next(it)] if has_res else [None, None]
    if has_side:
        side = [[next(it) for _ in range(n_in)]]
        side += [next(it), next(it)] if has_res else [None, None]
    wbf_ref = refs[-1]
    o_ref, side_o_ref = (refs[-3], refs[-2]) if has_side else (refs[-2], None)

    def project(x_refs, res_ref, gate_ref, out_ref):
        acc = None
        for x_ref, off, size in zip(x_refs, row_offs, row_sizes):
            if w_transposed:
                part = lax.dot_general(x_ref[...], wbf_ref[:, off:off + size], NT_DIMS,
                                       preferred_element_type=F32)
            else:
                part = jnp.dot(x_ref[...], wbf_ref[off:off + size, :], preferred_element_type=F32)
            acc = part if acc is None else acc + part
        if res_ref is not None:
            acc = res_ref[...] + gate_ref[...] * acc
        out_ref[...] = acc.astype(out_ref.dtype)

    @pl.when(jnp.logical_and(pl.program_id(1) == 0, pl.program_id(2) == 0))
    def _():
        wbf_ref[...] = w_ref[...].astype(BF16)
        if has_side:
            project(*side, side_o_ref)

    project(*main, o_ref)


def _matmul(xs, w, l, n_tiles, tn, tm_pref, res=None, gate=None, w_tile=lambda n: n,
            w_transposed=False, stack=None, side=None, name="mm"):
    bm, tmx = xs[0].shape[:2]
    k = w.shape[2] if w_transposed else w.shape[1]
    sizes = [x.shape[2] for x in xs]
    offs = [sum(sizes[:i]) for i in range(len(xs))]
    assert sum(sizes) == k
    tm = _tile(tmx, tm_pref, SUBLANES)
    in_specs = [pl.BlockSpec((None, tm, s), lambda n, b, t: (b, t, 0)) for s in sizes]
    if w_transposed:
        in_specs.append(pl.BlockSpec((None, tn, k), lambda n, b, t: (l, w_tile(n), 0)))
    else:
        in_specs.append(pl.BlockSpec((None, k, tn), lambda n, b, t: (l, 0, w_tile(n))))
    args = list(xs) + [w]
    if res is not None:
        in_specs.append(pl.BlockSpec((None, tm, tn), lambda n, b, t: (b, t, n)))
        in_specs.append(_mod_spec(gate, tm, tn, True))
        args += [res, gate]
    out_specs = [pl.BlockSpec((None, tm, tn), lambda n, b, t: (b, t, n))]
    out_shape = [jax.ShapeDtypeStruct((bm, tmx, n_tiles * tn), F32)]
    if side is not None:
        side_xs, side_res, side_gate = side
        ts = side_xs[0].shape[1]
        side_cols = pl.BlockSpec((None, ts, tn), lambda n, b, t: (0, 0, n))
        in_specs += [pl.BlockSpec((None, ts, s), lambda n, b, t: (0, 0, 0)) for s in sizes]
        args += list(side_xs)
        if res is not None:
            in_specs += [side_cols, side_cols]
            args += [side_res, side_gate]
        out_specs.append(side_cols)
        out_shape.append(jax.ShapeDtypeStruct((1, ts, n_tiles * tn), F32))
    aliases = {}
    if stack is not None:
        depth, buf = stack
        out_specs[0] = pl.BlockSpec((None, None, tm, tn), lambda n, b, t: (l, b, t, n))
        out_shape[0] = jax.ShapeDtypeStruct((depth, bm, tmx, n_tiles * tn), F32)
        if buf is not None:
            in_specs.append(pl.BlockSpec(memory_space=pl.ANY))
            args.append(buf)
            aliases = {len(args) - 1: 0}
    outs = pl.pallas_call(
        functools.partial(_mm_kernel, row_offs=tuple(offs), row_sizes=tuple(sizes),
                          has_res=res is not None, has_side=side is not None,
                          w_transposed=w_transposed),
        grid=(n_tiles, bm, tmx // tm),
        in_specs=in_specs,
        out_specs=out_specs,
        out_shape=out_shape,
        input_output_aliases=aliases,
        scratch_shapes=[pltpu.VMEM((tn, k) if w_transposed else (k, tn), BF16)],
        compiler_params=_params(("arbitrary", "arbitrary", "arbitrary")),
        name=name,
    )(*args)
    return outs if side is not None else outs[0]


def _gates_kernel(h_ref, wg_ref, o_ref):
    row = lax.broadcasted_iota(jnp.int32, wg_ref.shape, 0)
    wg = jnp.where(row < N_GATES, wg_ref[...], 0.0).astype(BF16)
    o_ref[...] = lax.dot_general(wg, h_ref[...], NT_DIMS, preferred_element_type=F32)


def _gates(h, w_in_t, l):
    bm, tmx, d = h.shape
    tm = _tile(tmx, 1024)
    assert W_COL_GATES % GATE_ROWS == 0 and w_in_t.shape[1] == W_COL_GATES + N_GATES
    return pl.pallas_call(
        _gates_kernel,
        grid=(bm, tmx // tm),
        in_specs=[pl.BlockSpec((None, tm, d), lambda b, t: (b, t, 0)),
                  pl.BlockSpec((None, GATE_ROWS, d), lambda b, t: (l, W_COL_GATES // GATE_ROWS, 0))],
        out_specs=pl.BlockSpec((None, GATE_ROWS, tm), lambda b, t: (b, 0, t)),
        out_shape=jax.ShapeDtypeStruct((bm, GATE_ROWS, tmx), F32),
        compiler_params=_params(("arbitrary", "arbitrary")),
        name="gates",
    )(h, w_in_t)


def _lambda(lam_ref, lam_init):
    lp = lam_ref[...]
    a = jnp.exp(jnp.sum(lp[0:1] * lp[1:2], axis=1, keepdims=True))
    b = jnp.exp(jnp.sum(lp[2:3] * lp[3:4], axis=1, keepdims=True))
    return a - b + lam_init


SCORE_SCALE_LOG2 = DA_QK ** -0.5 * math.log2(math.e)


def _online_softmax_step(s, v_bf, m_scr, l_scr, acc_scr):
    m_prev = m_scr[...]
    m_new = jnp.maximum(m_prev, jnp.max(s, axis=1, keepdims=True))
    alpha = jnp.exp2(m_prev - m_new)
    p = jnp.exp2(s - m_new)
    l_scr[...] = alpha * l_scr[...] + jnp.sum(p, axis=1, keepdims=True)
    acc_scr[...] = alpha * acc_scr[...] + jnp.dot(p.astype(BF16), v_bf, preferred_element_type=F32)
    m_scr[...] = m_new


def _attn_prompt_block(qi, q_ref, kbf, vbf, lam, g_ref, o_ref, *, tq, lam_init):
    n0 = qi * tq
    q = q_ref[n0:n0 + tq, :] * SCORE_SCALE_LOG2
    lane = lax.broadcasted_iota(jnp.int32, q.shape, 1)
    row = lax.broadcasted_iota(jnp.int32, (tq, tq), 0)
    col = lax.broadcasted_iota(jnp.int32, (tq, tq), 1)
    outs = []
    for first_map in (True, False):
        qm = jnp.where((lane < DA_QK) if first_map else (lane >= DA_QK), q, 0.0).astype(BF16)
        s_diag = lax.dot_general(qm, kbf[n0:n0 + tq, :], NT_DIMS, preferred_element_type=F32)
        s_diag = jnp.where(col <= row, s_diag, NEG_INF)
        m = jnp.max(s_diag, axis=1, keepdims=True)
        if qi > 0:
            s_past = lax.dot_general(qm, kbf[0:n0, :], NT_DIMS, preferred_element_type=F32)
            m = jnp.maximum(m, jnp.max(s_past, axis=1, keepdims=True))
        p_diag = jnp.exp2(s_diag - m)
        o = jnp.dot(p_diag.astype(BF16), vbf[n0:n0 + tq, :], preferred_element_type=F32)
        if qi > 0:
            p_past = jnp.exp2(s_past - m)
            o = o + jnp.dot(p_past.astype(BF16), vbf[0:n0, :], preferred_element_type=F32)
        outs.append(o[:, 0:HEAD_DIM] / o[:, HEAD_DIM:HEAD_DIM + 1])
    oa = outs[0] - lam * outs[1]
    o_ref[n0:n0 + tq, :] = (_rms(oa) * g_ref[...] * (1.0 - lam_init)).astype(o_ref.dtype)


def _attn_prompt_kernel(q_ref, k_ref, v_ref, lam_ref, g_ref, o_ref, kbf, vbf, *, tq, n_q, lam_init):
    kbf[...] = k_ref[...].astype(BF16)
    vbf[:, 0:HEAD_DIM] = v_ref[...].astype(BF16)
    vbf[:, HEAD_DIM:2 * HEAD_DIM] = jnp.ones(v_ref.shape, BF16)
    lam = _lambda(lam_ref, lam_init)
    for qi in range(n_q):
        _attn_prompt_block(qi, q_ref, kbf, vbf, lam, g_ref, o_ref, tq=tq, lam_init=lam_init)


def _attn_prompt(q_new, k_stack, v_stack, lam_p, da_g, l, lam_init):
    b, t, _ = q_new.shape
    tq = _tile(t, 256)
    kv_spec = pl.BlockSpec((None, None, t, HEAD_DIM), lambda b, h: (l, b, 0, h))
    return pl.pallas_call(
        functools.partial(_attn_prompt_kernel, tq=tq, n_q=t // tq, lam_init=lam_init),
        grid=(b, N_DA),
        in_specs=[
            pl.BlockSpec((None, t, HEAD_DIM), lambda b, h: (b, 0, h)),
            kv_spec, kv_spec,
            pl.BlockSpec((None, 4, DA_QK), lambda b, h: (l, 0, 0)),
            pl.BlockSpec((None, 1, HEAD_DIM), lambda b, h: (l, 0, h)),
        ],
        out_specs=pl.BlockSpec((None, t, HEAD_DIM), lambda b, h: (b, 0, h)),
        out_shape=jax.ShapeDtypeStruct((b, t, DA_W), BF16),
        scratch_shapes=[pltpu.VMEM((t, HEAD_DIM), BF16), pltpu.VMEM((t, 2 * HEAD_DIM), BF16)],
        compiler_params=_params(("arbitrary", "arbitrary")),
        name="attn_prompt",
    )(q_new, k_stack, v_stack, lam_p, da_g)


DEC_ROWS = N_DA * 2 * SUBLANES
PAGE_ROWS = PAGE * N_DA


def _attn_decode_kernel(pt_ref, q_ref, kn_ref, vn_ref, lam_ref, g_ref, *rest,
                        pages, n_steps, t_new, lam_init):
    k_refs = rest[:pages]
    v_refs = rest[pages:2 * pages]
    o_ref, qm, bias, m_scr, l_scr, acc_scr, kpad, vpad = rest[2 * pages:]
    step = pl.program_id(1)

    @pl.when(step == 0)
    def _():
        q = q_ref[...] * SCORE_SCALE_LOG2
        lane = lax.broadcasted_iota(jnp.int32, (SUBLANES, HEAD_DIM), 1)
        for h in range(N_DA):
            qh = q[:, h * HEAD_DIM:(h + 1) * HEAD_DIM]
            r0 = h * 2 * SUBLANES
            qm[r0:r0 + SUBLANES, :] = jnp.where(lane < DA_QK, qh, 0.0)
            qm[r0 + SUBLANES:r0 + 2 * SUBLANES, :] = jnp.where(lane >= DA_QK, qh, 0.0)
        row = lax.broadcasted_iota(jnp.int32, bias.shape, 0)
        col = lax.broadcasted_iota(jnp.int32, bias.shape, 1)
        bias[...] = jnp.where((col & (N_DA - 1)) == (row >> 4), 0.0, NEG_INF)
        m_scr[...] = jnp.full(m_scr.shape, -jnp.inf, F32)
        l_scr[...] = jnp.zeros(l_scr.shape, F32)
        acc_scr[...] = jnp.zeros(acc_scr.shape, F32)

    qb = qm[...].astype(BF16)
    head_bias = bias[...]
    group = 2 if pages % 2 == 0 else 1

    def scores_of(g):
        return [lax.dot_general(qb, k_refs[p][...].astype(BF16), NT_DIMS,
                                preferred_element_type=F32) + head_bias
                for p in range(g * group, (g + 1) * group)]

    m_run, l_run, acc = m_scr[...], l_scr[...], acc_scr[...]
    upcoming = scores_of(0)
    for g in range(pages // group):
        scores = upcoming
        if (g + 1) * group < pages:
            upcoming = scores_of(g + 1)
        m_new = m_run
        for s in scores:
            m_new = jnp.maximum(m_new, jnp.max(s, axis=1, keepdims=True))
        alpha = jnp.exp2(m_run - m_new)
        l_run = alpha * l_run
        pv = None
        for i, s in enumerate(scores):
            e = jnp.exp2(s - m_new)
            l_run = l_run + jnp.sum(e, axis=1, keepdims=True)
            d = jnp.dot(e.astype(BF16), v_refs[g * group + i][...].astype(BF16),
                        preferred_element_type=F32)
            pv = d if pv is None else pv + d
        acc = alpha * acc + pv
        m_run = m_new
    m_scr[...] = m_run
    l_scr[...] = l_run
    acc_scr[...] = acc

    @pl.when(step == n_steps - 1)
    def _():
        kpad[...] = jnp.zeros(kpad.shape, F32)
        vpad[...] = jnp.zeros(vpad.shape, F32)
        kpad[0:t_new * N_DA, :] = kn_ref[...]
        vpad[0:t_new * N_DA, :] = vn_ref[...]
        s = lax.dot_general(qb, kpad[...].astype(BF16), NT_DIMS, preferred_element_type=F32)
        row = lax.broadcasted_iota(jnp.int32, s.shape, 0)
        col = lax.broadcasted_iota(jnp.int32, s.shape, 1)
        causal = (col >> 3) <= (row & (SUBLANES - 1))
        s = jnp.where(causal, s + head_bias, NEG_INF)
        _online_softmax_step(s, vpad[...].astype(BF16), m_scr, l_scr, acc_scr)

        lam = _lambda(lam_ref, lam_init)
        o = acc_scr[...] / l_scr[...]
        for h in range(N_DA):
            r0 = h * 2 * SUBLANES
            cols = slice(h * HEAD_DIM, (h + 1) * HEAD_DIM)
            oa = o[r0:r0 + SUBLANES] - lam * o[r0 + SUBLANES:r0 + 2 * SUBLANES]
            o_ref[:, cols] = _rms(oa) * g_ref[:, cols] * (1.0 - lam_init)


def _attn_decode(q_new, k_new, v_new, cache_k, cache_v, page_table, lam_p, da_g, l, lam_init):
    b, t_new, _ = q_new.shape
    assert t_new == SUBLANES and cache_k.shape[2:] == (PAGE, N_DA, HEAD_DIM)
    n_pages = page_table.shape[1]
    pages = _tile(n_pages, 8, 1)
    n_steps = n_pages // pages
    n_pool = cache_k.shape[1]
    ck = cache_k.reshape(cache_k.shape[0], n_pool, PAGE_ROWS, HEAD_DIM)
    cv = cache_v.reshape(cache_v.shape[0], n_pool, PAGE_ROWS, HEAD_DIM)
    kn = k_new.reshape(b, t_new * N_DA, HEAD_DIM)
    vn = v_new.reshape(b, t_new * N_DA, HEAD_DIM)

    def page_spec(p):
        return pl.BlockSpec((None, None, PAGE_ROWS, HEAD_DIM),
                            lambda b, s, pt: (l, pt[b * n_pages + s * pages + p], 0, 0))

    new_spec = pl.BlockSpec((None, t_new * N_DA, HEAD_DIM), lambda b, s, pt: (b, 0, 0))
    grid_spec = pltpu.PrefetchScalarGridSpec(
        num_scalar_prefetch=1,
        grid=(b, n_steps),
        in_specs=[
            pl.BlockSpec((None, t_new, DA_W), lambda b, s, pt: (b, 0, 0)),
            new_spec, new_spec,
            pl.BlockSpec((None, 4, DA_QK), lambda b, s, pt: (l, 0, 0)),
            pl.BlockSpec((None, 1, DA_W), lambda b, s, pt: (l, 0, 0)),
        ] + [page_spec(p) for p in range(pages)] * 2,
        out_specs=pl.BlockSpec((None, t_new, DA_W), lambda b, s, pt: (b, 0, 0)),
        scratch_shapes=[
            pltpu.VMEM((DEC_ROWS, HEAD_DIM), F32),
            pltpu.VMEM((DEC_ROWS, PAGE_ROWS), F32),
            pltpu.VMEM((DEC_ROWS, 1), F32), pltpu.VMEM((DEC_ROWS, 1), F32),
            pltpu.VMEM((DEC_ROWS, HEAD_DIM), F32),
            pltpu.VMEM((PAGE_ROWS, HEAD_DIM), F32), pltpu.VMEM((PAGE_ROWS, HEAD_DIM), F32),
        ],
    )
    return pl.pallas_call(
        functools.partial(_attn_decode_kernel, pages=pages, n_steps=n_steps, t_new=t_new,
                          lam_init=lam_init),
        grid_spec=grid_spec,
        out_shape=jax.ShapeDtypeStruct((b, t_new, DA_W), F32),
        compiler_params=_params(("arbitrary", "arbitrary")),
        name="attn_decode",
    )(page_table.reshape(-1), q_new, kn, vn, lam_p, da_g, *([ck] * pages), *([cv] * pages))


def _rotary(x, cos2, sin2):
    return x * cos2 + pltpu.roll(x, DA_QK, 1) * sin2


def _ret_parts(q_ref, k_ref, v_ref, gate_ref, cos_ref, sin_ref, lg_ref, s0_ref, gn_ref,
               o_ref, sout_ref, s_scr, *, l_true):
    L = CHUNK
    lg = lg_ref[...][:, 0:1]
    ti = lax.broadcasted_iota(jnp.int32, (L, L), 0)
    si = lax.broadcasted_iota(jnp.int32, (L, L), 1)
    rel = (ti - si).astype(F32)
    dmat = jnp.where(rel >= 0, jnp.exp(lg * jnp.maximum(rel, 0.0)), 0.0)
    idx = lax.broadcasted_iota(jnp.int32, (L, 1), 0).astype(F32)
    q_dec = jnp.exp(lg * (idx + 1.0))
    k_dec = jnp.exp(lg * (l_true - 1.0 - idx))
    s_dec = jnp.exp(lg * float(l_true))
    s_scr[...] = s0_ref[...]

    def body(c):
        rows = pl.ds(c * L if isinstance(c, int) else pl.multiple_of(c * L, L), L)
        cos2, sin2 = cos_ref[rows, :], sin_ref[rows, :]
        q = _rotary(q_ref[rows, :], cos2, sin2)
        k = _rotary(k_ref[rows, :], cos2, sin2) * (HEAD_DIM ** -0.5)
        q_bf, k_bf, v_bf = q.astype(BF16), k.astype(BF16), v_ref[rows, :].astype(BF16)
        state = s_scr[...]
        scores = lax.dot_general(q_bf, k_bf, NT_DIMS, preferred_element_type=F32) * dmat
        inner = jnp.dot(scores.astype(BF16), v_bf, preferred_element_type=F32)
        cross = jnp.dot(q_bf, state.astype(BF16), preferred_element_type=F32) * q_dec
        kd_t = (k * k_dec).T.astype(BF16)
        s_scr[...] = state * s_dec + jnp.dot(kd_t, v_bf, preferred_element_type=F32)
        g = gate_ref[rows, :]
        y = _rms(inner + cross) * gn_ref[...]
        o_ref[rows, :] = (y.astype(F32) * (g * jax.nn.sigmoid(g))).astype(o_ref.dtype)

    def finalize():
        sout_ref[...] = s_scr[...]

    return body, finalize


N_RET_IN, N_RET_OUT, N_RET_SCR = 9, 2, 1


def _retention_call(proj, cos2, sin2, lg_tab, s0, gn, l):
    b, t, _ = proj.shape

    def col(c0):
        return pl.BlockSpec((None, t, HEAD_DIM), lambda b, h: (b, 0, c0 // HEAD_DIM + h))

    st_spec = pl.BlockSpec((None, None, HEAD_DIM, HEAD_DIM), lambda b, h: (b, h, 0, 0))
    in_specs = [
        col(COL_RQ), col(COL_RK), col(COL_RV), col(COL_RG),
        pl.BlockSpec((t, HEAD_DIM), lambda b, h: (0, 0)),
        pl.BlockSpec((t, HEAD_DIM), lambda b, h: (0, 0)),
        pl.BlockSpec((None, 1, HEAD_DIM), lambda b, h: (h, 0, 0)),
        st_spec,
        pl.BlockSpec((None, 1, HEAD_DIM), lambda b, h: (l, 0, h)),
    ]
    out_specs = [pl.BlockSpec((None, t, HEAD_DIM), lambda b, h: (b, 0, h)), st_spec]
    out_shape = [jax.ShapeDtypeStruct((b, t, RET_W), BF16),
                 jax.ShapeDtypeStruct((b, N_RET, HEAD_DIM, HEAD_DIM), F32)]
    scratch = [pltpu.VMEM((HEAD_DIM, HEAD_DIM), F32)]
    args = (proj, proj, proj, proj, cos2, sin2, lg_tab, s0, gn)
    return in_specs, args, out_specs, out_shape, scratch


def _row_to_col(row, eye):
    return jnp.sum(jnp.where(eye, row, 0.0), axis=1, keepdims=True)


def _cumsum_lanes(x, lane):
    n = x.shape[1]
    shift = 1
    while shift < n:
        x = x + jnp.where(lane >= shift, pltpu.roll(x, shift, 1), 0.0)
        shift *= 2
    return x


def _mlstm_parts(u_ref, v_ref, og_ref, prev_ref, ig_ref, fg_ref, ib_ref, fb_ref, cw_ref, cb_ref,
                 wq_ref, wk_ref, c0_ref, n0_ref, m0_ref, gn_ref, skip_ref,
                 o_ref, cout_ref, nout_ref, mout_ref,
                 xp_scr, uc_scr, c_scr, n_scr, m_scr, *, t, l_true):
    L = CHUNK
    halo = SUBLANES

    xp_scr[0:halo, :] = prev_ref[...]
    xp_scr[halo:halo + t, :] = u_ref[...]
    conv = cb_ref[...]
    for j in range(ML_CONV):
        start = halo - (ML_CONV - 1) + j
        conv = conv + xp_scr[start:start + t, :] * cw_ref[j:j + 1, :]
    uc_scr[...] = conv * jax.nn.sigmoid(conv)

    wq_bf = wq_ref[...].astype(BF16)
    wk_bf = wk_ref[...].astype(BF16)
    c_scr[...] = c0_ref[...]
    n_scr[...] = n0_ref[...]
    m_scr[...] = m0_ref[...]

    ti = lax.broadcasted_iota(jnp.int32, (L, L), 0)
    si = lax.broadcasted_iota(jnp.int32, (L, L), 1)
    eye = ti == si
    causal = si <= ti
    lane = lax.broadcasted_iota(jnp.int32, (1, L), 1)

    def body(c):
        rows = pl.ds(c * L if isinstance(c, int) else pl.multiple_of(c * L, L), L)
        uc = uc_scr[rows, :]
        uc_bf = uc.astype(BF16)
        q = jnp.dot(uc_bf, wq_bf, preferred_element_type=F32)
        k = jnp.dot(uc_bf, wk_bf, preferred_element_type=F32) * (HEAD_DIM ** -0.5)
        q_bf, k_bf, v_bf = q.astype(BF16), k.astype(BF16), v_ref[rows, :].astype(BF16)

        ig = ig_ref[:, rows] + ib_ref[...]
        fg = fg_ref[:, rows] + fb_ref[...]
        lf = -(jnp.maximum(-fg, 0.0) + jnp.log(1.0 + jnp.exp(-jnp.abs(fg))))
        b_row = _cumsum_lanes(lf, lane)
        b_col = _row_to_col(b_row, eye)
        m_old = m_scr[...]
        c_old = c_scr[...]
        n_old = n_scr[...]

        dlog = jnp.where(causal, b_col + (ig - b_row), -jnp.inf)
        inter = b_col + m_old
        m_row = jnp.maximum(inter, jnp.max(dlog, axis=1, keepdims=True))
        w = jnp.exp(dlog - m_row)
        s_in = jnp.exp(inter - m_row)
        a = w * lax.dot_general(q_bf, k_bf, NT_DIMS, preferred_element_type=F32)
        num = (jnp.dot(a.astype(BF16), v_bf, preferred_element_type=F32)
               + jnp.dot(q_bf, c_old.astype(BF16), preferred_element_type=F32) * s_in)
        qn = jnp.sum(q_bf.astype(F32) * n_old.astype(BF16).astype(F32), axis=1, keepdims=True)
        den = jnp.sum(a, axis=1, keepdims=True) + qn * s_in
        hh = num / jnp.maximum(jnp.abs(den), jnp.exp(-m_row))

        og = og_ref[rows, :]
        y = _rms(hh) * gn_ref[...] + skip_ref[...] * uc
        o_ref[rows, :] = (jax.nn.sigmoid(og) * y).astype(o_ref.dtype)

        b_last = jnp.sum(jnp.where(lane == l_true - 1, b_row, 0.0), axis=1, keepdims=True)
        dlast = jnp.where(lane < l_true, b_last - b_row + ig, -jnp.inf)
        m_new = jnp.maximum(b_last + m_old, jnp.max(dlast, axis=1, keepdims=True))
        ws_col = _row_to_col(jnp.exp(dlast - m_new), eye)
        s_old = jnp.exp(b_last + m_old - m_new)
        kw = k * ws_col
        c_scr[...] = c_old * s_old + jnp.dot(kw.T.astype(BF16), v_bf, preferred_element_type=F32)
        n_scr[...] = n_old * s_old + jnp.sum(kw, axis=0, keepdims=True)
        m_scr[...] = m_new

    def finalize():
        cout_ref[...] = c_scr[...]
        nout_ref[...] = n_scr[...]
        mout_ref[...] = m_scr[...]

    return body, finalize


N_ML_IN, N_ML_OUT, N_ML_SCR = 17, 4, 5


def _mlstm_call(proj, prev, gates, gbias, cw, cb, wq, wk, c0, n0, m0, gn, skip, l):
    b, t, _ = proj.shape

    def col(c0_):
        return pl.BlockSpec((None, t, HEAD_DIM), lambda b, h: (b, 0, c0_ // HEAD_DIM + h))

    def per_head(rows):
        return pl.BlockSpec((None, rows, HEAD_DIM), lambda b, h: (l, 0, h))

    c_spec = pl.BlockSpec((None, None, HEAD_DIM, HEAD_DIM), lambda b, h: (b, h, 0, 0))
    n_spec = pl.BlockSpec((None, None, 1, HEAD_DIM), lambda b, h: (b, h, 0, 0))
    m_spec = pl.BlockSpec((None, None, 1, 1), lambda b, h: (b, h, 0, 0))
    w_spec = pl.BlockSpec((None, None, HEAD_DIM, HEAD_DIM), lambda b, h: (l, h, 0, 0))
    in_specs = [
        col(COL_MU), col(COL_MV), col(COL_MO),
        pl.BlockSpec((None, SUBLANES, HEAD_DIM), lambda b, h: (b, 0, h)),
        pl.BlockSpec((None, None, 1, t), lambda b, h: (b, h, 0, 0)),
        pl.BlockSpec((None, None, 1, t), lambda b, h: (b, N_ML + h, 0, 0)),
        pl.BlockSpec((None, None, 1, 1), lambda b, h: (l, h, 0, 0)),
        pl.BlockSpec((None, None, 1, 1), lambda b, h: (l, N_ML + h, 0, 0)),
        per_head(ML_CONV), per_head(1),
        w_spec, w_spec, c_spec, n_spec, m_spec,
        per_head(1), per_head(1),
    ]
    out_specs = [pl.BlockSpec((None, t, HEAD_DIM), lambda b, h: (b, 0, h)), c_spec, n_spec, m_spec]
    out_shape = [jax.ShapeDtypeStruct((b, t, ML_W), BF16),
                 jax.ShapeDtypeStruct((b, N_ML, HEAD_DIM, HEAD_DIM), F32),
                 jax.ShapeDtypeStruct((b, N_ML, 1, HEAD_DIM), F32),
                 jax.ShapeDtypeStruct((b, N_ML, 1, 1), F32)]
    scratch = [pltpu.VMEM((SUBLANES + t, HEAD_DIM), F32), pltpu.VMEM((t, HEAD_DIM), F32),
               pltpu.VMEM((HEAD_DIM, HEAD_DIM), F32), pltpu.VMEM((1, HEAD_DIM), F32),
               pltpu.VMEM((1, 1), F32)]
    args = (proj, proj, proj, prev, gates[:, :, None, :], gates[:, :, None, :], gbias, gbias,
            cw, cb, wq, wk, c0, n0, m0, gn, skip)
    return in_specs, args, out_specs, out_shape, scratch


def _seqmix_kernel(*refs, t, n_chunks, l_true):
    assert N_RET == N_ML
    n_in, n_out = N_RET_IN + N_ML_IN, N_RET_OUT + N_ML_OUT
    ins, outs, scr = refs[:n_in], refs[n_in:n_in + n_out], refs[n_in + n_out:]
    ret_body, ret_fin = _ret_parts(*ins[:N_RET_IN], *outs[:N_RET_OUT], *scr[:N_RET_SCR],
                                   l_true=l_true)
    ml_body, ml_fin = _mlstm_parts(*ins[N_RET_IN:], *outs[N_RET_OUT:], *scr[N_RET_SCR:],
                                   t=t, l_true=l_true)

    if n_chunks <= CHUNK_UNROLL:
        for c in range(n_chunks):
            ret_body(c)
            ml_body(c)
    else:
        def body(c, carry):
            ret_body(c)
            ml_body(c)
            return carry

        lax.fori_loop(0, n_chunks, body, 0, unroll=CHUNK_UNROLL)
    ret_fin()
    ml_fin()


def _seqmix(ret_call, ml_call, b, t, l_true):
    in_specs, args, out_specs, out_shape, scratch = (r + m for r, m in zip(ret_call, ml_call))
    return pl.pallas_call(
        functools.partial(_seqmix_kernel, t=t, n_chunks=t // CHUNK, l_true=l_true),
        grid=(b, N_RET),
        in_specs=in_specs,
        out_specs=out_specs,
        out_shape=out_shape,
        scratch_shapes=scratch,
        compiler_params=_params(("arbitrary", "arbitrary")),
        name="seqmix",
    )(*args)


def _conv_gate_kernel(ug_ref, uv_ref, hg_ref, hv_ref, pg_ref, pv_ref, wg_ref, wv_ref,
                      bg_ref, bv_ref, o_ref, sg, sv, *, tt):
    first = pl.program_id(1) == 0
    halo = SUBLANES

    def conv(u_ref, h_ref, p_ref, w_ref, b_ref, scr):
        @pl.when(first)
        def _():
            scr[0:halo, :] = p_ref[...]

        @pl.when(jnp.logical_not(first))
        def _():
            scr[0:halo, :] = h_ref[...]

        scr[halo:halo + tt, :] = u_ref[...]
        y = b_ref[...]
        for j in range(FFN_CONV):
            start = halo - (FFN_CONV - 1) + j
            y = y + scr[start:start + tt, :] * w_ref[j:j + 1, :]
        return y

    gate = conv(ug_ref, hg_ref, pg_ref, wg_ref, bg_ref, sg)
    val = conv(uv_ref, hv_ref, pv_ref, wv_ref, bv_ref, sv)
    o_ref[...] = (gate * jax.nn.sigmoid(gate) * val).astype(o_ref.dtype)


def _conv_gate(up, prev, cw, cb, l):
    b, t, f2 = up.shape
    f = f2 // 2
    tt = _tile(t, 512, SUBLANES)
    tn = _tile(f, max(512, 512 * 512 // tt))
    nf = f // tn
    rb = tt // SUBLANES

    def main(off):
        return pl.BlockSpec((None, tt, tn), lambda b, i, n: (b, i, n + off))

    def halo(off):
        return pl.BlockSpec((None, SUBLANES, tn),
                            lambda b, i, n: (b, jnp.maximum(i * rb - 1, 0), n + off))

    def prev_spec(off):
        return pl.BlockSpec((None, SUBLANES, tn), lambda b, i, n: (b, 0, n + off))

    def w_spec(rows, off):
        return pl.BlockSpec((None, rows, tn), lambda b, i, n: (l, 0, n + off))

    return pl.pallas_call(
        functools.partial(_conv_gate_kernel, tt=tt),
        grid=(b, t // tt, nf),
        in_specs=[main(0), main(nf), halo(0), halo(nf), prev_spec(0), prev_spec(nf),
                  w_spec(FFN_CONV, 0), w_spec(FFN_CONV, nf), w_spec(1, 0), w_spec(1, nf)],
        out_specs=pl.BlockSpec((None, tt, tn), lambda b, i, n: (b, i, n)),
        out_shape=jax.ShapeDtypeStruct((b, t, f), BF16),
        scratch_shapes=[pltpu.VMEM((SUBLANES + tt, tn), F32), pltpu.VMEM((SUBLANES + tt, tn), F32)],
        compiler_params=_params(("arbitrary", "arbitrary", "arbitrary")),
        name="conv_gate",
    )(up, up, up, up, prev, prev, cw, cw, cb, cb)


def _ffn_up_kernel(x_ref, wg_ref, wv_ref, pg_ref, pv_ref, cwg_ref, cwv_ref, cbg_ref, cbv_ref,
                   xs_ref, o_ref, sg_ref, sv_ref, ups_g_ref, ups_v_ref,
                   wbf_g, wbf_v, scr_g, scr_v, *, tm, n_sub):
    t = pl.program_id(2)
    halo = SUBLANES

    @pl.when(jnp.logical_and(pl.program_id(1) == 0, t == 0))
    def _():
        wbf_g[...] = wg_ref[...].astype(BF16)
        wbf_v[...] = wv_ref[...].astype(BF16)
        ups_g_ref[...] = jnp.dot(xs_ref[...], wbf_g[...], preferred_element_type=F32)
        ups_v_ref[...] = jnp.dot(xs_ref[...], wbf_v[...], preferred_element_type=F32)

    @pl.when(t == 0)
    def _():
        scr_g[0:halo, :] = pg_ref[...]
        scr_v[0:halo, :] = pv_ref[...]

    sub = tm // n_sub

    def project(c):
        rows = slice(c * sub, (c + 1) * sub)
        x = x_ref[rows, :]
        dst = slice(halo + c * sub, halo + (c + 1) * sub)
        scr_g[dst, :] = jnp.dot(x, wbf_g[...], preferred_element_type=F32)
        scr_v[dst, :] = jnp.dot(x, wbf_v[...], preferred_element_type=F32)

    def conv(c, scr, cw_ref, cb_ref):
        y = cb_ref[...]
        for j in range(FFN_CONV):
            start = halo - (FFN_CONV - 1) + j + c * sub
            y = y + scr[start:start + sub, :] * cw_ref[j:j + 1, :]
        return y

    project(0)
    for c in range(n_sub):
        if c + 1 < n_sub:
            project(c + 1)
        gate = conv(c, scr_g, cwg_ref, cbg_ref)
        val = conv(c, scr_v, cwv_ref, cbv_ref)
        o_ref[c * sub:(c + 1) * sub, :] = (gate * jax.nn.sigmoid(gate) * val).astype(o_ref.dtype)

    for scr, s_ref in ((scr_g, sg_ref), (scr_v, sv_ref)):
        tail = scr[tm:tm + halo, :]
        s_ref[...] = tail
        scr[0:halo, :] = tail


def _ffn_up_fused(x, w_up, prev, cw, cb, l, x_side):
    b, t, d = x.shape
    ts = x_side.shape[1]
    f = w_up.shape[2] // 2
    tm = _tile(t, 1024, SUBLANES)
    tn = _tile(f, 512)
    nf = f // tn

    def cols(rows, off, arr_l):
        return pl.BlockSpec((None, rows, tn), lambda n, b, i: (arr_l(b), 0, n + off))

    layer = lambda b: l
    batch = lambda b: b
    tail_spec = pl.BlockSpec((None, SUBLANES, tn), lambda n, b, i: (b, 0, n))
    side_spec = pl.BlockSpec((None, ts, tn), lambda n, b, i: (0, 0, n))
    return pl.pallas_call(
        functools.partial(_ffn_up_kernel, tm=tm, n_sub=FFN_SUB if tm % (FFN_SUB * 16) == 0 else 1),
        grid=(nf, b, t // tm),
        in_specs=[
            pl.BlockSpec((None, tm, d), lambda n, b, i: (b, i, 0)),
            cols(d, 0, layer), cols(d, nf, layer),
            cols(SUBLANES, 0, batch), cols(SUBLANES, nf, batch),
            cols(FFN_CONV, 0, layer), cols(FFN_CONV, nf, layer),
            cols(1, 0, layer), cols(1, nf, layer),
            pl.BlockSpec((None, ts, d), lambda n, b, i: (0, 0, 0)),
        ],
        out_specs=[pl.BlockSpec((None, tm, tn), lambda n, b, i: (b, i, n)), tail_spec, tail_spec,
                   side_spec, side_spec],
        out_shape=[jax.ShapeDtypeStruct((b, t, f), BF16),
                   jax.ShapeDtypeStruct((b, SUBLANES, f), F32),
                   jax.ShapeDtypeStruct((b, SUBLANES, f), F32),
                   jax.ShapeDtypeStruct((1, ts, f), F32),
                   jax.ShapeDtypeStruct((1, ts, f), F32)],
        scratch_shapes=[pltpu.VMEM((d, tn), BF16), pltpu.VMEM((d, tn), BF16),
                        pltpu.VMEM((SUBLANES + tm, tn), F32), pltpu.VMEM((SUBLANES + tm, tn), F32)],
        compiler_params=_params(("arbitrary", "arbitrary", "arbitrary")),
        name="ffn_up_conv",
    )(x, w_up, w_up, prev, prev, cw, cw, cb, cb, x_side)


def _rope_tables(pos, rows):
    half = HEAD_DIM // 2
    inv = ROPE_BASE ** (-jnp.arange(half, dtype=F32) / half)
    ang = pos.astype(F32)[:, None] * inv[None, :]
    cos, sin = jnp.cos(ang), jnp.sin(ang)
    cos2 = jnp.concatenate([cos, cos], axis=-1)
    sin2 = jnp.concatenate([-sin, sin], axis=-1)
    pad = rows - pos.shape[0]
    if pad:
        cos2 = jnp.pad(cos2, ((0, pad), (0, 0)))
        sin2 = jnp.pad(sin2, ((0, pad), (0, 0)))
    return cos2, sin2


def _prev_rows(buf):
    return jnp.pad(buf, ((0, 0), (SUBLANES - buf.shape[1], 0), (0, 0)))


def _mixers(l, b, t, ret_in, ml_in, gates, st, w, rope, lg_tab):
    t_pad = -(-t // CHUNK) * CHUNK
    l_true = CHUNK if t % CHUNK == 0 else t
    if t_pad != t:
        ret_in = jnp.pad(ret_in, ((0, 0), (0, t_pad - t), (0, 0)))
        ml_in = jnp.pad(ml_in, ((0, 0), (0, t_pad - t), (0, 0)))
        gates = jnp.pad(gates, ((0, 0), (0, 0), (0, t_pad - t)))
    ret_call = _retention_call(ret_in, rope[0], rope[1], lg_tab, st["ret_s"], w["ret_norm_g"], l)
    ml_call = _mlstm_call(
        ml_in, _prev_rows(st["ml_conv"]), gates, w["gbias"], w["ml_conv_w"], w["ml_conv_b"],
        w["ml_wq"], w["ml_wk"], st["ml_c"], st["ml_n"][:, :, None, :], st["ml_m"][:, :, None, None],
        w["ml_norm_g"], w["ml_skip"], l)
    ro, ret_s, mo, ml_c, ml_n, ml_m = _seqmix(ret_call, ml_call, b, t_pad, l_true)
    new_st = dict(ret_s=ret_s, ml_c=ml_c, ml_n=ml_n[:, :, 0, :], ml_m=ml_m[:, :, 0, 0])
    return ro[:, :t], mo[:, :t], new_st


def _layer(l, lam_init, xp, xs, mods_p, mods_s, st_p, st_s, w, cache, rope_p, rope_s, lg_tab,
           kv_stacks):
    bp, tp, d = xp.shape
    bs = st_s["ret_s"].shape[0]
    ts = xs.shape[1] // bs
    sh1p, sc1p, g1p, sh2p, sc2p, g2p = mods_p
    sh1s, sc1s, g1s, sh2s, sc2s, g2s = mods_s
    depth = w["w_in_t"].shape[0]
    f2 = w["ffn_w_up"].shape[2]

    hp = _norm(xp, w["g_mix"], l, sc1p, sh1p)
    hs = _norm(xs, w["g_mix"], l, sc1s, sh1s)

    def in_proj(col0, width, tn, name, stack=None):
        assert col0 % tn == 0 and width % tn == 0
        return _matmul([hp], w["w_in_t"], l, width // tn, tn, 1024, name=name, w_transposed=True,
                       w_tile=lambda n: n + col0 // tn, stack=stack, side=([hs], None, None))

    q_p, q_s = in_proj(W_COL_Q, DA_W, 1024, "q_proj")
    k_stack, k_s = in_proj(W_COL_K, DA_W, 1024, "k_proj", stack=(depth, kv_stacks[0]))
    v_stack, v_s = in_proj(W_COL_V, DA_W, 1024, "v_proj", stack=(depth, kv_stacks[1]))
    ret_p, ret_s = in_proj(W_COL_RET, 4 * RET_W, 1024, "ret_proj")
    ml_p, ml_s = in_proj(W_COL_ML, 3 * ML_W, 512, "ml_proj")
    gates_p = _gates(hp, w["w_in_t"], l)
    gates_s = _gates(hs, w["w_in_t"], l).reshape(GATE_ROWS, bs, ts).transpose(1, 0, 2)
    ml_s = ml_s.reshape(bs, ts, 3 * ML_W)

    oa_p = _attn_prompt(q_p, k_stack, v_stack, w["lam_p"], w["da_norm_g"], l, lam_init)
    oa_s = _attn_decode(q_s.reshape(bs, ts, DA_W), k_s, v_s, *cache, w["lam_p"], w["da_norm_g"],
                        l, lam_init)
    ro_p, mo_p, new_p = _mixers(l, bp, tp, ret_p, ml_p, gates_p, st_p, w, rope_p, lg_tab)
    ro_s, mo_s, new_s = _mixers(l, bs, ts, ret_s.reshape(bs, ts, 4 * RET_W), ml_s, gates_s,
                                st_s, w, rope_s, lg_tab)
    mix_s = [a.astype(BF16).reshape(1, bs * ts, a.shape[-1]) for a in (oa_s, ro_s, mo_s)]
    xp, xs = _matmul([oa_p, ro_p, mo_p], w["w_out"], l, d // _tile(d, 1024), _tile(d, 1024), 1024,
                     res=xp, gate=g1p, side=(mix_s, xs, g1s), name="out_proj")

    h2p = _norm(xp, w["g_ffn"], l, sc2p, sh2p)
    h2s = _norm(xs, w["g_ffn"], l, sc2s, sh2s)
    act_p, tail_g, tail_v, up_g_s, up_v_s = _ffn_up_fused(
        h2p, w["ffn_w_up"], _prev_rows(st_p["ffn_conv"]), w["ffn_conv_w"], w["ffn_conv_b"], l, h2s)
    up_s = jnp.concatenate([up_g_s, up_v_s], axis=-1).reshape(bs, ts, f2)
    act_s = _conv_gate(up_s, _prev_rows(st_s["ffn_conv"]), w["ffn_conv_w"], w["ffn_conv_b"], l)
    xp, xs = _matmul([act_p], w["ffn_w_down"], l, d // _tile(d, 512), _tile(d, 512), 512,
                     res=xp, gate=g2p, side=([act_s.reshape(1, bs * ts, f2 // 2)], xs, g2s),
                     name="ffn_down")

    new_p.update(
        ml_conv=ml_p[:, tp - (ML_CONV - 1):, COL_MU:COL_MU + ML_W],
        ffn_conv=jnp.concatenate([tail_g, tail_v], axis=-1)[:, SUBLANES - (FFN_CONV - 1):, :])
    new_s.update(ml_conv=ml_s[:, ts - (ML_CONV - 1):, COL_MU:COL_MU + ML_W],
                 ffn_conv=up_s[:, ts - (FFN_CONV - 1):, :])
    return xp, xs, (k_stack, v_stack), (k_s, v_s), new_p, new_s


def kernel(x_prompt, x_sample, c_prompt, c_sample, cache_k, cache_v, page_table, state_ret, state_ml_c, state_ml_n, state_ml_m, state_ml_conv, state_ffn_conv, w_ada, b_ada, g_mix, w_in, lam_q1, lam_k1, lam_q2, lam_k2, da_norm_g, ret_norm_g, ml_conv_w, ml_conv_b, ml_wq, ml_wk, ml_b_i, ml_b_f, ml_norm_g, ml_skip, w_out, g_ffn, ffn_w_up, ffn_conv_w, ffn_conv_b, ffn_w_down, g_final):
    bp, tp, d = x_prompt.shape
    bs, ts, _ = x_sample.shape
    depth = w_in.shape[0]
    f2 = ffn_w_up.shape[2]
    past = page_table.shape[1] * PAGE

    def per_layer_rows(a):
        return a.reshape(depth, 1, a.shape[-1])

    zeros8 = jnp.zeros((depth, GATE_ROWS - N_GATES), F32)
    w = dict(
        g_mix=g_mix, g_ffn=g_ffn, w_out=w_out, ffn_w_up=ffn_w_up, ffn_w_down=ffn_w_down,
        w_in_t=jnp.swapaxes(w_in, 1, 2),
        gbias=jnp.concatenate([ml_b_i, ml_b_f, zeros8], axis=1)[:, :, None, None],
        lam_p=jnp.stack([lam_q1, lam_k1, lam_q2, lam_k2], axis=1),
        da_norm_g=per_layer_rows(da_norm_g), ret_norm_g=per_layer_rows(ret_norm_g),
        ml_conv_w=ml_conv_w, ml_conv_b=per_layer_rows(ml_conv_b), ml_wq=ml_wq, ml_wk=ml_wk,
        ml_norm_g=per_layer_rows(ml_norm_g), ml_skip=per_layer_rows(ml_skip),
        ffn_conv_w=ffn_conv_w, ffn_conv_b=per_layer_rows(ffn_conv_b))

    n_c = bp + bs
    c_rows = -(-n_c // SUBLANES) * SUBLANES
    c_all = jnp.pad(jnp.concatenate([c_prompt, c_sample], axis=0), ((0, c_rows - n_c), (0, 0)))
    mod = _ada(c_all, w_ada, b_ada)

    lg_np = np.log(1.0 - 2.0 ** (-5.0 - np.arange(N_RET, dtype=np.float64)))
    lg_tab = jnp.asarray(np.broadcast_to(lg_np[:, None, None], (N_RET, 1, HEAD_DIM)), F32)
    tp_pad = -(-tp // CHUNK) * CHUNK
    ts_pad = -(-ts // CHUNK) * CHUNK
    rope_p = _rope_tables(jnp.arange(tp), tp_pad)
    rope_s = _rope_tables(past + jnp.arange(ts), ts_pad)

    zero_st = dict(
        ret_s=jnp.zeros((bp, N_RET, HEAD_DIM, HEAD_DIM), F32),
        ml_c=jnp.zeros((bp, N_ML, HEAD_DIM, HEAD_DIM), F32),
        ml_n=jnp.zeros((bp, N_ML, HEAD_DIM), F32),
        ml_m=jnp.zeros((bp, N_ML), F32),
        ml_conv=jnp.zeros((bp, ML_CONV - 1, ML_W), F32),
        ffn_conv=jnp.zeros((bp, FFN_CONV - 1, f2), F32))
    names = ("ret_s", "ml_c", "ml_n", "ml_m", "ml_conv", "ffn_conv")
    out_p = {n: [] for n in names}
    out_s = {n: [] for n in names}
    kv_p = (None, None)
    k_s, v_s = [], []
    yp = x_prompt
    ys = x_sample.reshape(1, bs * ts, d)
    for l in range(depth):
        lam_init = 0.8 - 0.6 * math.exp(-0.3 * l)
        parts = jnp.split(mod[l], 6, axis=-1)
        mods_p = [m[:bp, None, :] for m in parts]
        mods_s = [jnp.repeat(m[bp:n_c], ts, axis=0)[None] for m in parts]
        st_in = dict(ret_s=state_ret[l], ml_c=state_ml_c[l], ml_n=state_ml_n[l], ml_m=state_ml_m[l],
                     ml_conv=state_ml_conv[l], ffn_conv=state_ffn_conv[l])
        yp, ys, kv_p, kv_s, st_p, st_s = _layer(
            l, lam_init, yp, ys, mods_p, mods_s, zero_st, st_in, w,
            (cache_k, cache_v, page_table), rope_p, rope_s, lg_tab, kv_p)
        k_s.append(kv_s[0])
        v_s.append(kv_s[1])
        for n in names:
            out_p[n].append(st_p[n])
            out_s[n].append(st_s[n])

    y_prompt = _norm(yp, g_final, None, out_dtype=F32)
    y_sample = _norm(ys, g_final, None, out_dtype=F32).reshape(bs, ts, d)
    kv_shape_p = (depth, bp, tp, N_DA, HEAD_DIM)
    kv_shape_s = (depth, bs, ts, N_DA, HEAD_DIM)
    return (y_prompt, y_sample,
            kv_p[0].reshape(kv_shape_p), kv_p[1].reshape(kv_shape_p),
            jnp.stack(k_s).reshape(kv_shape_s), jnp.stack(v_s).reshape(kv_shape_s),
            jnp.stack(out_p["ret_s"]), jnp.stack(out_s["ret_s"]),
            jnp.stack(out_p["ml_c"]), jnp.stack(out_s["ml_c"]),
            jnp.stack(out_p["ml_n"]), jnp.stack(out_s["ml_n"]),
            jnp.stack(out_p["ml_m"]), jnp.stack(out_s["ml_m"]),
            jnp.stack(out_p["ml_conv"]), jnp.stack(out_s["ml_conv"]),
            jnp.stack(out_p["ffn_conv"]), jnp.stack(out_s["ffn_conv"]))
```

```python
import functools
import math

import numpy as np
import jax
import jax.numpy as jnp
from jax import lax
from jax.experimental import pallas as pl
from jax.experimental.pallas import tpu as pltpu

F32 = jnp.float32
BF16 = jnp.bfloat16

HEAD_DIM = 128
DA_QK = HEAD_DIM // 2
N_DA = 8
N_RET = 4
N_ML = 4
DA_W = N_DA * HEAD_DIM
RET_W = N_RET * HEAD_DIM
ML_W = N_ML * HEAD_DIM
ML_CONV = 4
FFN_CONV = 3
CHUNK = 128
PAGE = 128
RMS_EPS = 1e-6
ROPE_BASE = 10000.0
NEG_INF = -1e30

W_COL_Q, W_COL_K, W_COL_V = 0, DA_W, 2 * DA_W
W_COL_RET = 3 * DA_W
W_COL_ML = W_COL_RET + 4 * RET_W
W_COL_GATES = W_COL_ML + 3 * ML_W
COL_RQ, COL_RK, COL_RV, COL_RG = 0, RET_W, 2 * RET_W, 3 * RET_W
COL_MU, COL_MV, COL_MO = 0, ML_W, 2 * ML_W
N_GATES = 2 * N_ML
GATE_ROWS = 16

SUBLANES = 8
LANES = 128
VMEM_LIMIT = 56 * 1024 * 1024
FFN_SUB = 4
CHUNK_UNROLL = 16

NT_DIMS = (((1,), (1,)), ((), ()))
TN_DIMS = (((0,), (0,)), ((), ()))


def _tile(n, pref, mult=LANES):
    if n <= pref:
        return n
    t = (pref // mult) * mult
    while t >= mult:
        if n % t == 0:
            return t
        t -= mult
    return n


def _params(sem):
    return pltpu.CompilerParams(dimension_semantics=sem, vmem_limit_bytes=VMEM_LIMIT)


def _rms(x):
    return x * lax.rsqrt(jnp.mean(x * x, axis=-1, keepdims=True) + RMS_EPS)


def _ada_kernel(c_ref, w_ref, b_ref, o_ref):
    c = c_ref[...]
    a = (c * jax.nn.sigmoid(c)).astype(BF16)
    o_ref[...] = jnp.dot(a, w_ref[...].astype(BF16), preferred_element_type=F32) + b_ref[...]


def _ada(c_all, w_ada, b_ada):
    depth, d, n = w_ada.shape
    rows = c_all.shape[0]
    tn = _tile(n, 1024)
    return pl.pallas_call(
        _ada_kernel,
        grid=(depth, n // tn),
        in_specs=[
            pl.BlockSpec((rows, d), lambda l, j: (0, 0)),
            pl.BlockSpec((None, d, tn), lambda l, j: (l, 0, j)),
            pl.BlockSpec((None, 1, tn), lambda l, j: (l, 0, j)),
        ],
        out_specs=pl.BlockSpec((None, rows, tn), lambda l, j: (l, 0, j)),
        out_shape=jax.ShapeDtypeStruct((depth, rows, n), F32),
        compiler_params=_params(("arbitrary", "arbitrary")),
        name="ada",
    )(c_all, w_ada, b_ada.reshape(depth, 1, n))


def _norm_kernel(*refs, modulated):
    if modulated:
        x_ref, g_ref, sc_ref, sh_ref, o_ref = refs
    else:
        x_ref, g_ref, o_ref = refs
    y = _rms(x_ref[...]) * g_ref[...]
    if modulated:
        y = y * (1.0 + sc_ref[...]) + sh_ref[...]
    o_ref[...] = y.astype(o_ref.dtype)


def _mod_spec(mod, tm, tn, with_n):
    per_row = mod.shape[1] != 1
    rows = tm if per_row else 1
    if with_n:
        return pl.BlockSpec((None, rows, tn), lambda n, b, t: (b, t if per_row else 0, n))
    return pl.BlockSpec((None, rows, tn), lambda b, t: (b, t if per_row else 0, 0))


def _norm(x, g, l, sc=None, sh=None, out_dtype=BF16):
    bm, tmx, d = x.shape
    tm = _tile(tmx, 512, SUBLANES)
    modulated = sc is not None
    g_spec = (pl.BlockSpec((None, 1, d), lambda b, t: (l, 0, 0)) if l is not None
              else pl.BlockSpec((1, d), lambda b, t: (0, 0)))
    in_specs = [pl.BlockSpec((None, tm, d), lambda b, t: (b, t, 0)), g_spec]
    args = [x, g.reshape(g.shape[0], 1, d) if l is not None else g.reshape(1, d)]
    if modulated:
        in_specs += [_mod_spec(sc, tm, d, False), _mod_spec(sh, tm, d, False)]
        args += [sc, sh]
    return pl.pallas_call(
        functools.partial(_norm_kernel, modulated=modulated),
        grid=(bm, tmx // tm),
        in_specs=in_specs,
        out_specs=pl.BlockSpec((None, tm, d), lambda b, t: (b, t, 0)),
        out_shape=jax.ShapeDtypeStruct((bm, tmx, d), out_dtype),
        compiler_params=_params(("arbitrary", "arbitrary")),
        name="norm",
    )(*args)


def _mm_kernel(*refs, row_offs, row_sizes, has_res, has_side, w_transposed):
    n_in = len(row_offs)
    it = iter(refs)
    main = [[next(it) for _ in range(n_in)]]
    w_ref = next(it)
    main += [next(it), next(it)] if has_res else [None, None]
    if has_side:
        side = [[next(it) for _ in range(n_in)]]
        side += [next(it), next(it)] if has_res else [None, None]
    wbf_ref = refs[-1]
    o_ref, side_o_ref = (refs[-3], refs[-2]) if has_side else (refs[-2], None)

    def project(x_refs, res_ref, gate_ref, out_ref):
        acc = None
        for x_ref, off, size in zip(x_refs, row_offs, row_sizes):
            if w_transposed:
                part = lax.dot_general(x_ref[...], wbf_ref[:, off:off + size], NT_DIMS,
                                       preferred_element_type=F32)
            else:
                part = jnp.dot(x_ref[...], wbf_ref[off:off + size, :], preferred_element_type=F32)
            acc = part if acc is None else acc + part
        if res_ref is not None:
            acc = res_ref[...] + gate_ref[...] * acc
        out_ref[...] = acc.astype(out_ref.dtype)

    @pl.when(jnp.logical_and(pl.program_id(1) == 0, pl.program_id(2) == 0))
    def _():
        wbf_ref[...] = w_ref[...].astype(BF16)
        if has_side:
            project(*side, side_o_ref)

    project(*main, o_ref)


def _matmul(xs, w, l, n_tiles, tn, tm_pref, res=None, gate=None, w_tile=lambda n: n,
            w_transposed=False, stack=None, side=None, name="mm"):
    bm, tmx = xs[0].shape[:2]
    k = w.shape[2] if w_transposed else w.shape[1]
    sizes = [x.shape[2] for x in xs]
    offs = [sum(sizes[:i]) for i in range(len(xs))]
    assert sum(sizes) == k
    tm = _tile(tmx, tm_pref, SUBLANES)
    in_specs = [pl.BlockSpec((None, tm, s), lambda n, b, t: (b, t, 0)) for s in sizes]
    if w_transposed:
        in_specs.append(pl.BlockSpec((None, tn, k), lambda n, b, t: (l, w_tile(n), 0)))
    else:
        in_specs.append(pl.BlockSpec((None, k, tn), lambda n, b, t: (l, 0, w_tile(n))))
    args = list(xs) + [w]
    if res is not None:
        in_specs.append(pl.BlockSpec((None, tm, tn), lambda n, b, t: (b, t, n)))
        in_specs.append(_mod_spec(gate, tm, tn, True))
        args += [res, gate]
    out_specs = [pl.BlockSpec((None, tm, tn), lambda n, b, t: (b, t, n))]
    out_shape = [jax.ShapeDtypeStruct((bm, tmx, n_tiles * tn), F32)]
    if side is not None:
        side_xs, side_res, side_gate = side
        ts = side_xs[0].shape[1]
        side_cols = pl.BlockSpec((None, ts, tn), lambda n, b, t: (0, 0, n))
        in_specs += [pl.BlockSpec((None, ts, s), lambda n, b, t: (0, 0, 0)) for s in sizes]
        args += list(side_xs)
        if res is not None:
            in_specs += [side_cols, side_cols]
            args += [side_res, side_gate]
        out_specs.append(side_cols)
        out_shape.append(jax.ShapeDtypeStruct((1, ts, n_tiles * tn), F32))
    aliases = {}
    if stack is not None:
        depth, buf = stack
        out_specs[0] = pl.BlockSpec((None, None, tm, tn), lambda n, b, t: (l, b, t, n))
        out_shape[0] = jax.ShapeDtypeStruct((depth, bm, tmx, n_tiles * tn), F32)
        if buf is not None:
            in_specs.append(pl.BlockSpec(memory_space=pl.ANY))
            args.append(buf)
            aliases = {len(args) - 1: 0}
    outs = pl.pallas_call(
        functools.partial(_mm_kernel, row_offs=tuple(offs), row_sizes=tuple(sizes),
                          has_res=res is not None, has_side=side is not None,
                          w_transposed=w_transposed),
        grid=(n_tiles, bm, tmx // tm),
        in_specs=in_specs,
        out_specs=out_specs,
        out_shape=out_shape,
        input_output_aliases=aliases,
        scratch_shapes=[pltpu.VMEM((tn, k) if w_transposed else (k, tn), BF16)],
        compiler_params=_params(("arbitrary", "arbitrary", "arbitrary")),
        name=name,
    )(*args)
    return outs if side is not None else outs[0]


def _gates_kernel(h_ref, wg_ref, o_ref):
    row = lax.broadcasted_iota(jnp.int32, wg_ref.shape, 0)
    wg = jnp.where(row < N_GATES, wg_ref[...], 0.0).astype(BF16)
    o_ref[...] = lax.dot_general(wg, h_ref[...], NT_DIMS, preferred_element_type=F32)


def _gates(h, w_in_t, l):
    bm, tmx, d = h.shape
    tm = _tile(tmx, 1024)
    assert W_COL_GATES % GATE_ROWS == 0 and w_in_t.shape[1] == W_COL_GATES + N_GATES
    return pl.pallas_call(
        _gates_kernel,
        grid=(bm, tmx // tm),
        in_specs=[pl.BlockSpec((None, tm, d), lambda b, t: (b, t, 0)),
                  pl.BlockSpec((None, GATE_ROWS, d), lambda b, t: (l, W_COL_GATES // GATE_ROWS, 0))],
        out_specs=pl.BlockSpec((None, GATE_ROWS, tm), lambda b, t: (b, 0, t)),
        out_shape=jax.ShapeDtypeStruct((bm, GATE_ROWS, tmx), F32),
        compiler_params=_params(("arbitrary", "arbitrary")),
        name="gates",
    )(h, w_in_t)


def _lambda(lam_ref, lam_init):
    lp = lam_ref[...]
    a = jnp.exp(jnp.sum(lp[0:1] * lp[1:2], axis=1, keepdims=True))
    b = jnp.exp(jnp.sum(lp[2:3] * lp[3:4], axis=1, keepdims=True))
    return a - b + lam_init


SCORE_SCALE_LOG2 = DA_QK ** -0.5 * math.log2(math.e)


def _online_softmax_step(s, v_bf, m_scr, l_scr, acc_scr):
    m_prev = m_scr[...]
    m_new = jnp.maximum(m_prev, jnp.max(s, axis=1, keepdims=True))
    alpha = jnp.exp2(m_prev - m_new)
    p = jnp.exp2(s - m_new)
    l_scr[...] = alpha * l_scr[...] + jnp.sum(p, axis=1, keepdims=True)
    acc_scr[...] = alpha * acc_scr[...] + jnp.dot(p.astype(BF16), v_bf, preferred_element_type=F32)
    m_scr[...] = m_new


def _attn_prompt_block(qi, q_ref, kbf, vbf, lam, g_ref, o_ref, *, tq, lam_init):
    n0 = qi * tq
    q = q_ref[n0:n0 + tq, :] * SCORE_SCALE_LOG2
    lane = lax.broadcasted_iota(jnp.int32, q.shape, 1)
    row = lax.broadcasted_iota(jnp.int32, (tq, tq), 0)
    col = lax.broadcasted_iota(jnp.int32, (tq, tq), 1)
    outs = []
    for first_map in (True, False):
        qm = jnp.where((lane < DA_QK) if first_map else (lane >= DA_QK), q, 0.0).astype(BF16)
        s_diag = lax.dot_general(qm, kbf[n0:n0 + tq, :], NT_DIMS, preferred_element_type=F32)
        s_diag = jnp.where(col <= row, s_diag, NEG_INF)
        m = jnp.max(s_diag, axis=1, keepdims=True)
        if qi > 0:
            s_past = lax.dot_general(qm, kbf[0:n0, :], NT_DIMS, preferred_element_type=F32)
            m = jnp.maximum(m, jnp.max(s_past, axis=1, keepdims=True))
        p_diag = jnp.exp2(s_diag - m)
        o = jnp.dot(p_diag.astype(BF16), vbf[n0:n0 + tq, :], preferred_element_type=F32)
        if qi > 0:
            p_past = jnp.exp2(s_past - m)
            o = o + jnp.dot(p_past.astype(BF16), vbf[0:n0, :], preferred_element_type=F32)
        outs.append(o[:, 0:HEAD_DIM] / o[:, HEAD_DIM:HEAD_DIM + 1])
    oa = outs[0] - lam * outs[1]
    o_ref[n0:n0 + tq, :] = (_rms(oa) * g_ref[...] * (1.0 - lam_init)).astype(o_ref.dtype)


def _attn_prompt_kernel(q_ref, k_ref, v_ref, lam_ref, g_ref, o_ref, kbf, vbf, *, tq, n_q, lam_init):
    kbf[...] = k_ref[...].astype(BF16)
    vbf[:, 0:HEAD_DIM] = v_ref[...].astype(BF16)
    vbf[:, HEAD_DIM:2 * HEAD_DIM] = jnp.ones(v_ref.shape, BF16)
    lam = _lambda(lam_ref, lam_init)
    for qi in range(n_q):
        _attn_prompt_block(qi, q_ref, kbf, vbf, lam, g_ref, o_ref, tq=tq, lam_init=lam_init)


def _attn_prompt(q_new, k_stack, v_stack, lam_p, da_g, l, lam_init):
    b, t, _ = q_new.shape
    tq = _tile(t, 256)
    kv_spec = pl.BlockSpec((None, None, t, HEAD_DIM), lambda b, h: (l, b, 0, h))
    return pl.pallas_call(
        functools.partial(_attn_prompt_kernel, tq=tq, n_q=t // tq, lam_init=lam_init),
        grid=(b, N_DA),
        in_specs=[
            pl.BlockSpec((None, t, HEAD_DIM), lambda b, h: (b, 0, h)),
            kv_spec, kv_spec,
            pl.BlockSpec((None, 4, DA_QK), lambda b, h: (l, 0, 0)),
            pl.BlockSpec((None, 1, HEAD_DIM), lambda b, h: (l, 0, h)),
        ],
        out_specs=pl.BlockSpec((None, t, HEAD_DIM), lambda b, h: (b, 0, h)),
        out_shape=jax.ShapeDtypeStruct((b, t, DA_W), BF16),
        scratch_shapes=[pltpu.VMEM((t, HEAD_DIM), BF16), pltpu.VMEM((t, 2 * HEAD_DIM), BF16)],
        compiler_params=_params(("arbitrary", "arbitrary")),
        name="attn_prompt",
    )(q_new, k_stack, v_stack, lam_p, da_g)


DEC_ROWS = N_DA * 2 * SUBLANES
PAGE_ROWS = PAGE * N_DA


def _attn_decode_kernel(pt_ref, q_ref, kn_ref, vn_ref, lam_ref, g_ref, *rest,
                        pages, n_steps, t_new, lam_init):
    k_refs = rest[:pages]
    v_refs = rest[pages:2 * pages]
    o_ref, qm, bias, m_scr, l_scr, acc_scr, kpad, vpad = rest[2 * pages:]
    step = pl.program_id(1)

    @pl.when(step == 0)
    def _():
        q = q_ref[...] * SCORE_SCALE_LOG2
        lane = lax.broadcasted_iota(jnp.int32, (SUBLANES, HEAD_DIM), 1)
        for h in range(N_DA):
            qh = q[:, h * HEAD_DIM:(h + 1) * HEAD_DIM]
            r0 = h * 2 * SUBLANES
            qm[r0:r0 + SUBLANES, :] = jnp.where(lane < DA_QK, qh, 0.0)
            qm[r0 + SUBLANES:r0 + 2 * SUBLANES, :] = jnp.where(lane >= DA_QK, qh, 0.0)
        row = lax.broadcasted_iota(jnp.int32, bias.shape, 0)
        col = lax.broadcasted_iota(jnp.int32, bias.shape, 1)
        bias[...] = jnp.where((col & (N_DA - 1)) == (row >> 4), 0.0, NEG_INF)
        m_scr[...] = jnp.full(m_scr.shape, -jnp.inf, F32)
        l_scr[...] = jnp.zeros(l_scr.shape, F32)
        acc_scr[...] = jnp.zeros(acc_scr.shape, F32)

    qb = qm[...].astype(BF16)
    head_bias = bias[...]
    group = 2 if pages % 2 == 0 else 1

    def scores_of(g):
        return [lax.dot_general(qb, k_refs[p][...].astype(BF16), NT_DIMS,
                                preferred_element_type=F32) + head_bias
                for p in range(g * group, (g + 1) * group)]

    m_run, l_run, acc = m_scr[...], l_scr[...], acc_scr[...]
    upcoming = scores_of(0)
    for g in range(pages // group):
        scores = upcoming
        if (g + 1) * group < pages:
            upcoming = scores_of(g + 1)
        m_new = m_run
        for s in scores:
            m_new = jnp.maximum(m_new, jnp.max(s, axis=1, keepdims=True))
        alpha = jnp.exp2(m_run - m_new)
        l_run = alpha * l_run
        pv = None
        for i, s in enumerate(scores):
            e = jnp.exp2(s - m_new)
            l_run = l_run + jnp.sum(e, axis=1, keepdims=True)
            d = jnp.dot(e.astype(BF16), v_refs[g * group + i][...].astype(BF16),
                        preferred_element_type=F32)
            pv = d if pv is None else pv + d
        acc = alpha * acc + pv
        m_run = m_new
    m_scr[...] = m_run
    l_scr[...] = l_run
    acc_scr[...] = acc

    @pl.when(step == n_steps - 1)
    def _():
        kpad[...] = jnp.zeros(kpad.shape, F32)
        vpad[...] = jnp.zeros(vpad.shape, F32)
        kpad[0:t_new * N_DA, :] = kn_ref[...]
        vpad[0:t_new * N_DA, :] = vn_ref[...]
        s = lax.dot_general(qb, kpad[...].astype(BF16), NT_DIMS, preferred_element_type=F32)
        row = lax.broadcasted_iota(jnp.int32, s.shape, 0)
        col = lax.broadcasted_iota(jnp.int32, s.shape, 1)
        causal = (col >> 3) <= (row & (SUBLANES - 1))
        s = jnp.where(causal, s + head_bias, NEG_INF)
        _online_softmax_step(s, vpad[...].astype(BF16), m_scr, l_scr, acc_scr)

        lam = _lambda(lam_ref, lam_init)
        o = acc_scr[...] / l_scr[...]
        for h in range(N_DA):
            r0 = h * 2 * SUBLANES
            cols = slice(h * HEAD_DIM, (h + 1) * HEAD_DIM)
            oa = o[r0:r0 + SUBLANES] - lam * o[r0 + SUBLANES:r0 + 2 * SUBLANES]
            o_ref[:, cols] = _rms(oa) * g_ref[:, cols] * (1.0 - lam_init)


def _attn_decode(q_new, k_new, v_new, cache_k, cache_v, page_table, lam_p, da_g, l, lam_init):
    b, t_new, _ = q_new.shape
    assert t_new == SUBLANES and cache_k.shape[2:] == (PAGE, N_DA, HEAD_DIM)
    n_pages = page_table.shape[1]
    pages = _tile(n_pages, 8, 1)
    n_steps = n_pages // pages
    n_pool = cache_k.shape[1]
    ck = cache_k.reshape(cache_k.shape[0], n_pool, PAGE_ROWS, HEAD_DIM)
    cv = cache_v.reshape(cache_v.shape[0], n_pool, PAGE_ROWS, HEAD_DIM)
    kn = k_new.reshape(b, t_new * N_DA, HEAD_DIM)
    vn = v_new.reshape(b, t_new * N_DA, HEAD_DIM)

    def page_spec(p):
        return pl.BlockSpec((None, None, PAGE_ROWS, HEAD_DIM),
                            lambda b, s, pt: (l, pt[b * n_pages + s * pages + p], 0, 0))

    new_spec = pl.BlockSpec((None, t_new * N_DA, HEAD_DIM), lambda b, s, pt: (b, 0, 0))
    grid_spec = pltpu.PrefetchScalarGridSpec(
        num_scalar_prefetch=1,
        grid=(b, n_steps),
        in_specs=[
            pl.BlockSpec((None, t_new, DA_W), lambda b, s, pt: (b, 0, 0)),
            new_spec, new_spec,
            pl.BlockSpec((None, 4, DA_QK), lambda b, s, pt: (l, 0, 0)),
            pl.BlockSpec((None, 1, DA_W), lambda b, s, pt: (l, 0, 0)),
        ] + [page_spec(p) for p in range(pages)] * 2,
        out_specs=pl.BlockSpec((None, t_new, DA_W), lambda b, s, pt: (b, 0, 0)),
        scratch_shapes=[
            pltpu.VMEM((DEC_ROWS, HEAD_DIM), F32),
            pltpu.VMEM((DEC_ROWS, PAGE_ROWS), F32),
            pltpu.VMEM((DEC_ROWS, 1), F32), pltpu.VMEM((DEC_ROWS, 1), F32),
            pltpu.VMEM((DEC_ROWS, HEAD_DIM), F32),
            pltpu.VMEM((PAGE_ROWS, HEAD_DIM), F32), pltpu.VMEM((PAGE_ROWS, HEAD_DIM), F32),
        ],
    )
    return pl.pallas_call(
        functools.partial(_attn_decode_kernel, pages=pages, n_steps=n_steps, t_new=t_new,
                          lam_init=lam_init),
        grid_spec=grid_spec,
        out_shape=jax.ShapeDtypeStruct((b, t_new, DA_W), F32),
        compiler_params=_params(("arbitrary", "arbitrary")),
        name="attn_decode",
    )(page_table.reshape(-1), q_new, kn, vn, lam_p, da_g, *([ck] * pages), *([cv] * pages))


def _rotary(x, cos2, sin2):
    return x * cos2 + pltpu.roll(x, DA_QK, 1) * sin2


def _ret_parts(q_ref, k_ref, v_ref, gate_ref, cos_ref, sin_ref, lg_ref, s0_ref, gn_ref,
               o_ref, sout_ref, s_scr, *, l_true):
    L = CHUNK
    lg = lg_ref[...][:, 0:1]
    ti = lax.broadcasted_iota(jnp.int32, (L, L), 0)
    si = lax.broadcasted_iota(jnp.int32, (L, L), 1)
    rel = (ti - si).astype(F32)
    dmat = jnp.where(rel >= 0, jnp.exp(lg * jnp.maximum(rel, 0.0)), 0.0)
    idx = lax.broadcasted_iota(jnp.int32, (L, 1), 0).astype(F32)
    q_dec = jnp.exp(lg * (idx + 1.0))
    k_dec = jnp.exp(lg * (l_true - 1.0 - idx))
    s_dec = jnp.exp(lg * float(l_true))
    s_scr[...] = s0_ref[...]

    def body(c):
        rows = pl.ds(c * L if isinstance(c, int) else pl.multiple_of(c * L, L), L)
        cos2, sin2 = cos_ref[rows, :], sin_ref[rows, :]
        q = _rotary(q_ref[rows, :], cos2, sin2)
        k = _rotary(k_ref[rows, :], cos2, sin2) * (HEAD_DIM ** -0.5)
        q_bf, k_bf, v_bf = q.astype(BF16), k.astype(BF16), v_ref[rows, :].astype(BF16)
        state = s_scr[...]
        scores = lax.dot_general(q_bf, k_bf, NT_DIMS, preferred_element_type=F32) * dmat
        inner = jnp.dot(scores.astype(BF16), v_bf, preferred_element_type=F32)
        cross = jnp.dot(q_bf, state.astype(BF16), preferred_element_type=F32) * q_dec
        kd_t = (k * k_dec).T.astype(BF16)
        s_scr[...] = state * s_dec + jnp.dot(kd_t, v_bf, preferred_element_type=F32)
        g = gate_ref[rows, :]
        y = _rms(inner + cross) * gn_ref[...]
        o_ref[rows, :] = (y.astype(F32) * (g * jax.nn.sigmoid(g))).astype(o_ref.dtype)

    def finalize():
        sout_ref[...] = s_scr[...]

    return body, finalize


N_RET_IN, N_RET_OUT, N_RET_SCR = 9, 2, 1


def _retention_call(proj, cos2, sin2, lg_tab, s0, gn, l):
    b, t, _ = proj.shape

    def col(c0):
        return pl.BlockSpec((None, t, HEAD_DIM), lambda b, h: (b, 0, c0 // HEAD_DIM + h))

    st_spec = pl.BlockSpec((None, None, HEAD_DIM, HEAD_DIM), lambda b, h: (b, h, 0, 0))
    in_specs = [
        col(COL_RQ), col(COL_RK), col(COL_RV), col(COL_RG),
        pl.BlockSpec((t, HEAD_DIM), lambda b, h: (0, 0)),
        pl.BlockSpec((t, HEAD_DIM), lambda b, h: (0, 0)),
        pl.BlockSpec((None, 1, HEAD_DIM), lambda b, h: (h, 0, 0)),
        st_spec,
        pl.BlockSpec((None, 1, HEAD_DIM), lambda b, h: (l, 0, h)),
    ]
    out_specs = [pl.BlockSpec((None, t, HEAD_DIM), lambda b, h: (b, 0, h)), st_spec]
    out_shape = [jax.ShapeDtypeStruct((b, t, RET_W), BF16),
                 jax.ShapeDtypeStruct((b, N_RET, HEAD_DIM, HEAD_DIM), F32)]
    scratch = [pltpu.VMEM((HEAD_DIM, HEAD_DIM), F32)]
    args = (proj, proj, proj, proj, cos2, sin2, lg_tab, s0, gn)
    return in_specs, args, out_specs, out_shape, scratch


def _row_to_col(row, eye):
    return jnp.sum(jnp.where(eye, row, 0.0), axis=1, keepdims=True)


def _cumsum_lanes(x, lane):
    n = x.shape[1]
    shift = 1
    while shift < n:
        x = x + jnp.where(lane >= shift, pltpu.roll(x, shift, 1), 0.0)
        shift *= 2
    return x


def _mlstm_parts(u_ref, v_ref, og_ref, prev_ref, ig_ref, fg_ref, ib_ref, fb_ref, cw_ref, cb_ref,
                 wq_ref, wk_ref, c0_ref, n0_ref, m0_ref, gn_ref, skip_ref,
                 o_ref, cout_ref, nout_ref, mout_ref,
                 xp_scr, uc_scr, c_scr, n_scr, m_scr, *, t, l_true):
    L = CHUNK
    halo = SUBLANES

    xp_scr[0:halo, :] = prev_ref[...]
    xp_scr[halo:halo + t, :] = u_ref[...]
    conv = cb_ref[...]
    for j in range(ML_CONV):
        start = halo - (ML_CONV - 1) + j
        conv = conv + xp_scr[start:start + t, :] * cw_ref[j:j + 1, :]
    uc_scr[...] = conv * jax.nn.sigmoid(conv)

    wq_bf = wq_ref[...].astype(BF16)
    wk_bf = wk_ref[...].astype(BF16)
    c_scr[...] = c0_ref[...]
    n_scr[...] = n0_ref[...]
    m_scr[...] = m0_ref[...]

    ti = lax.broadcasted_iota(jnp.int32, (L, L), 0)
    si = lax.broadcasted_iota(jnp.int32, (L, L), 1)
    eye = ti == si
    causal = si <= ti
    lane = lax.broadcasted_iota(jnp.int32, (1, L), 1)

    def body(c):
        rows = pl.ds(c * L if isinstance(c, int) else pl.multiple_of(c * L, L), L)
        uc = uc_scr[rows, :]
        uc_bf = uc.astype(BF16)
        q = jnp.dot(uc_bf, wq_bf, preferred_element_type=F32)
        k = jnp.dot(uc_bf, wk_bf, preferred_element_type=F32) * (HEAD_DIM ** -0.5)
        q_bf, k_bf, v_bf = q.astype(BF16), k.astype(BF16), v_ref[rows, :].astype(BF16)

        ig = ig_ref[:, rows] + ib_ref[...]
        fg = fg_ref[:, rows] + fb_ref[...]
        lf = -(jnp.maximum(-fg, 0.0) + jnp.log(1.0 + jnp.exp(-jnp.abs(fg))))
        b_row = _cumsum_lanes(lf, lane)
        b_col = _row_to_col(b_row, eye)
        m_old = m_scr[...]
        c_old = c_scr[...]
        n_old = n_scr[...]

        dlog = jnp.where(causal, b_col + (ig - b_row), -jnp.inf)
        inter = b_col + m_old
        m_row = jnp.maximum(inter, jnp.max(dlog, axis=1, keepdims=True))
        w = jnp.exp(dlog - m_row)
        s_in = jnp.exp(inter - m_row)
        a = w * lax.dot_general(q_bf, k_bf, NT_DIMS, preferred_element_type=F32)
        num = (jnp.dot(a.astype(BF16), v_bf, preferred_element_type=F32)
               + jnp.dot(q_bf, c_old.astype(BF16), preferred_element_type=F32) * s_in)
        qn = jnp.sum(q_bf.astype(F32) * n_old.astype(BF16).astype(F32), axis=1, keepdims=True)
        den = jnp.sum(a, axis=1, keepdims=True) + qn * s_in
        hh = num / jnp.maximum(jnp.abs(den), jnp.exp(-m_row))

        og = og_ref[rows, :]
        y = _rms(hh) * gn_ref[...] + skip_ref[...] * uc
        o_ref[rows, :] = (jax.nn.sigmoid(og) * y).astype(o_ref.dtype)

        b_last = jnp.sum(jnp.where(lane == l_true - 1, b_row, 0.0), axis=1, keepdims=True)
        dlast = jnp.where(lane < l_true, b_last - b_row + ig, -jnp.inf)
        m_new = jnp.maximum(b_last + m_old, jnp.max(dlast, axis=1, keepdims=True))
        ws_col = _row_to_col(jnp.exp(dlast - m_new), eye)
        s_old = jnp.exp(b_last + m_old - m_new)
        kw = k * ws_col
        c_scr[...] = c_old * s_old + jnp.dot(kw.T.astype(BF16), v_bf, preferred_element_type=F32)
        n_scr[...] = n_old * s_old + jnp.sum(kw, axis=0, keepdims=True)
        m_scr[...] = m_new

    def finalize():
        cout_ref[...] = c_scr[...]
        nout_ref[...] = n_scr[...]
        mout_ref[...] = m_scr[...]

    return body, finalize


N_ML_IN, N_ML_OUT, N_ML_SCR = 17, 4, 5


def _mlstm_call(proj, prev, gates, gbias, cw, cb, wq, wk, c0, n0, m0, gn, skip, l):
    b, t, _ = proj.shape

    def col(c0_):
        return pl.BlockSpec((None, t, HEAD_DIM), lambda b, h: (b, 0, c0_ // HEAD_DIM + h))

    def per_head(rows):
        return pl.BlockSpec((None, rows, HEAD_DIM), lambda b, h: (l, 0, h))

    c_spec = pl.BlockSpec((None, None, HEAD_DIM, HEAD_DIM), lambda b, h: (b, h, 0, 0))
    n_spec = pl.BlockSpec((None, None, 1, HEAD_DIM), lambda b, h: (b, h, 0, 0))
    m_spec = pl.BlockSpec((None, None, 1, 1), lambda b, h: (b, h, 0, 0))
    w_spec = pl.BlockSpec((None, None, HEAD_DIM, HEAD_DIM), lambda b, h: (l, h, 0, 0))
    in_specs = [
        col(COL_MU), col(COL_MV), col(COL_MO),
        pl.BlockSpec((None, SUBLANES, HEAD_DIM), lambda b, h: (b, 0, h)),
        pl.BlockSpec((None, None, 1, t), lambda b, h: (b, h, 0, 0)),
        pl.BlockSpec((None, None, 1, t), lambda b, h: (b, N_ML + h, 0, 0)),
        pl.BlockSpec((None, None, 1, 1), lambda b, h: (l, h, 0, 0)),
        pl.BlockSpec((None, None, 1, 1), lambda b, h: (l, N_ML + h, 0, 0)),
        per_head(ML_CONV), per_head(1),
        w_spec, w_spec, c_spec, n_spec, m_spec,
        per_head(1), per_head(1),
    ]
    out_specs = [pl.BlockSpec((None, t, HEAD_DIM), lambda b, h: (b, 0, h)), c_spec, n_spec, m_spec]
    out_shape = [jax.ShapeDtypeStruct((b, t, ML_W), BF16),
                 jax.ShapeDtypeStruct((b, N_ML, HEAD_DIM, HEAD_DIM), F32),
                 jax.ShapeDtypeStruct((b, N_ML, 1, HEAD_DIM), F32),
                 jax.ShapeDtypeStruct((b, N_ML, 1, 1), F32)]
    scratch = [pltpu.VMEM((SUBLANES + t, HEAD_DIM), F32), pltpu.VMEM((t, HEAD_DIM), F32),
               pltpu.VMEM((HEAD_DIM, HEAD_DIM), F32), pltpu.VMEM((1, HEAD_DIM), F32),
               pltpu.VMEM((1, 1), F32)]
    args = (proj, proj, proj, prev, gates[:, :, None, :], gates[:, :, None, :], gbias, gbias,
            cw, cb, wq, wk, c0, n0, m0, gn, skip)
    return in_specs, args, out_specs, out_shape, scratch


def _seqmix_kernel(*refs, t, n_chunks, l_true):
    assert N_RET == N_ML
    n_in, n_out = N_RET_IN + N_ML_IN, N_RET_OUT + N_ML_OUT
    ins, outs, scr = refs[:n_in], refs[n_in:n_in + n_out], refs[n_in + n_out:]
    ret_body, ret_fin = _ret_parts(*ins[:N_RET_IN], *outs[:N_RET_OUT], *scr[:N_RET_SCR],
                                   l_true=l_true)
    ml_body, ml_fin = _mlstm_parts(*ins[N_RET_IN:], *outs[N_RET_OUT:], *scr[N_RET_SCR:],
                                   t=t, l_true=l_true)

    if n_chunks <= CHUNK_UNROLL:
        for c in range(n_chunks):
            ret_body(c)
            ml_body(c)
    else:
        def body(c, carry):
            ret_body(c)
            ml_body(c)
            return carry

        lax.fori_loop(0, n_chunks, body, 0, unroll=CHUNK_UNROLL)
    ret_fin()
    ml_fin()


def _seqmix(ret_call, ml_call, b, t, l_true):
    in_specs, args, out_specs, out_shape, scratch = (r + m for r, m in zip(ret_call, ml_call))
    return pl.pallas_call(
        functools.partial(_seqmix_kernel, t=t, n_chunks=t // CHUNK, l_true=l_true),
        grid=(b, N_RET),
        in_specs=in_specs,
        out_specs=out_specs,
        out_shape=out_shape,
        scratch_shapes=scratch,
        compiler_params=_params(("arbitrary", "arbitrary")),
        name="seqmix",
    )(*args)


def _ret_kernel(*refs, n_chunks, l_true):
    body, finalize = _ret_parts(*refs, l_true=l_true)
    if n_chunks <= CHUNK_UNROLL:
        for c in range(n_chunks):
            body(c)
    else:
        lax.fori_loop(0, n_chunks, lambda c, carry: (body(c), carry)[1], 0, unroll=CHUNK_UNROLL)
    finalize()


def _retention(ret_call, b, t, l_true):
    in_specs, args, out_specs, out_shape, scratch = ret_call
    return pl.pallas_call(
        functools.partial(_ret_kernel, n_chunks=t // CHUNK, l_true=l_true),
        grid=(b, N_RET),
        in_specs=in_specs, out_specs=out_specs, out_shape=out_shape, scratch_shapes=scratch,
        compiler_params=_params(("arbitrary", "arbitrary")),
        name="retention",
    )(*args)


def _mlstm_kernel(u_ref, v_ref, og_ref, prev_ref, g_ref, gb_ref, cw_ref, cb_ref, wq_ref, wk_ref,
                  c0_ref, n0_ref, m0_ref, gn_ref, skip_ref,
                  o_ref, cout_ref, nout_ref, mout_ref,
                  xp_scr, uc_scr, c_scr, n_scr, m_scr, *, t, n_chunks, l_true):
    L, H = CHUNK, N_ML
    halo = SUBLANES

    xp_scr[0:halo, :] = prev_ref[...]
    xp_scr[halo:halo + t, :] = u_ref[...]
    conv = cb_ref[...]
    for j in range(ML_CONV):
        start = halo - (ML_CONV - 1) + j
        conv = conv + xp_scr[start:start + t, :] * cw_ref[j:j + 1, :]
    uc_scr[...] = conv * jax.nn.sigmoid(conv)

    wq_bf = [wq_ref[h].astype(BF16) for h in range(H)]
    wk_bf = [wk_ref[h].astype(BF16) for h in range(H)]
    c_scr[...] = c0_ref[...]
    n_scr[...] = n0_ref[...]
    m_scr[...] = m0_ref[...]

    ti = lax.broadcasted_iota(jnp.int32, (H * L, L), 0) & (L - 1)
    si = lax.broadcasted_iota(jnp.int32, (H * L, L), 1)
    eye = ti == si
    causal = si <= ti
    lane = lax.broadcasted_iota(jnp.int32, (H, L), 1)

    def heads(x):
        return [x[:, h * HEAD_DIM:(h + 1) * HEAD_DIM] for h in range(H)]

    def block(x, h):
        return x[h * L:(h + 1) * L]

    def stack(xs):
        return jnp.concatenate(xs, axis=0)

    def rows_to_blocks(r):
        return stack([jnp.broadcast_to(r[h:h + 1, :], (L, L)) for h in range(H)])

    def per_head_col(v):
        return stack([jnp.broadcast_to(v[h:h + 1, :], (L, 1)) for h in range(H)])

    def rows_to_col(r):
        return jnp.sum(jnp.where(eye, rows_to_blocks(r), 0.0), axis=1, keepdims=True)

    def body(c):
        rows = pl.ds(c * L if isinstance(c, int) else pl.multiple_of(c * L, L), L)
        uc_h = heads(uc_scr[rows, :])
        q_h = [jnp.dot(uc_h[h].astype(BF16), wq_bf[h], preferred_element_type=F32) for h in range(H)]
        k_h = [jnp.dot(uc_h[h].astype(BF16), wk_bf[h], preferred_element_type=F32)
               * (HEAD_DIM ** -0.5) for h in range(H)]
        q_bf = [x.astype(BF16) for x in q_h]
        k_bf = [x.astype(BF16) for x in k_h]
        v_bf = [x.astype(BF16) for x in heads(v_ref[rows, :])]

        gates = g_ref[:, rows] + gb_ref[...]
        ig, fg = gates[0:H], gates[H:2 * H]
        lf = -(jnp.maximum(-fg, 0.0) + jnp.log(1.0 + jnp.exp(-jnp.abs(fg))))
        b_row = _cumsum_lanes(lf, lane)
        b_col = rows_to_col(b_row)
        m_old = m_scr[...]
        n_old = n_scr[...]
        c_old = [c_scr[h] for h in range(H)]

        dlog = jnp.where(causal, b_col + rows_to_blocks(ig - b_row), -jnp.inf)
        inter = b_col + per_head_col(m_old)
        m_row = jnp.maximum(inter, jnp.max(dlog, axis=1, keepdims=True))
        w = jnp.exp(dlog - m_row)
        s_in = jnp.exp(inter - m_row)
        qk = stack([lax.dot_general(q_bf[h], k_bf[h], NT_DIMS, preferred_element_type=F32)
                    for h in range(H)])
        a = w * qk
        a_bf = a.astype(BF16)
        num = stack([jnp.dot(block(a_bf, h), v_bf[h], preferred_element_type=F32)
                     + jnp.dot(q_bf[h], c_old[h].astype(BF16), preferred_element_type=F32)
                     * block(s_in, h) for h in range(H)])
        qn = stack([jnp.sum(q_bf[h].astype(F32) * n_old[h:h + 1, :].astype(BF16).astype(F32),
                            axis=1, keepdims=True) for h in range(H)])
        den = jnp.sum(a, axis=1, keepdims=True) + qn * s_in
        hn = _rms(num / jnp.maximum(jnp.abs(den), jnp.exp(-m_row)))

        og_h = heads(og_ref[rows, :])
        gn_h, skip_h = heads(gn_ref[...]), heads(skip_ref[...])
        for h in range(H):
            y = block(hn, h) * gn_h[h] + skip_h[h] * uc_h[h]
            o_ref[rows, h * HEAD_DIM:(h + 1) * HEAD_DIM] = (
                jax.nn.sigmoid(og_h[h]) * y).astype(o_ref.dtype)

        b_last = jnp.sum(jnp.where(lane == l_true - 1, b_row, 0.0), axis=1, keepdims=True)
        dlast = jnp.where(lane < l_true, b_last - b_row + ig, -jnp.inf)
        m_new = jnp.maximum(b_last + m_old, jnp.max(dlast, axis=1, keepdims=True))
        ws_col = rows_to_col(jnp.exp(dlast - m_new))
        s_old = jnp.exp(b_last + m_old - m_new)
        kw = stack(k_h) * ws_col
        for h in range(H):
            c_scr[h] = c_old[h] * s_old[h:h + 1, :] + jnp.dot(
                block(kw, h).T.astype(BF16), v_bf[h], preferred_element_type=F32)
        n_scr[...] = n_old * s_old + stack(
            [jnp.sum(block(kw, h), axis=0, keepdims=True) for h in range(H)])
        m_scr[...] = m_new

    if n_chunks <= CHUNK_UNROLL:
        for c in range(n_chunks):
            body(c)
    else:
        lax.fori_loop(0, n_chunks, lambda c, carry: (body(c), carry)[1], 0, unroll=CHUNK_UNROLL)
    cout_ref[...] = c_scr[...]
    nout_ref[...] = n_scr[...]
    mout_ref[...] = m_scr[...]


def _mlstm(ml_in, prev, gates, gbias, cw, cb, wq, wk, c0, n0, m0, gn, skip, l, l_true):
    b, t, _ = ml_in.shape

    def seq(j):
        return pl.BlockSpec((None, t, ML_W), lambda b: (b, 0, j))

    def layer(rows):
        return pl.BlockSpec((None, rows, ML_W), lambda b: (l, 0, 0))

    w_spec = pl.BlockSpec((None, N_ML, HEAD_DIM, HEAD_DIM), lambda b: (l, 0, 0, 0))
    c_spec = pl.BlockSpec((None, N_ML, HEAD_DIM, HEAD_DIM), lambda b: (b, 0, 0, 0))
    n_spec = pl.BlockSpec((None, N_ML, HEAD_DIM), lambda b: (b, 0, 0))
    m_spec = pl.BlockSpec((None, N_ML, 1), lambda b: (b, 0, 0))
    return pl.pallas_call(
        functools.partial(_mlstm_kernel, t=t, n_chunks=t // CHUNK, l_true=l_true),
        grid=(b,),
        in_specs=[
            seq(COL_MU // ML_W), seq(COL_MV // ML_W), seq(COL_MO // ML_W),
            pl.BlockSpec((None, SUBLANES, ML_W), lambda b: (b, 0, 0)),
            pl.BlockSpec((None, N_GATES, t), lambda b: (b, 0, 0)),
            pl.BlockSpec((None, N_GATES, 1), lambda b: (l, 0, 0)),
            layer(ML_CONV), layer(1), w_spec, w_spec, c_spec, n_spec, m_spec, layer(1), layer(1),
        ],
        out_specs=[pl.BlockSpec((None, t, ML_W), lambda b: (b, 0, 0)), c_spec, n_spec, m_spec],
        out_shape=[jax.ShapeDtypeStruct((b, t, ML_W), BF16),
                   jax.ShapeDtypeStruct((b, N_ML, HEAD_DIM, HEAD_DIM), F32),
                   jax.ShapeDtypeStruct((b, N_ML, HEAD_DIM), F32),
                   jax.ShapeDtypeStruct((b, N_ML, 1), F32)],
        scratch_shapes=[
            pltpu.VMEM((SUBLANES + t, ML_W), F32), pltpu.VMEM((t, ML_W), F32),
            pltpu.VMEM((N_ML, HEAD_DIM, HEAD_DIM), F32), pltpu.VMEM((N_ML, HEAD_DIM), F32),
            pltpu.VMEM((N_ML, 1), F32),
        ],
        compiler_params=_params(("arbitrary",)),
        name="mlstm",
    )(ml_in, ml_in, ml_in, prev, gates, gbias, cw, cb, wq, wk, c0, n0, m0, gn, skip)


def _conv_gate_kernel(ug_ref, uv_ref, hg_ref, hv_ref, pg_ref, pv_ref, wg_ref, wv_ref,
                      bg_ref, bv_ref, o_ref, sg, sv, *, tt):
    first = pl.program_id(1) == 0
    halo = SUBLANES

    def conv(u_ref, h_ref, p_ref, w_ref, b_ref, scr):
        @pl.when(first)
        def _():
            scr[0:halo, :] = p_ref[...]

        @pl.when(jnp.logical_not(first))
        def _():
            scr[0:halo, :] = h_ref[...]

        scr[halo:halo + tt, :] = u_ref[...]
        y = b_ref[...]
        for j in range(FFN_CONV):
            start = halo - (FFN_CONV - 1) + j
            y = y + scr[start:start + tt, :] * w_ref[j:j + 1, :]
        return y

    gate = conv(ug_ref, hg_ref, pg_ref, wg_ref, bg_ref, sg)
    val = conv(uv_ref, hv_ref, pv_ref, wv_ref, bv_ref, sv)
    o_ref[...] = (gate * jax.nn.sigmoid(gate) * val).astype(o_ref.dtype)


def _conv_gate(up, prev, cw, cb, l):
    b, t, f2 = up.shape
    f = f2 // 2
    tt = _tile(t, 512, SUBLANES)
    tn = _tile(f, max(512, 512 * 512 // tt))
    nf = f // tn
    rb = tt // SUBLANES

    def main(off):
        return pl.BlockSpec((None, tt, tn), lambda b, i, n: (b, i, n + off))

    def halo(off):
        return pl.BlockSpec((None, SUBLANES, tn),
                            lambda b, i, n: (b, jnp.maximum(i * rb - 1, 0), n + off))

    def prev_spec(off):
        return pl.BlockSpec((None, SUBLANES, tn), lambda b, i, n: (b, 0, n + off))

    def w_spec(rows, off):
        return pl.BlockSpec((None, rows, tn), lambda b, i, n: (l, 0, n + off))

    return pl.pallas_call(
        functools.partial(_conv_gate_kernel, tt=tt),
        grid=(b, t // tt, nf),
        in_specs=[main(0), main(nf), halo(0), halo(nf), prev_spec(0), prev_spec(nf),
                  w_spec(FFN_CONV, 0), w_spec(FFN_CONV, nf), w_spec(1, 0), w_spec(1, nf)],
        out_specs=pl.BlockSpec((None, tt, tn), lambda b, i, n: (b, i, n)),
        out_shape=jax.ShapeDtypeStruct((b, t, f), BF16),
        scratch_shapes=[pltpu.VMEM((SUBLANES + tt, tn), F32), pltpu.VMEM((SUBLANES + tt, tn), F32)],
        compiler_params=_params(("arbitrary", "arbitrary", "arbitrary")),
        name="conv_gate",
    )(up, up, up, up, prev, prev, cw, cw, cb, cb)


def _ffn_up_kernel(x_ref, wg_ref, wv_ref, pg_ref, pv_ref, cwg_ref, cwv_ref, cbg_ref, cbv_ref,
                   xs_ref, o_ref, sg_ref, sv_ref, ups_g_ref, ups_v_ref,
                   wbf_g, wbf_v, scr_g, scr_v, *, tm, n_sub):
    t = pl.program_id(2)
    halo = SUBLANES

    @pl.when(jnp.logical_and(pl.program_id(1) == 0, t == 0))
    def _():
        wbf_g[...] = wg_ref[...].astype(BF16)
        wbf_v[...] = wv_ref[...].astype(BF16)
        ups_g_ref[...] = jnp.dot(xs_ref[...], wbf_g[...], preferred_element_type=F32)
        ups_v_ref[...] = jnp.dot(xs_ref[...], wbf_v[...], preferred_element_type=F32)

    @pl.when(t == 0)
    def _():
        scr_g[0:halo, :] = pg_ref[...]
        scr_v[0:halo, :] = pv_ref[...]

    sub = tm // n_sub

    def project(c):
        rows = slice(c * sub, (c + 1) * sub)
        x = x_ref[rows, :]
        dst = slice(halo + c * sub, halo + (c + 1) * sub)
        scr_g[dst, :] = jnp.dot(x, wbf_g[...], preferred_element_type=F32)
        scr_v[dst, :] = jnp.dot(x, wbf_v[...], preferred_element_type=F32)

    def conv(c, scr, cw_ref, cb_ref):
        y = cb_ref[...]
        for j in range(FFN_CONV):
            start = halo - (FFN_CONV - 1) + j + c * sub
            y = y + scr[start:start + sub, :] * cw_ref[j:j + 1, :]
        return y

    project(0)
    for c in range(n_sub):
        if c + 1 < n_sub:
            project(c + 1)
        gate = conv(c, scr_g, cwg_ref, cbg_ref)
        val = conv(c, scr_v, cwv_ref, cbv_ref)
        o_ref[c * sub:(c + 1) * sub, :] = (gate * jax.nn.sigmoid(gate) * val).astype(o_ref.dtype)

    for scr, s_ref in ((scr_g, sg_ref), (scr_v, sv_ref)):
        tail = scr[tm:tm + halo, :]
        s_ref[...] = tail
        scr[0:halo, :] = tail


def _ffn_up_fused(x, w_up, prev, cw, cb, l, x_side):
    b, t, d = x.shape
    ts = x_side.shape[1]
    f = w_up.shape[2] // 2
    tm = _tile(t, 1024, SUBLANES)
    tn = _tile(f, 512)
    nf = f // tn

    def cols(rows, off, arr_l):
        return pl.BlockSpec((None, rows, tn), lambda n, b, i: (arr_l(b), 0, n + off))

    layer = lambda b: l
    batch = lambda b: b
    tail_spec = pl.BlockSpec((None, SUBLANES, tn), lambda n, b, i: (b, 0, n))
    side_spec = pl.BlockSpec((None, ts, tn), lambda n, b, i: (0, 0, n))
    return pl.pallas_call(
        functools.partial(_ffn_up_kernel, tm=tm, n_sub=FFN_SUB if tm % (FFN_SUB * 16) == 0 else 1),
        grid=(nf, b, t // tm),
        in_specs=[
            pl.BlockSpec((None, tm, d), lambda n, b, i: (b, i, 0)),
            cols(d, 0, layer), cols(d, nf, layer),
            cols(SUBLANES, 0, batch), cols(SUBLANES, nf, batch),
            cols(FFN_CONV, 0, layer), cols(FFN_CONV, nf, layer),
            cols(1, 0, layer), cols(1, nf, layer),
            pl.BlockSpec((None, ts, d), lambda n, b, i: (0, 0, 0)),
        ],
        out_specs=[pl.BlockSpec((None, tm, tn), lambda n, b, i: (b, i, n)), tail_spec, tail_spec,
                   side_spec, side_spec],
        out_shape=[jax.ShapeDtypeStruct((b, t, f), BF16),
                   jax.ShapeDtypeStruct((b, SUBLANES, f), F32),
                   jax.ShapeDtypeStruct((b, SUBLANES, f), F32),
                   jax.ShapeDtypeStruct((1, ts, f), F32),
                   jax.ShapeDtypeStruct((1, ts, f), F32)],
        scratch_shapes=[pltpu.VMEM((d, tn), BF16), pltpu.VMEM((d, tn), BF16),
                        pltpu.VMEM((SUBLANES + tm, tn), F32), pltpu.VMEM((SUBLANES + tm, tn), F32)],
        compiler_params=_params(("arbitrary", "arbitrary", "arbitrary")),
        name="ffn_up_conv",
    )(x, w_up, w_up, prev, prev, cw, cw, cb, cb, x_side)


def _rope_tables(pos, rows):
    half = HEAD_DIM // 2
    inv = ROPE_BASE ** (-jnp.arange(half, dtype=F32) / half)
    ang = pos.astype(F32)[:, None] * inv[None, :]
    cos, sin = jnp.cos(ang), jnp.sin(ang)
    cos2 = jnp.concatenate([cos, cos], axis=-1)
    sin2 = jnp.concatenate([-sin, sin], axis=-1)
    pad = rows - pos.shape[0]
    if pad:
        cos2 = jnp.pad(cos2, ((0, pad), (0, 0)))
        sin2 = jnp.pad(sin2, ((0, pad), (0, 0)))
    return cos2, sin2


def _prev_rows(buf):
    return jnp.pad(buf, ((0, 0), (SUBLANES - buf.shape[1], 0), (0, 0)))


def _mixers(l, b, t, ret_in, ml_in, gates, st, w, rope, lg_tab):
    t_pad = -(-t // CHUNK) * CHUNK
    l_true = CHUNK if t % CHUNK == 0 else t
    if t_pad != t:
        ret_in = jnp.pad(ret_in, ((0, 0), (0, t_pad - t), (0, 0)))
        ml_in = jnp.pad(ml_in, ((0, 0), (0, t_pad - t), (0, 0)))
        gates = jnp.pad(gates, ((0, 0), (0, 0), (0, t_pad - t)))
    ret_call = _retention_call(ret_in, rope[0], rope[1], lg_tab, st["ret_s"], w["ret_norm_g"], l)
    ro, ret_s = _retention(ret_call, b, t_pad, l_true)
    mo, ml_c, ml_n, ml_m = _mlstm(
        ml_in, _prev_rows(st["ml_conv"]), gates, w["gbias"], w["ml_conv_w"], w["ml_conv_b"],
        w["ml_wq"], w["ml_wk"], st["ml_c"], st["ml_n"], st["ml_m"][:, :, None],
        w["ml_norm_g"], w["ml_skip"], l, l_true)
    new_st = dict(ret_s=ret_s, ml_c=ml_c, ml_n=ml_n, ml_m=ml_m[:, :, 0])
    return ro[:, :t], mo[:, :t], new_st


def _layer(l, lam_init, xp, xs, mods_p, mods_s, st_p, st_s, w, cache, rope_p, rope_s, lg_tab,
           kv_stacks):
    bp, tp, d = xp.shape
    bs = st_s["ret_s"].shape[0]
    ts = xs.shape[1] // bs
    sh1p, sc1p, g1p, sh2p, sc2p, g2p = mods_p
    sh1s, sc1s, g1s, sh2s, sc2s, g2s = mods_s
    depth = w["w_in_t"].shape[0]
    f2 = w["ffn_w_up"].shape[2]

    hp = _norm(xp, w["g_mix"], l, sc1p, sh1p)
    hs = _norm(xs, w["g_mix"], l, sc1s, sh1s)

    def in_proj(col0, width, tn, name, stack=None):
        assert col0 % tn == 0 and width % tn == 0
        return _matmul([hp], w["w_in_t"], l, width // tn, tn, 1024, name=name, w_transposed=True,
                       w_tile=lambda n: n + col0 // tn, stack=stack, side=([hs], None, None))

    q_p, q_s = in_proj(W_COL_Q, DA_W, 1024, "q_proj")
    k_stack, k_s = in_proj(W_COL_K, DA_W, 1024, "k_proj", stack=(depth, kv_stacks[0]))
    v_stack, v_s = in_proj(W_COL_V, DA_W, 1024, "v_proj", stack=(depth, kv_stacks[1]))
    ret_p, ret_s = in_proj(W_COL_RET, 4 * RET_W, 1024, "ret_proj")
    ml_p, ml_s = in_proj(W_COL_ML, 3 * ML_W, 512, "ml_proj")
    gates_p = _gates(hp, w["w_in_t"], l)
    gates_s = _gates(hs, w["w_in_t"], l).reshape(GATE_ROWS, bs, ts).transpose(1, 0, 2)
    ml_s = ml_s.reshape(bs, ts, 3 * ML_W)

    oa_p = _attn_prompt(q_p, k_stack, v_stack, w["lam_p"], w["da_norm_g"], l, lam_init)
    oa_s = _attn_decode(q_s.reshape(bs, ts, DA_W), k_s, v_s, *cache, w["lam_p"], w["da_norm_g"],
                        l, lam_init)
    ro_p, mo_p, new_p = _mixers(l, bp, tp, ret_p, ml_p, gates_p, st_p, w, rope_p, lg_tab)
    ro_s, mo_s, new_s = _mixers(l, bs, ts, ret_s.reshape(bs, ts, 4 * RET_W), ml_s, gates_s,
                                st_s, w, rope_s, lg_tab)
    mix_s = [a.astype(BF16).reshape(1, bs * ts, a.shape[-1]) for a in (oa_s, ro_s, mo_s)]
    xp, xs = _matmul([oa_p, ro_p, mo_p], w["w_out"], l, d // _tile(d, 1024), _tile(d, 1024), 1024,
                     res=xp, gate=g1p, side=(mix_s, xs, g1s), name="out_proj")

    h2p = _norm(xp, w["g_ffn"], l, sc2p, sh2p)
    h2s = _norm(xs, w["g_ffn"], l, sc2s, sh2s)
    act_p, tail_g, tail_v, up_g_s, up_v_s = _ffn_up_fused(
        h2p, w["ffn_w_up"], _prev_rows(st_p["ffn_conv"]), w["ffn_conv_w"], w["ffn_conv_b"], l, h2s)
    up_s = jnp.concatenate([up_g_s, up_v_s], axis=-1).reshape(bs, ts, f2)
    act_s = _conv_gate(up_s, _prev_rows(st_s["ffn_conv"]), w["ffn_conv_w"], w["ffn_conv_b"], l)
    xp, xs = _matmul([act_p], w["ffn_w_down"], l, d // _tile(d, 512), _tile(d, 512), 512,
                     res=xp, gate=g2p, side=([act_s.reshape(1, bs * ts, f2 // 2)], xs, g2s),
                     name="ffn_down")

    new_p.update(
        ml_conv=ml_p[:, tp - (ML_CONV - 1):, COL_MU:COL_MU + ML_W],
        ffn_conv=jnp.concatenate([tail_g, tail_v], axis=-1)[:, SUBLANES - (FFN_CONV - 1):, :])
    new_s.update(ml_conv=ml_s[:, ts - (ML_CONV - 1):, COL_MU:COL_MU + ML_W],
                 ffn_conv=up_s[:, ts - (FFN_CONV - 1):, :])
    return xp, xs, (k_stack, v_stack), (k_s, v_s), new_p, new_s


def kernel(x_prompt, x_sample, c_prompt, c_sample, cache_k, cache_v, page_table, state_ret, state_ml_c, state_ml_n, state_ml_m, state_ml_conv, state_ffn_conv, w_ada, b_ada, g_mix, w_in, lam_q1, lam_k1, lam_q2, lam_k2, da_norm_g, ret_norm_g, ml_conv_w, ml_conv_b, ml_wq, ml_wk, ml_b_i, ml_b_f, ml_norm_g, ml_skip, w_out, g_ffn, ffn_w_up, ffn_conv_w, ffn_conv_b, ffn_w_down, g_final):
    bp, tp, d = x_prompt.shape
    bs, ts, _ = x_sample.shape
    depth = w_in.shape[0]
    f2 = ffn_w_up.shape[2]
    past = page_table.shape[1] * PAGE

    def per_layer_rows(a):
        return a.reshape(depth, 1, a.shape[-1])

    zeros8 = jnp.zeros((depth, GATE_ROWS - N_GATES), F32)
    w = dict(
        g_mix=g_mix, g_ffn=g_ffn, w_out=w_out, ffn_w_up=ffn_w_up, ffn_w_down=ffn_w_down,
        w_in_t=jnp.swapaxes(w_in, 1, 2),
        gbias=jnp.concatenate([ml_b_i, ml_b_f, zeros8], axis=1)[:, :, None],
        lam_p=jnp.stack([lam_q1, lam_k1, lam_q2, lam_k2], axis=1),
        da_norm_g=per_layer_rows(da_norm_g), ret_norm_g=per_layer_rows(ret_norm_g),
        ml_conv_w=ml_conv_w, ml_conv_b=per_layer_rows(ml_conv_b), ml_wq=ml_wq, ml_wk=ml_wk,
        ml_norm_g=per_layer_rows(ml_norm_g), ml_skip=per_layer_rows(ml_skip),
        ffn_conv_w=ffn_conv_w, ffn_conv_b=per_layer_rows(ffn_conv_b))

    n_c = bp + bs
    c_rows = -(-n_c // SUBLANES) * SUBLANES
    c_all = jnp.pad(jnp.concatenate([c_prompt, c_sample], axis=0), ((0, c_rows - n_c), (0, 0)))
    mod = _ada(c_all, w_ada, b_ada)

    lg_np = np.log(1.0 - 2.0 ** (-5.0 - np.arange(N_RET, dtype=np.float64)))
    lg_tab = jnp.asarray(np.broadcast_to(lg_np[:, None, None], (N_RET, 1, HEAD_DIM)), F32)
    tp_pad = -(-tp // CHUNK) * CHUNK
    ts_pad = -(-ts // CHUNK) * CHUNK
    rope_p = _rope_tables(jnp.arange(tp), tp_pad)
    rope_s = _rope_tables(past + jnp.arange(ts), ts_pad)

    zero_st = dict(
        ret_s=jnp.zeros((bp, N_RET, HEAD_DIM, HEAD_DIM), F32),
        ml_c=jnp.zeros((bp, N_ML, HEAD_DIM, HEAD_DIM), F32),
        ml_n=jnp.zeros((bp, N_ML, HEAD_DIM), F32),
        ml_m=jnp.zeros((bp, N_ML), F32),
        ml_conv=jnp.zeros((bp, ML_CONV - 1, ML_W), F32),
        ffn_conv=jnp.zeros((bp, FFN_CONV - 1, f2), F32))
    names = ("ret_s", "ml_c", "ml_n", "ml_m", "ml_conv", "ffn_conv")
    out_p = {n: [] for n in names}
    out_s = {n: [] for n in names}
    kv_p = (None, None)
    k_s, v_s = [], []
    yp = x_prompt
    ys = x_sample.reshape(1, bs * ts, d)
    for l in range(depth):
        lam_init = 0.8 - 0.6 * math.exp(-0.3 * l)
        parts = jnp.split(mod[l], 6, axis=-1)
        mods_p = [m[:bp, None, :] for m in parts]
        mods_s = [jnp.repeat(m[bp:n_c], ts, axis=0)[None] for m in parts]
        st_in = dict(ret_s=state_ret[l], ml_c=state_ml_c[l], ml_n=state_ml_n[l], ml_m=state_ml_m[l],
                     ml_conv=state_ml_conv[l], ffn_conv=state_ffn_conv[l])
        yp, ys, kv_p, kv_s, st_p, st_s = _layer(
            l, lam_init, yp, ys, mods_p, mods_s, zero_st, st_in, w,
            (cache_k, cache_v, page_table), rope_p, rope_s, lg_tab, kv_p)
        k_s.append(kv_s[0])
        v_s.append(kv_s[1])
        for n in names:
            out_p[n].append(st_p[n])
            out_s[n].append(st_s[n])

    y_prompt = _norm(yp, g_final, None, out_dtype=F32)
    y_sample = _norm(ys, g_final, None, out_dtype=F32).reshape(bs, ts, d)
    kv_shape_p = (depth, bp, tp, N_DA, HEAD_DIM)
    kv_shape_s = (depth, bs, ts, N_DA, HEAD_DIM)
    return (y_prompt, y_sample,
            kv_p[0].reshape(kv_shape_p), kv_p[1].reshape(kv_shape_p),
            jnp.stack(k_s).reshape(kv_shape_s), jnp.stack(v_s).reshape(kv_shape_s),
            jnp.stack(out_p["ret_s"]), jnp.stack(out_s["ret_s"]),
            jnp.stack(out_p["ml_c"]), jnp.stack(out_s["ml_c"]),
            jnp.stack(out_p["ml_n"]), jnp.stack(out_s["ml_n"]),
            jnp.stack(out_p["ml_m"]), jnp.stack(out_s["ml_m"]),
            jnp.stack(out_p["ml_conv"]), jnp.stack(out_s["ml_conv"]),
            jnp.stack(out_p["ffn_conv"]), jnp.stack(out_s["ffn_conv"]))
```

```python
import functools
import math

import numpy as np
import jax
import jax.numpy as jnp
from jax import lax
from jax.experimental import pallas as pl
from jax.experimental.pallas import tpu as pltpu

F32 = jnp.float32
BF16 = jnp.bfloat16

HEAD_DIM = 128
DA_QK = HEAD_DIM // 2
N_DA = 8
N_RET = 4
N_ML = 4
DA_W = N_DA * HEAD_DIM
RET_W = N_RET * HEAD_DIM
ML_W = N_ML * HEAD_DIM
ML_CONV = 4
FFN_CONV = 3
CHUNK = 128
PAGE = 128
RMS_EPS = 1e-6
ROPE_BASE = 10000.0
NEG_INF = -1e30

W_COL_Q, W_COL_K, W_COL_V = 0, DA_W, 2 * DA_W
W_COL_RET = 3 * DA_W
W_COL_ML = W_COL_RET + 4 * RET_W
W_COL_GATES = W_COL_ML + 3 * ML_W
COL_RQ, COL_RK, COL_RV, COL_RG = 0, RET_W, 2 * RET_W, 3 * RET_W
COL_MU, COL_MV, COL_MO = 0, ML_W, 2 * ML_W
N_GATES = 2 * N_ML
GATE_ROWS = 16

SUBLANES = 8
LANES = 128
VMEM_LIMIT = 56 * 1024 * 1024
FFN_SUB = 4
CHUNK_UNROLL = 16

NT_DIMS = (((1,), (1,)), ((), ()))
TN_DIMS = (((0,), (0,)), ((), ()))


def _tile(n, pref, mult=LANES):
    if n <= pref:
        return n
    t = (pref // mult) * mult
    while t >= mult:
        if n % t == 0:
            return t
        t -= mult
    return n


def _params(sem):
    return pltpu.CompilerParams(dimension_semantics=sem, vmem_limit_bytes=VMEM_LIMIT)


def _rms(x):
    return x * lax.rsqrt(jnp.mean(x * x, axis=-1, keepdims=True) + RMS_EPS)


def _ada_kernel(c_ref, w_ref, b_ref, o_ref):
    c = c_ref[...]
    a = (c * jax.nn.sigmoid(c)).astype(BF16)
    o_ref[...] = jnp.dot(a, w_ref[...].astype(BF16), preferred_element_type=F32) + b_ref[...]


def _ada(c_all, w_ada, b_ada):
    depth, d, n = w_ada.shape
    rows = c_all.shape[0]
    tn = _tile(n, 1024)
    return pl.pallas_call(
        _ada_kernel,
        grid=(depth, n // tn),
        in_specs=[
            pl.BlockSpec((rows, d), lambda l, j: (0, 0)),
            pl.BlockSpec((None, d, tn), lambda l, j: (l, 0, j)),
            pl.BlockSpec((None, 1, tn), lambda l, j: (l, 0, j)),
        ],
        out_specs=pl.BlockSpec((None, rows, tn), lambda l, j: (l, 0, j)),
        out_shape=jax.ShapeDtypeStruct((depth, rows, n), F32),
        compiler_params=_params(("arbitrary", "arbitrary")),
        name="ada",
    )(c_all, w_ada, b_ada.reshape(depth, 1, n))


def _norm_kernel(*refs, modulated):
    if modulated:
        x_ref, g_ref, sc_ref, sh_ref, o_ref = refs
    else:
        x_ref, g_ref, o_ref = refs
    y = _rms(x_ref[...]) * g_ref[...]
    if modulated:
        y = y * (1.0 + sc_ref[...]) + sh_ref[...]
    o_ref[...] = y.astype(o_ref.dtype)


def _mod_spec(mod, tm, tn, with_n):
    per_row = mod.shape[1] != 1
    rows = tm if per_row else 1
    if with_n:
        return pl.BlockSpec((None, rows, tn), lambda n, b, t: (b, t if per_row else 0, n))
    return pl.BlockSpec((None, rows, tn), lambda b, t: (b, t if per_row else 0, 0))


def _norm(x, g, l, sc=None, sh=None, out_dtype=BF16):
    bm, tmx, d = x.shape
    tm = _tile(tmx, 512, SUBLANES)
    modulated = sc is not None
    g_spec = (pl.BlockSpec((None, 1, d), lambda b, t: (l, 0, 0)) if l is not None
              else pl.BlockSpec((1, d), lambda b, t: (0, 0)))
    in_specs = [pl.BlockSpec((None, tm, d), lambda b, t: (b, t, 0)), g_spec]
    args = [x, g.reshape(g.shape[0], 1, d) if l is not None else g.reshape(1, d)]
    if modulated:
        in_specs += [_mod_spec(sc, tm, d, False), _mod_spec(sh, tm, d, False)]
        args += [sc, sh]
    return pl.pallas_call(
        functools.partial(_norm_kernel, modulated=modulated),
        grid=(bm, tmx // tm),
        in_specs=in_specs,
        out_specs=pl.BlockSpec((None, tm, d), lambda b, t: (b, t, 0)),
        out_shape=jax.ShapeDtypeStruct((bm, tmx, d), out_dtype),
        compiler_params=_params(("arbitrary", "arbitrary")),
        name="norm",
    )(*args)


def _mm_kernel(*refs, row_offs, row_sizes, has_res, has_side, w_transposed):
    n_in = len(row_offs)
    it = iter(refs)
    main = [[next(it) for _ in range(n_in)]]
    w_ref = next(it)
    main += [next(it), next(it)] if has_res else [None, None]
    if has_side:
        side = [[next(it) for _ in range(n_in)]]
        side += [next(it), next(it)] if has_res else [None, None]
    wbf_ref = refs[-1]
    o_ref, side_o_ref = (refs[-3], refs[-2]) if has_side else (refs[-2], None)

    def project(x_refs, res_ref, gate_ref, out_ref):
        acc = None
        for x_ref, off, size in zip(x_refs, row_offs, row_sizes):
            if w_transposed:
                part = lax.dot_general(x_ref[...], wbf_ref[:, off:off + size], NT_DIMS,
                                       preferred_element_type=F32)
            else:
                part = jnp.dot(x_ref[...], wbf_ref[off:off + size, :], preferred_element_type=F32)
            acc = part if acc is None else acc + part
        if res_ref is not None:
            acc = res_ref[...] + gate_ref[...] * acc
        out_ref[...] = acc.astype(out_ref.dtype)

    @pl.when(jnp.logical_and(pl.program_id(1) == 0, pl.program_id(2) == 0))
    def _():
        wbf_ref[...] = w_ref[...].astype(BF16)
        if has_side:
            project(*side, side_o_ref)

    project(*main, o_ref)


def _matmul(xs, w, l, n_tiles, tn, tm_pref, res=None, gate=None, w_tile=lambda n: n,
            w_transposed=False, stack=None, side=None, name="mm"):
    bm, tmx = xs[0].shape[:2]
    k = w.shape[2] if w_transposed else w.shape[1]
    sizes = [x.shape[2] for x in xs]
    offs = [sum(sizes[:i]) for i in range(len(xs))]
    assert sum(sizes) == k
    tm = _tile(tmx, tm_pref, SUBLANES)
    in_specs = [pl.BlockSpec((None, tm, s), lambda n, b, t: (b, t, 0)) for s in sizes]
    if w_transposed:
        in_specs.append(pl.BlockSpec((None, tn, k), lambda n, b, t: (l, w_tile(n), 0)))
    else:
        in_specs.append(pl.BlockSpec((None, k, tn), lambda n, b, t: (l, 0, w_tile(n))))
    args = list(xs) + [w]
    if res is not None:
        in_specs.append(pl.BlockSpec((None, tm, tn), lambda n, b, t: (b, t, n)))
        in_specs.append(_mod_spec(gate, tm, tn, True))
        args += [res, gate]
    out_specs = [pl.BlockSpec((None, tm, tn), lambda n, b, t: (b, t, n))]
    out_shape = [jax.ShapeDtypeStruct((bm, tmx, n_tiles * tn), F32)]
    if side is not None:
        side_xs, side_res, side_gate = side
        ts = side_xs[0].shape[1]
        side_cols = pl.BlockSpec((None, ts, tn), lambda n, b, t: (0, 0, n))
        in_specs += [pl.BlockSpec((None, ts, s), lambda n, b, t: (0, 0, 0)) for s in sizes]
        args += list(side_xs)
        if res is not None:
            in_specs += [side_cols, side_cols]
            args += [side_res, side_gate]
        out_specs.append(side_cols)
        out_shape.append(jax.ShapeDtypeStruct((1, ts, n_tiles * tn), F32))
    aliases = {}
    if stack is not None:
        depth, buf = stack
        out_specs[0] = pl.BlockSpec((None, None, tm, tn), lambda n, b, t: (l, b, t, n))
        out_shape[0] = jax.ShapeDtypeStruct((depth, bm, tmx, n_tiles * tn), F32)
        if buf is not None:
            in_specs.append(pl.BlockSpec(memory_space=pl.ANY))
            args.append(buf)
            aliases = {len(args) - 1: 0}
    outs = pl.pallas_call(
        functools.partial(_mm_kernel, row_offs=tuple(offs), row_sizes=tuple(sizes),
                          has_res=res is not None, has_side=side is not None,
                          w_transposed=w_transposed),
        grid=(n_tiles, bm, tmx // tm),
        in_specs=in_specs,
        out_specs=out_specs,
        out_shape=out_shape,
        input_output_aliases=aliases,
        scratch_shapes=[pltpu.VMEM((tn, k) if w_transposed else (k, tn), BF16)],
        compiler_params=_params(("arbitrary", "arbitrary", "arbitrary")),
        name=name,
    )(*args)
    return outs if side is not None else outs[0]


def _gates_kernel(h_ref, wg_ref, o_ref):
    row = lax.broadcasted_iota(jnp.int32, wg_ref.shape, 0)
    wg = jnp.where(row < N_GATES, wg_ref[...], 0.0).astype(BF16)
    o_ref[...] = lax.dot_general(wg, h_ref[...], NT_DIMS, preferred_element_type=F32)


def _gates(h, w_in_t, l):
    bm, tmx, d = h.shape
    tm = _tile(tmx, 1024)
    assert W_COL_GATES % GATE_ROWS == 0 and w_in_t.shape[1] == W_COL_GATES + N_GATES
    return pl.pallas_call(
        _gates_kernel,
        grid=(bm, tmx // tm),
        in_specs=[pl.BlockSpec((None, tm, d), lambda b, t: (b, t, 0)),
                  pl.BlockSpec((None, GATE_ROWS, d), lambda b, t: (l, W_COL_GATES // GATE_ROWS, 0))],
        out_specs=pl.BlockSpec((None, GATE_ROWS, tm), lambda b, t: (b, 0, t)),
        out_shape=jax.ShapeDtypeStruct((bm, GATE_ROWS, tmx), F32),
        compiler_params=_params(("arbitrary", "arbitrary")),
        name="gates",
    )(h, w_in_t)


def _lambda(lam_ref, lam_init):
    lp = lam_ref[...]
    a = jnp.exp(jnp.sum(lp[0:1] * lp[1:2], axis=1, keepdims=True))
    b = jnp.exp(jnp.sum(lp[2:3] * lp[3:4], axis=1, keepdims=True))
    return a - b + lam_init


SCORE_SCALE_LOG2 = DA_QK ** -0.5 * math.log2(math.e)


def _online_softmax_step(s, v_bf, m_scr, l_scr, acc_scr):
    m_prev = m_scr[...]
    m_new = jnp.maximum(m_prev, jnp.max(s, axis=1, keepdims=True))
    alpha = jnp.exp2(m_prev - m_new)
    p = jnp.exp2(s - m_new)
    l_scr[...] = alpha * l_scr[...] + jnp.sum(p, axis=1, keepdims=True)
    acc_scr[...] = alpha * acc_scr[...] + jnp.dot(p.astype(BF16), v_bf, preferred_element_type=F32)
    m_scr[...] = m_new


def _attn_prompt_block(qi, q_ref, kbf, vbf, lam, g_ref, o_ref, *, tq, lam_init):
    n0 = qi * tq
    q = q_ref[n0:n0 + tq, :] * SCORE_SCALE_LOG2
    lane = lax.broadcasted_iota(jnp.int32, q.shape, 1)
    row = lax.broadcasted_iota(jnp.int32, (tq, tq), 0)
    col = lax.broadcasted_iota(jnp.int32, (tq, tq), 1)
    outs = []
    for first_map in (True, False):
        qm = jnp.where((lane < DA_QK) if first_map else (lane >= DA_QK), q, 0.0).astype(BF16)
        s_diag = lax.dot_general(qm, kbf[n0:n0 + tq, :], NT_DIMS, preferred_element_type=F32)
        s_diag = jnp.where(col <= row, s_diag, NEG_INF)
        m = jnp.max(s_diag, axis=1, keepdims=True)
        if qi > 0:
            s_past = lax.dot_general(qm, kbf[0:n0, :], NT_DIMS, preferred_element_type=F32)
            m = jnp.maximum(m, jnp.max(s_past, axis=1, keepdims=True))
        p_diag = jnp.exp2(s_diag - m)
        o = jnp.dot(p_diag.astype(BF16), vbf[n0:n0 + tq, :], preferred_element_type=F32)
        if qi > 0:
            p_past = jnp.exp2(s_past - m)
            o = o + jnp.dot(p_past.astype(BF16), vbf[0:n0, :], preferred_element_type=F32)
        outs.append(o[:, 0:HEAD_DIM] / o[:, HEAD_DIM:HEAD_DIM + 1])
    oa = outs[0] - lam * outs[1]
    o_ref[n0:n0 + tq, :] = (_rms(oa) * g_ref[...] * (1.0 - lam_init)).astype(o_ref.dtype)


def _attn_prompt_kernel(q_ref, k_ref, v_ref, lam_ref, g_ref, o_ref, kbf, vbf, *, tq, n_q, lam_init):
    kbf[...] = k_ref[...].astype(BF16)
    vbf[:, 0:HEAD_DIM] = v_ref[...].astype(BF16)
    vbf[:, HEAD_DIM:2 * HEAD_DIM] = jnp.ones(v_ref.shape, BF16)
    lam = _lambda(lam_ref, lam_init)
    for qi in range(n_q):
        _attn_prompt_block(qi, q_ref, kbf, vbf, lam, g_ref, o_ref, tq=tq, lam_init=lam_init)


def _attn_prompt(q_new, k_stack, v_stack, lam_p, da_g, l, lam_init):
    b, t, _ = q_new.shape
    tq = _tile(t, 256)
    kv_spec = pl.BlockSpec((None, None, t, HEAD_DIM), lambda b, h: (l, b, 0, h))
    return pl.pallas_call(
        functools.partial(_attn_prompt_kernel, tq=tq, n_q=t // tq, lam_init=lam_init),
        grid=(b, N_DA),
        in_specs=[
            pl.BlockSpec((None, t, HEAD_DIM), lambda b, h: (b, 0, h)),
            kv_spec, kv_spec,
            pl.BlockSpec((None, 4, DA_QK), lambda b, h: (l, 0, 0)),
            pl.BlockSpec((None, 1, HEAD_DIM), lambda b, h: (l, 0, h)),
        ],
        out_specs=pl.BlockSpec((None, t, HEAD_DIM), lambda b, h: (b, 0, h)),
        out_shape=jax.ShapeDtypeStruct((b, t, DA_W), BF16),
        scratch_shapes=[pltpu.VMEM((t, HEAD_DIM), BF16), pltpu.VMEM((t, 2 * HEAD_DIM), BF16)],
        compiler_params=_params(("arbitrary", "arbitrary")),
        name="attn_prompt",
    )(q_new, k_stack, v_stack, lam_p, da_g)


DEC_ROWS = N_DA * 2 * SUBLANES
PAGE_ROWS = PAGE * N_DA
LOG2_HEADS = N_DA.bit_length() - 1
LOG2_HEAD_ROWS = (2 * SUBLANES).bit_length() - 1


def _attn_decode_kernel(pt_ref, q_ref, kn_ref, vn_ref, lam_ref, g_ref, *rest,
                        pages, n_steps, t_new, lam_init):
    k_refs = rest[:pages]
    v_refs = rest[pages:2 * pages]
    o_ref, qm, bias, m_scr, l_scr, acc_scr, kpad, vpad = rest[2 * pages:]
    step = pl.program_id(1)

    @pl.when(step == 0)
    def _():
        q = q_ref[...] * SCORE_SCALE_LOG2
        lane = lax.broadcasted_iota(jnp.int32, (SUBLANES, HEAD_DIM), 1)
        for h in range(N_DA):
            qh = q[:, h * HEAD_DIM:(h + 1) * HEAD_DIM]
            r0 = h * 2 * SUBLANES
            qm[r0:r0 + SUBLANES, :] = jnp.where(lane < DA_QK, qh, 0.0)
            qm[r0 + SUBLANES:r0 + 2 * SUBLANES, :] = jnp.where(lane >= DA_QK, qh, 0.0)
        row = lax.broadcasted_iota(jnp.int32, bias.shape, 0)
        col = lax.broadcasted_iota(jnp.int32, bias.shape, 1)
        bias[...] = jnp.where((col & (N_DA - 1)) == (row >> LOG2_HEAD_ROWS), 0.0, NEG_INF)
        m_scr[...] = jnp.full(m_scr.shape, -jnp.inf, F32)
        l_scr[...] = jnp.zeros(l_scr.shape, F32)
        acc_scr[...] = jnp.zeros(acc_scr.shape, F32)

    qb = qm[...].astype(BF16)
    head_bias = bias[...]
    group = 2 if pages % 2 == 0 else 1

    def scores_of(g):
        return [lax.dot_general(qb, k_refs[p][...].astype(BF16), NT_DIMS,
                                preferred_element_type=F32) + head_bias
                for p in range(g * group, (g + 1) * group)]

    m_run, l_run, acc = m_scr[...], l_scr[...], acc_scr[...]
    upcoming = scores_of(0)
    for g in range(pages // group):
        scores = upcoming
        if (g + 1) * group < pages:
            upcoming = scores_of(g + 1)
        m_new = m_run
        for s in scores:
            m_new = jnp.maximum(m_new, jnp.max(s, axis=1, keepdims=True))
        alpha = jnp.exp2(m_run - m_new)
        l_run = alpha * l_run
        pv = None
        for i, s in enumerate(scores):
            e = jnp.exp2(s - m_new)
            l_run = l_run + jnp.sum(e, axis=1, keepdims=True)
            d = jnp.dot(e.astype(BF16), v_refs[g * group + i][...].astype(BF16),
                        preferred_element_type=F32)
            pv = d if pv is None else pv + d
        acc = alpha * acc + pv
        m_run = m_new
    m_scr[...] = m_run
    l_scr[...] = l_run
    acc_scr[...] = acc

    @pl.when(step == n_steps - 1)
    def _():
        kpad[...] = jnp.zeros(kpad.shape, F32)
        vpad[...] = jnp.zeros(vpad.shape, F32)
        kpad[0:t_new * N_DA, :] = kn_ref[...]
        vpad[0:t_new * N_DA, :] = vn_ref[...]
        s = lax.dot_general(qb, kpad[...].astype(BF16), NT_DIMS, preferred_element_type=F32)
        row = lax.broadcasted_iota(jnp.int32, s.shape, 0)
        col = lax.broadcasted_iota(jnp.int32, s.shape, 1)
        causal = (col >> LOG2_HEADS) <= (row & (SUBLANES - 1))
        s = jnp.where(causal, s + head_bias, NEG_INF)
        _online_softmax_step(s, vpad[...].astype(BF16), m_scr, l_scr, acc_scr)

        lam = _lambda(lam_ref, lam_init)
        o = acc_scr[...] / l_scr[...]
        for h in range(N_DA):
            r0 = h * 2 * SUBLANES
            cols = slice(h * HEAD_DIM, (h + 1) * HEAD_DIM)
            oa = o[r0:r0 + SUBLANES] - lam * o[r0 + SUBLANES:r0 + 2 * SUBLANES]
            o_ref[:, cols] = _rms(oa) * g_ref[:, cols] * (1.0 - lam_init)


def _attn_decode(q_new, k_new, v_new, cache_k, cache_v, page_table, lam_p, da_g, l, lam_init):
    b, t_new, _ = q_new.shape
    assert t_new == SUBLANES and cache_k.shape[2:] == (PAGE, N_DA, HEAD_DIM)
    n_pages = page_table.shape[1]
    pages = _tile(n_pages, 8, 1)
    n_steps = n_pages // pages
    n_pool = cache_k.shape[1]
    ck = cache_k.reshape(cache_k.shape[0], n_pool, PAGE_ROWS, HEAD_DIM)
    cv = cache_v.reshape(cache_v.shape[0], n_pool, PAGE_ROWS, HEAD_DIM)
    kn = k_new.reshape(b, t_new * N_DA, HEAD_DIM)
    vn = v_new.reshape(b, t_new * N_DA, HEAD_DIM)

    def page_spec(p):
        return pl.BlockSpec((None, None, PAGE_ROWS, HEAD_DIM),
                            lambda b, s, pt: (l, pt[b * n_pages + s * pages + p], 0, 0))

    new_spec = pl.BlockSpec((None, t_new * N_DA, HEAD_DIM), lambda b, s, pt: (b, 0, 0))
    grid_spec = pltpu.PrefetchScalarGridSpec(
        num_scalar_prefetch=1,
        grid=(b, n_steps),
        in_specs=[
            pl.BlockSpec((None, t_new, DA_W), lambda b, s, pt: (b, 0, 0)),
            new_spec, new_spec,
            pl.BlockSpec((None, 4, DA_QK), lambda b, s, pt: (l, 0, 0)),
            pl.BlockSpec((None, 1, DA_W), lambda b, s, pt: (l, 0, 0)),
        ] + [page_spec(p) for p in range(pages)] * 2,
        out_specs=pl.BlockSpec((None, t_new, DA_W), lambda b, s, pt: (b, 0, 0)),
        scratch_shapes=[
            pltpu.VMEM((DEC_ROWS, HEAD_DIM), F32),
            pltpu.VMEM((DEC_ROWS, PAGE_ROWS), F32),
            pltpu.VMEM((DEC_ROWS, 1), F32), pltpu.VMEM((DEC_ROWS, 1), F32),
            pltpu.VMEM((DEC_ROWS, HEAD_DIM), F32),
            pltpu.VMEM((PAGE_ROWS, HEAD_DIM), F32), pltpu.VMEM((PAGE_ROWS, HEAD_DIM), F32),
        ],
    )
    return pl.pallas_call(
        functools.partial(_attn_decode_kernel, pages=pages, n_steps=n_steps, t_new=t_new,
                          lam_init=lam_init),
        grid_spec=grid_spec,
        out_shape=jax.ShapeDtypeStruct((b, t_new, DA_W), F32),
        compiler_params=_params(("arbitrary", "arbitrary")),
        name="attn_decode",
    )(page_table.reshape(-1), q_new, kn, vn, lam_p, da_g, *([ck] * pages), *([cv] * pages))


def _rotary(x, cos2, sin2):
    return x * cos2 + pltpu.roll(x, DA_QK, 1) * sin2


def _cumsum_lanes(x, lane):
    n = x.shape[1]
    shift = 1
    while shift < n:
        x = x + jnp.where(lane >= shift, pltpu.roll(x, shift, 1), 0.0)
        shift *= 2
    return x


def _heads(x, n):
    return [x[:, h * HEAD_DIM:(h + 1) * HEAD_DIM] for h in range(n)]


def _block(x, h):
    return x[h * CHUNK:(h + 1) * CHUNK]


def _stack(xs):
    return jnp.concatenate(xs, axis=0)


def _per_head_col(v):
    return _stack([jnp.broadcast_to(v[h:h + 1, :], (CHUNK, 1)) for h in range(v.shape[0])])


def _chunk_rows(c):
    return pl.ds(c * CHUNK if isinstance(c, int) else pl.multiple_of(c * CHUNK, CHUNK), CHUNK)


def _chunk_loop(body, n_chunks):
    if n_chunks <= CHUNK_UNROLL:
        for c in range(n_chunks):
            body(c)
    else:
        lax.fori_loop(0, n_chunks, lambda c, carry: (body(c), carry)[1], 0, unroll=CHUNK_UNROLL)


def _retention_kernel(q_ref, k_ref, v_ref, gate_ref, cos_ref, sin_ref, lg_ref, s0_ref, gn_ref,
                      o_ref, sout_ref, s_scr, *, n_chunks, l_true):
    L, H = CHUNK, N_RET
    lg = lg_ref[...][:, 0:1]
    lg_col = _per_head_col(lg)
    ti = lax.broadcasted_iota(jnp.int32, (H * L, L), 0) & (L - 1)
    si = lax.broadcasted_iota(jnp.int32, (H * L, L), 1)
    rel = (ti - si).astype(F32)
    dmat = jnp.where(rel >= 0, jnp.exp(lg_col * jnp.maximum(rel, 0.0)), 0.0)
    idx = (lax.broadcasted_iota(jnp.int32, (H * L, 1), 0) & (L - 1)).astype(F32)
    q_dec = jnp.exp(lg_col * (idx + 1.0))
    k_dec = jnp.exp(lg_col * (l_true - 1.0 - idx))
    s_dec = jnp.exp(lg * float(l_true))
    s_scr[...] = s0_ref[...]

    def body(c):
        rows = _chunk_rows(c)
        cos2, sin2 = cos_ref[rows, :], sin_ref[rows, :]
        q_h = [_rotary(x, cos2, sin2) for x in _heads(q_ref[rows, :], H)]
        k_h = [_rotary(x, cos2, sin2) * (HEAD_DIM ** -0.5) for x in _heads(k_ref[rows, :], H)]
        q_bf = [x.astype(BF16) for x in q_h]
        k_bf = [x.astype(BF16) for x in k_h]
        v_bf = [x.astype(BF16) for x in _heads(v_ref[rows, :], H)]
        state = [s_scr[h] for h in range(H)]
        scores = _stack([lax.dot_general(q_bf[h], k_bf[h], NT_DIMS, preferred_element_type=F32)
                         for h in range(H)]) * dmat
        scores_bf = scores.astype(BF16)
        inner = _stack([jnp.dot(_block(scores_bf, h), v_bf[h], preferred_element_type=F32)
                        for h in range(H)])
        cross = _stack([jnp.dot(q_bf[h], state[h].astype(BF16), preferred_element_type=F32)
                        for h in range(H)]) * q_dec
        kd = _stack(k_h) * k_dec
        for h in range(H):
            s_scr[h] = state[h] * s_dec[h:h + 1, :] + jnp.dot(
                _block(kd, h).T.astype(BF16), v_bf[h], preferred_element_type=F32)
        y = _rms(inner + cross)
        g_h = _heads(gate_ref[rows, :], H)
        gn_h = _heads(gn_ref[...], H)
        for h in range(H):
            o_ref[rows, h * HEAD_DIM:(h + 1) * HEAD_DIM] = (
                (_block(y, h) * gn_h[h]) * (g_h[h] * jax.nn.sigmoid(g_h[h]))).astype(o_ref.dtype)

    _chunk_loop(body, n_chunks)
    sout_ref[...] = s_scr[...]


def _retention(ret_in, cos2, sin2, lg_tab, s0, gn, l, l_true):
    b, t, _ = ret_in.shape

    def seq(j):
        return pl.BlockSpec((None, t, RET_W), lambda b: (b, 0, j))

    st_spec = pl.BlockSpec((None, N_RET, HEAD_DIM, HEAD_DIM), lambda b: (b, 0, 0, 0))
    return pl.pallas_call(
        functools.partial(_retention_kernel, n_chunks=t // CHUNK, l_true=l_true),
        grid=(b,),
        in_specs=[
            seq(COL_RQ // RET_W), seq(COL_RK // RET_W), seq(COL_RV // RET_W), seq(COL_RG // RET_W),
            pl.BlockSpec((t, HEAD_DIM), lambda b: (0, 0)),
            pl.BlockSpec((t, HEAD_DIM), lambda b: (0, 0)),
            pl.BlockSpec((N_RET, HEAD_DIM), lambda b: (0, 0)),
            st_spec,
            pl.BlockSpec((None, 1, RET_W), lambda b: (l, 0, 0)),
        ],
        out_specs=[pl.BlockSpec((None, t, RET_W), lambda b: (b, 0, 0)), st_spec],
        out_shape=[jax.ShapeDtypeStruct((b, t, RET_W), BF16),
                   jax.ShapeDtypeStruct((b, N_RET, HEAD_DIM, HEAD_DIM), F32)],
        scratch_shapes=[pltpu.VMEM((N_RET, HEAD_DIM, HEAD_DIM), F32)],
        compiler_params=_params(("arbitrary",)),
        name="retention",
    )(ret_in, ret_in, ret_in, ret_in, cos2, sin2, lg_tab, s0, gn)


def _mlstm_kernel(u_ref, v_ref, og_ref, prev_ref, g_ref, gb_ref, cw_ref, cb_ref, wq_ref, wk_ref,
                  c0_ref, n0_ref, m0_ref, gn_ref, skip_ref,
                  o_ref, cout_ref, nout_ref, mout_ref,
                  xp_scr, uc_scr, c_scr, n_scr, m_scr, *, t, n_chunks, l_true):
    L, H = CHUNK, N_ML
    halo = SUBLANES

    xp_scr[0:halo, :] = prev_ref[...]
    xp_scr[halo:halo + t, :] = u_ref[...]
    conv = cb_ref[...]
    for j in range(ML_CONV):
        start = halo - (ML_CONV - 1) + j
        conv = conv + xp_scr[start:start + t, :] * cw_ref[j:j + 1, :]
    uc_scr[...] = conv * jax.nn.sigmoid(conv)

    wq_bf = [wq_ref[h].astype(BF16) for h in range(H)]
    wk_bf = [wk_ref[h].astype(BF16) for h in range(H)]
    c_scr[...] = c0_ref[...]
    n_scr[...] = n0_ref[...]
    m_scr[...] = m0_ref[...]

    ti = lax.broadcasted_iota(jnp.int32, (H * L, L), 0) & (L - 1)
    si = lax.broadcasted_iota(jnp.int32, (H * L, L), 1)
    eye = ti == si
    causal = si <= ti
    lane = lax.broadcasted_iota(jnp.int32, (H, L), 1)

    heads = functools.partial(_heads, n=H)
    block, stack, per_head_col = _block, _stack, _per_head_col

    def rows_to_blocks(r):
        return stack([jnp.broadcast_to(r[h:h + 1, :], (L, L)) for h in range(H)])

    def rows_to_col(r):
        return jnp.sum(jnp.where(eye, rows_to_blocks(r), 0.0), axis=1, keepdims=True)

    def body(c):
        rows = _chunk_rows(c)
        uc_h = heads(uc_scr[rows, :])
        q_h = [jnp.dot(uc_h[h].astype(BF16), wq_bf[h], preferred_element_type=F32) for h in range(H)]
        k_h = [jnp.dot(uc_h[h].astype(BF16), wk_bf[h], preferred_element_type=F32)
               * (HEAD_DIM ** -0.5) for h in range(H)]
        q_bf = [x.astype(BF16) for x in q_h]
        k_bf = [x.astype(BF16) for x in k_h]
        v_bf = [x.astype(BF16) for x in heads(v_ref[rows, :])]

        gates = g_ref[:, rows] + gb_ref[...]
        ig, fg = gates[0:H], gates[H:2 * H]
        lf = -(jnp.maximum(-fg, 0.0) + jnp.log(1.0 + jnp.exp(-jnp.abs(fg))))
        b_row = _cumsum_lanes(lf, lane)
        b_col = rows_to_col(b_row)
        m_old = m_scr[...]
        n_old = n_scr[...]
        c_old = [c_scr[h] for h in range(H)]

        dlog = jnp.where(causal, b_col + rows_to_blocks(ig - b_row), -jnp.inf)
        inter = b_col + per_head_col(m_old)
        m_row = jnp.maximum(inter, jnp.max(dlog, axis=1, keepdims=True))
        w = jnp.exp(dlog - m_row)
        s_in = jnp.exp(inter - m_row)
        qk = stack([lax.dot_general(q_bf[h], k_bf[h], NT_DIMS, preferred_element_type=F32)
                    for h in range(H)])
        a = w * qk
        a_bf = a.astype(BF16)
        num = stack([jnp.dot(block(a_bf, h), v_bf[h], preferred_element_type=F32)
                     + jnp.dot(q_bf[h], c_old[h].astype(BF16), preferred_element_type=F32)
                     * block(s_in, h) for h in range(H)])
        qn = stack([jnp.sum(q_bf[h].astype(F32) * n_old[h:h + 1, :].astype(BF16).astype(F32),
                            axis=1, keepdims=True) for h in range(H)])
        den = jnp.sum(a, axis=1, keepdims=True) + qn * s_in
        hn = _rms(num / jnp.maximum(jnp.abs(den), jnp.exp(-m_row)))

        og_h = heads(og_ref[rows, :])
        gn_h, skip_h = heads(gn_ref[...]), heads(skip_ref[...])
        for h in range(H):
            y = block(hn, h) * gn_h[h] + skip_h[h] * uc_h[h]
            o_ref[rows, h * HEAD_DIM:(h + 1) * HEAD_DIM] = (
                jax.nn.sigmoid(og_h[h]) * y).astype(o_ref.dtype)

        b_last = jnp.sum(jnp.where(lane == l_true - 1, b_row, 0.0), axis=1, keepdims=True)
        dlast = jnp.where(lane < l_true, b_last - b_row + ig, -jnp.inf)
        m_new = jnp.maximum(b_last + m_old, jnp.max(dlast, axis=1, keepdims=True))
        ws_col = rows_to_col(jnp.exp(dlast - m_new))
        s_old = jnp.exp(b_last + m_old - m_new)
        kw = stack(k_h) * ws_col
        for h in range(H):
            c_scr[h] = c_old[h] * s_old[h:h + 1, :] + jnp.dot(
                block(kw, h).T.astype(BF16), v_bf[h], preferred_element_type=F32)
        n_scr[...] = n_old * s_old + stack(
            [jnp.sum(block(kw, h), axis=0, keepdims=True) for h in range(H)])
        m_scr[...] = m_new

    _chunk_loop(body, n_chunks)
    cout_ref[...] = c_scr[...]
    nout_ref[...] = n_scr[...]
    mout_ref[...] = m_scr[...]


def _mlstm(ml_in, prev, gates, gbias, cw, cb, wq, wk, c0, n0, m0, gn, skip, l, l_true):
    b, t, _ = ml_in.shape

    def seq(j):
        return pl.BlockSpec((None, t, ML_W), lambda b: (b, 0, j))

    def layer(rows):
        return pl.BlockSpec((None, rows, ML_W), lambda b: (l, 0, 0))

    w_spec = pl.BlockSpec((None, N_ML, HEAD_DIM, HEAD_DIM), lambda b: (l, 0, 0, 0))
    c_spec = pl.BlockSpec((None, N_ML, HEAD_DIM, HEAD_DIM), lambda b: (b, 0, 0, 0))
    n_spec = pl.BlockSpec((None, N_ML, HEAD_DIM), lambda b: (b, 0, 0))
    m_spec = pl.BlockSpec((None, N_ML, 1), lambda b: (b, 0, 0))
    return pl.pallas_call(
        functools.partial(_mlstm_kernel, t=t, n_chunks=t // CHUNK, l_true=l_true),
        grid=(b,),
        in_specs=[
            seq(COL_MU // ML_W), seq(COL_MV // ML_W), seq(COL_MO // ML_W),
            pl.BlockSpec((None, SUBLANES, ML_W), lambda b: (b, 0, 0)),
            pl.BlockSpec((None, N_GATES, t), lambda b: (b, 0, 0)),
            pl.BlockSpec((None, N_GATES, 1), lambda b: (l, 0, 0)),
            layer(ML_CONV), layer(1), w_spec, w_spec, c_spec, n_spec, m_spec, layer(1), layer(1),
        ],
        out_specs=[pl.BlockSpec((None, t, ML_W), lambda b: (b, 0, 0)), c_spec, n_spec, m_spec],
        out_shape=[jax.ShapeDtypeStruct((b, t, ML_W), BF16),
                   jax.ShapeDtypeStruct((b, N_ML, HEAD_DIM, HEAD_DIM), F32),
                   jax.ShapeDtypeStruct((b, N_ML, HEAD_DIM), F32),
                   jax.ShapeDtypeStruct((b, N_ML, 1), F32)],
        scratch_shapes=[
            pltpu.VMEM((SUBLANES + t, ML_W), F32), pltpu.VMEM((t, ML_W), F32),
            pltpu.VMEM((N_ML, HEAD_DIM, HEAD_DIM), F32), pltpu.VMEM((N_ML, HEAD_DIM), F32),
            pltpu.VMEM((N_ML, 1), F32),
        ],
        compiler_params=_params(("arbitrary",)),
        name="mlstm",
    )(ml_in, ml_in, ml_in, prev, gates, gbias, cw, cb, wq, wk, c0, n0, m0, gn, skip)


def _conv_gate_kernel(ug_ref, uv_ref, hg_ref, hv_ref, pg_ref, pv_ref, wg_ref, wv_ref,
                      bg_ref, bv_ref, o_ref, sg, sv, *, tt):
    first = pl.program_id(1) == 0
    halo = SUBLANES

    def conv(u_ref, h_ref, p_ref, w_ref, b_ref, scr):
        @pl.when(first)
        def _():
            scr[0:halo, :] = p_ref[...]

        @pl.when(jnp.logical_not(first))
        def _():
            scr[0:halo, :] = h_ref[...]

        scr[halo:halo + tt, :] = u_ref[...]
        y = b_ref[...]
        for j in range(FFN_CONV):
            start = halo - (FFN_CONV - 1) + j
            y = y + scr[start:start + tt, :] * w_ref[j:j + 1, :]
        return y

    gate = conv(ug_ref, hg_ref, pg_ref, wg_ref, bg_ref, sg)
    val = conv(uv_ref, hv_ref, pv_ref, wv_ref, bv_ref, sv)
    o_ref[...] = (gate * jax.nn.sigmoid(gate) * val).astype(o_ref.dtype)


def _conv_gate(up, prev, cw, cb, l):
    b, t, f2 = up.shape
    f = f2 // 2
    tt = _tile(t, 512, SUBLANES)
    tn = _tile(f, max(512, 512 * 512 // tt))
    nf = f // tn
    rb = tt // SUBLANES

    def main(off):
        return pl.BlockSpec((None, tt, tn), lambda b, i, n: (b, i, n + off))

    def halo(off):
        return pl.BlockSpec((None, SUBLANES, tn),
                            lambda b, i, n: (b, jnp.maximum(i * rb - 1, 0), n + off))

    def prev_spec(off):
        return pl.BlockSpec((None, SUBLANES, tn), lambda b, i, n: (b, 0, n + off))

    def w_spec(rows, off):
        return pl.BlockSpec((None, rows, tn), lambda b, i, n: (l, 0, n + off))

    return pl.pallas_call(
        functools.partial(_conv_gate_kernel, tt=tt),
        grid=(b, t // tt, nf),
        in_specs=[main(0), main(nf), halo(0), halo(nf), prev_spec(0), prev_spec(nf),
                  w_spec(FFN_CONV, 0), w_spec(FFN_CONV, nf), w_spec(1, 0), w_spec(1, nf)],
        out_specs=pl.BlockSpec((None, tt, tn), lambda b, i, n: (b, i, n)),
        out_shape=jax.ShapeDtypeStruct((b, t, f), BF16),
        scratch_shapes=[pltpu.VMEM((SUBLANES + tt, tn), F32), pltpu.VMEM((SUBLANES + tt, tn), F32)],
        compiler_params=_params(("arbitrary", "arbitrary", "arbitrary")),
        name="conv_gate",
    )(up, up, up, up, prev, prev, cw, cw, cb, cb)


def _ffn_up_kernel(x_ref, wg_ref, wv_ref, pg_ref, pv_ref, cwg_ref, cwv_ref, cbg_ref, cbv_ref,
                   xs_ref, o_ref, sg_ref, sv_ref, ups_g_ref, ups_v_ref,
                   wbf_g, wbf_v, scr_g, scr_v, *, tm, n_sub):
    t = pl.program_id(2)
    halo = SUBLANES

    @pl.when(jnp.logical_and(pl.program_id(1) == 0, t == 0))
    def _():
        wbf_g[...] = wg_ref[...].astype(BF16)
        wbf_v[...] = wv_ref[...].astype(BF16)
        ups_g_ref[...] = jnp.dot(xs_ref[...], wbf_g[...], preferred_element_type=F32)
        ups_v_ref[...] = jnp.dot(xs_ref[...], wbf_v[...], preferred_element_type=F32)

    @pl.when(t == 0)
    def _():
        scr_g[0:halo, :] = pg_ref[...]
        scr_v[0:halo, :] = pv_ref[...]

    sub = tm // n_sub

    def project(c):
        rows = slice(c * sub, (c + 1) * sub)
        x = x_ref[rows, :]
        dst = slice(halo + c * sub, halo + (c + 1) * sub)
        scr_g[dst, :] = jnp.dot(x, wbf_g[...], preferred_element_type=F32)
        scr_v[dst, :] = jnp.dot(x, wbf_v[...], preferred_element_type=F32)

    def conv(c, scr, cw_ref, cb_ref):
        y = cb_ref[...]
        for j in range(FFN_CONV):
            start = halo - (FFN_CONV - 1) + j + c * sub
            y = y + scr[start:start + sub, :] * cw_ref[j:j + 1, :]
        return y

    project(0)
    for c in range(n_sub):
        if c + 1 < n_sub:
            project(c + 1)
        gate = conv(c, scr_g, cwg_ref, cbg_ref)
        val = conv(c, scr_v, cwv_ref, cbv_ref)
        o_ref[c * sub:(c + 1) * sub, :] = (gate * jax.nn.sigmoid(gate) * val).astype(o_ref.dtype)

    for scr, s_ref in ((scr_g, sg_ref), (scr_v, sv_ref)):
        tail = scr[tm:tm + halo, :]
        s_ref[...] = tail
        scr[0:halo, :] = tail


def _ffn_up_fused(x, w_up, prev, cw, cb, l, x_side):
    b, t, d = x.shape
    ts = x_side.shape[1]
    f = w_up.shape[2] // 2
    tm = _tile(t, 1024, SUBLANES)
    tn = _tile(f, 512)
    nf = f // tn

    def cols(rows, off, arr_l):
        return pl.BlockSpec((None, rows, tn), lambda n, b, i: (arr_l(b), 0, n + off))

    layer = lambda b: l
    batch = lambda b: b
    tail_spec = pl.BlockSpec((None, SUBLANES, tn), lambda n, b, i: (b, 0, n))
    side_spec = pl.BlockSpec((None, ts, tn), lambda n, b, i: (0, 0, n))
    return pl.pallas_call(
        functools.partial(_ffn_up_kernel, tm=tm, n_sub=FFN_SUB if tm % (FFN_SUB * 16) == 0 else 1),
        grid=(nf, b, t // tm),
        in_specs=[
            pl.BlockSpec((None, tm, d), lambda n, b, i: (b, i, 0)),
            cols(d, 0, layer), cols(d, nf, layer),
            cols(SUBLANES, 0, batch), cols(SUBLANES, nf, batch),
            cols(FFN_CONV, 0, layer), cols(FFN_CONV, nf, layer),
            cols(1, 0, layer), cols(1, nf, layer),
            pl.BlockSpec((None, ts, d), lambda n, b, i: (0, 0, 0)),
        ],
        out_specs=[pl.BlockSpec((None, tm, tn), lambda n, b, i: (b, i, n)), tail_spec, tail_spec,
                   side_spec, side_spec],
        out_shape=[jax.ShapeDtypeStruct((b, t, f), BF16),
                   jax.ShapeDtypeStruct((b, SUBLANES, f), F32),
                   jax.ShapeDtypeStruct((b, SUBLANES, f), F32),
                   jax.ShapeDtypeStruct((1, ts, f), F32),
                   jax.ShapeDtypeStruct((1, ts, f), F32)],
        scratch_shapes=[pltpu.VMEM((d, tn), BF16), pltpu.VMEM((d, tn), BF16),
                        pltpu.VMEM((SUBLANES + tm, tn), F32), pltpu.VMEM((SUBLANES + tm, tn), F32)],
        compiler_params=_params(("arbitrary", "arbitrary", "arbitrary")),
        name="ffn_up_conv",
    )(x, w_up, w_up, prev, prev, cw, cw, cb, cb, x_side)


def _rope_tables(pos, rows):
    half = HEAD_DIM // 2
    inv = ROPE_BASE ** (-jnp.arange(half, dtype=F32) / half)
    ang = pos.astype(F32)[:, None] * inv[None, :]
    cos, sin = jnp.cos(ang), jnp.sin(ang)
    cos2 = jnp.concatenate([cos, cos], axis=-1)
    sin2 = jnp.concatenate([-sin, sin], axis=-1)
    pad = rows - pos.shape[0]
    if pad:
        cos2 = jnp.pad(cos2, ((0, pad), (0, 0)))
        sin2 = jnp.pad(sin2, ((0, pad), (0, 0)))
    return cos2, sin2


def _prev_rows(buf):
    return jnp.pad(buf, ((0, 0), (SUBLANES - buf.shape[1], 0), (0, 0)))


def _mixers(l, b, t, ret_in, ml_in, gates, st, w, rope, lg_tab):
    t_pad = -(-t // CHUNK) * CHUNK
    l_true = CHUNK if t % CHUNK == 0 else t
    if t_pad != t:
        ret_in = jnp.pad(ret_in, ((0, 0), (0, t_pad - t), (0, 0)))
        ml_in = jnp.pad(ml_in, ((0, 0), (0, t_pad - t), (0, 0)))
        gates = jnp.pad(gates, ((0, 0), (0, 0), (0, t_pad - t)))
    ro, ret_s = _retention(ret_in, rope[0], rope[1], lg_tab, st["ret_s"], w["ret_norm_g"], l, l_true)
    mo, ml_c, ml_n, ml_m = _mlstm(
        ml_in, _prev_rows(st["ml_conv"]), gates, w["gbias"], w["ml_conv_w"], w["ml_conv_b"],
        w["ml_wq"], w["ml_wk"], st["ml_c"], st["ml_n"], st["ml_m"][:, :, None],
        w["ml_norm_g"], w["ml_skip"], l, l_true)
    new_st = dict(ret_s=ret_s, ml_c=ml_c, ml_n=ml_n, ml_m=ml_m[:, :, 0])
    return ro[:, :t], mo[:, :t], new_st


def _layer(l, lam_init, xp, xs, mods_p, mods_s, st_p, st_s, w, cache, rope_p, rope_s, lg_tab,
           kv_stacks):
    bp, tp, d = xp.shape
    bs = st_s["ret_s"].shape[0]
    ts = xs.shape[1] // bs
    sh1p, sc1p, g1p, sh2p, sc2p, g2p = mods_p
    sh1s, sc1s, g1s, sh2s, sc2s, g2s = mods_s
    depth = w["w_in_t"].shape[0]
    f2 = w["ffn_w_up"].shape[2]

    hp = _norm(xp, w["g_mix"], l, sc1p, sh1p)
    hs = _norm(xs, w["g_mix"], l, sc1s, sh1s)

    def in_proj(col0, width, tn, name, stack=None):
        assert col0 % tn == 0 and width % tn == 0
        return _matmul([hp], w["w_in_t"], l, width // tn, tn, 1024, name=name, w_transposed=True,
                       w_tile=lambda n: n + col0 // tn, stack=stack, side=([hs], None, None))

    q_p, q_s = in_proj(W_COL_Q, DA_W, 1024, "q_proj")
    k_stack, k_s = in_proj(W_COL_K, DA_W, 1024, "k_proj", stack=(depth, kv_stacks[0]))
    v_stack, v_s = in_proj(W_COL_V, DA_W, 1024, "v_proj", stack=(depth, kv_stacks[1]))
    ret_p, ret_s = in_proj(W_COL_RET, 4 * RET_W, 1024, "ret_proj")
    ml_p, ml_s = in_proj(W_COL_ML, 3 * ML_W, 512, "ml_proj")
    gates_p = _gates(hp, w["w_in_t"], l)
    gates_s = _gates(hs, w["w_in_t"], l).reshape(GATE_ROWS, bs, ts).transpose(1, 0, 2)
    ml_s = ml_s.reshape(bs, ts, 3 * ML_W)

    oa_p = _attn_prompt(q_p, k_stack, v_stack, w["lam_p"], w["da_norm_g"], l, lam_init)
    oa_s = _attn_decode(q_s.reshape(bs, ts, DA_W), k_s, v_s, *cache, w["lam_p"], w["da_norm_g"],
                        l, lam_init)
    ro_p, mo_p, new_p = _mixers(l, bp, tp, ret_p, ml_p, gates_p, st_p, w, rope_p, lg_tab)
    ro_s, mo_s, new_s = _mixers(l, bs, ts, ret_s.reshape(bs, ts, 4 * RET_W), ml_s, gates_s,
                                st_s, w, rope_s, lg_tab)
    mix_s = [a.astype(BF16).reshape(1, bs * ts, a.shape[-1]) for a in (oa_s, ro_s, mo_s)]
    xp, xs = _matmul([oa_p, ro_p, mo_p], w["w_out"], l, d // _tile(d, 1024), _tile(d, 1024), 1024,
                     res=xp, gate=g1p, side=(mix_s, xs, g1s), name="out_proj")

    h2p = _norm(xp, w["g_ffn"], l, sc2p, sh2p)
    h2s = _norm(xs, w["g_ffn"], l, sc2s, sh2s)
    act_p, tail_g, tail_v, up_g_s, up_v_s = _ffn_up_fused(
        h2p, w["ffn_w_up"], _prev_rows(st_p["ffn_conv"]), w["ffn_conv_w"], w["ffn_conv_b"], l, h2s)
    up_s = jnp.concatenate([up_g_s, up_v_s], axis=-1).reshape(bs, ts, f2)
    act_s = _conv_gate(up_s, _prev_rows(st_s["ffn_conv"]), w["ffn_conv_w"], w["ffn_conv_b"], l)
    xp, xs = _matmul([act_p], w["ffn_w_down"], l, d // _tile(d, 512), _tile(d, 512), 512,
                     res=xp, gate=g2p, side=([act_s.reshape(1, bs * ts, f2 // 2)], xs, g2s),
                     name="ffn_down")

    new_p.update(
        ml_conv=ml_p[:, tp - (ML_CONV - 1):, COL_MU:COL_MU + ML_W],
        ffn_conv=jnp.concatenate([tail_g, tail_v], axis=-1)[:, SUBLANES - (FFN_CONV - 1):, :])
    new_s.update(ml_conv=ml_s[:, ts - (ML_CONV - 1):, COL_MU:COL_MU + ML_W],
                 ffn_conv=up_s[:, ts - (FFN_CONV - 1):, :])
    return xp, xs, (k_stack, v_stack), (k_s, v_s), new_p, new_s


def kernel(x_prompt, x_sample, c_prompt, c_sample, cache_k, cache_v, page_table, state_ret, state_ml_c, state_ml_n, state_ml_m, state_ml_conv, state_ffn_conv, w_ada, b_ada, g_mix, w_in, lam_q1, lam_k1, lam_q2, lam_k2, da_norm_g, ret_norm_g, ml_conv_w, ml_conv_b, ml_wq, ml_wk, ml_b_i, ml_b_f, ml_norm_g, ml_skip, w_out, g_ffn, ffn_w_up, ffn_conv_w, ffn_conv_b, ffn_w_down, g_final):
    bp, tp, d = x_prompt.shape
    bs, ts, _ = x_sample.shape
    depth = w_in.shape[0]
    f2 = ffn_w_up.shape[2]
    past = page_table.shape[1] * PAGE

    def per_layer_rows(a):
        return a.reshape(depth, 1, a.shape[-1])

    zeros8 = jnp.zeros((depth, GATE_ROWS - N_GATES), F32)
    w = dict(
        g_mix=g_mix, g_ffn=g_ffn, w_out=w_out, ffn_w_up=ffn_w_up, ffn_w_down=ffn_w_down,
        w_in_t=jnp.swapaxes(w_in, 1, 2),
        gbias=jnp.concatenate([ml_b_i, ml_b_f, zeros8], axis=1)[:, :, None],
        lam_p=jnp.stack([lam_q1, lam_k1, lam_q2, lam_k2], axis=1),
        da_norm_g=per_layer_rows(da_norm_g), ret_norm_g=per_layer_rows(ret_norm_g),
        ml_conv_w=ml_conv_w, ml_conv_b=per_layer_rows(ml_conv_b), ml_wq=ml_wq, ml_wk=ml_wk,
        ml_norm_g=per_layer_rows(ml_norm_g), ml_skip=per_layer_rows(ml_skip),
        ffn_conv_w=ffn_conv_w, ffn_conv_b=per_layer_rows(ffn_conv_b))

    n_c = bp + bs
    c_rows = -(-n_c // SUBLANES) * SUBLANES
    c_all = jnp.pad(jnp.concatenate([c_prompt, c_sample], axis=0), ((0, c_rows - n_c), (0, 0)))
    mod = _ada(c_all, w_ada, b_ada)

    lg_np = np.log(1.0 - 2.0 ** (-5.0 - np.arange(N_RET, dtype=np.float64)))
    lg_tab = jnp.asarray(np.broadcast_to(lg_np[:, None], (N_RET, HEAD_DIM)), F32)
    tp_pad = -(-tp // CHUNK) * CHUNK
    ts_pad = -(-ts // CHUNK) * CHUNK
    rope_p = _rope_tables(jnp.arange(tp), tp_pad)
    rope_s = _rope_tables(past + jnp.arange(ts), ts_pad)

    zero_st = dict(
        ret_s=jnp.zeros((bp, N_RET, HEAD_DIM, HEAD_DIM), F32),
        ml_c=jnp.zeros((bp, N_ML, HEAD_DIM, HEAD_DIM), F32),
        ml_n=jnp.zeros((bp, N_ML, HEAD_DIM), F32),
        ml_m=jnp.zeros((bp, N_ML), F32),
        ml_conv=jnp.zeros((bp, ML_CONV - 1, ML_W), F32),
        ffn_conv=jnp.zeros((bp, FFN_CONV - 1, f2), F32))
    names = ("ret_s", "ml_c", "ml_n", "ml_m", "ml_conv", "ffn_conv")
    out_p = {n: [] for n in names}
    out_s = {n: [] for n in names}
    kv_p = (None, None)
    k_s, v_s = [], []
    yp = x_prompt
    ys = x_sample.reshape(1, bs * ts, d)
    for l in range(depth):
        lam_init = 0.8 - 0.6 * math.exp(-0.3 * l)
        parts = jnp.split(mod[l], 6, axis=-1)
        mods_p = [m[:bp, None, :] for m in parts]
        mods_s = [jnp.repeat(m[bp:n_c], ts, axis=0)[None] for m in parts]
        st_in = dict(ret_s=state_ret[l], ml_c=state_ml_c[l], ml_n=state_ml_n[l], ml_m=state_ml_m[l],
                     ml_conv=state_ml_conv[l], ffn_conv=state_ffn_conv[l])
        yp, ys, kv_p, kv_s, st_p, st_s = _layer(
            l, lam_init, yp, ys, mods_p, mods_s, zero_st, st_in, w,
            (cache_k, cache_v, page_table), rope_p, rope_s, lg_tab, kv_p)
        k_s.append(kv_s[0])
        v_s.append(kv_s[1])
        for n in names:
            out_p[n].append(st_p[n])
            out_s[n].append(st_s[n])

    y_prompt = _norm(yp, g_final, None, out_dtype=F32)
    y_sample = _norm(ys, g_final, None, out_dtype=F32).reshape(bs, ts, d)
    kv_shape_p = (depth, bp, tp, N_DA, HEAD_DIM)
    kv_shape_s = (depth, bs, ts, N_DA, HEAD_DIM)
    return (y_prompt, y_sample,
            kv_p[0].reshape(kv_shape_p), kv_p[1].reshape(kv_shape_p),
            jnp.stack(k_s).reshape(kv_shape_s), jnp.stack(v_s).reshape(kv_shape_s),
            jnp.stack(out_p["ret_s"]), jnp.stack(out_s["ret_s"]),
            jnp.stack(out_p["ml_c"]), jnp.stack(out_s["ml_c"]),
            jnp.stack(out_p["ml_n"]), jnp.stack(out_s["ml_n"]),
            jnp.stack(out_p["ml_m"]), jnp.stack(out_s["ml_m"]),
            jnp.stack(out_p["ml_conv"]), jnp.stack(out_s["ml_conv"]),
            jnp.stack(out_p["ffn_conv"]), jnp.stack(out_s["ffn_conv"]))
```

```python
import functools
import math

import numpy as np
import jax
import jax.numpy as jnp
from jax import lax
from jax.experimental import pallas as pl
from jax.experimental.pallas import tpu as pltpu

F32 = jnp.float32
BF16 = jnp.bfloat16

HEAD_DIM = 128
DA_QK = HEAD_DIM // 2
N_DA = 8
N_RET = 4
N_ML = 4
DA_W = N_DA * HEAD_DIM
RET_W = N_RET * HEAD_DIM
ML_W = N_ML * HEAD_DIM
ML_CONV = 4
FFN_CONV = 3
CHUNK = 128
PAGE = 128
RMS_EPS = 1e-6
ROPE_BASE = 10000.0
NEG_INF = -1e30

W_COL_Q, W_COL_K, W_COL_V = 0, DA_W, 2 * DA_W
W_COL_RET = 3 * DA_W
W_COL_ML = W_COL_RET + 4 * RET_W
W_COL_GATES = W_COL_ML + 3 * ML_W
COL_RQ, COL_RK, COL_RV, COL_RG = 0, RET_W, 2 * RET_W, 3 * RET_W
COL_MU, COL_MV, COL_MO = 0, ML_W, 2 * ML_W
N_GATES = 2 * N_ML
GATE_ROWS = 16

SUBLANES = 8
LANES = 128
VMEM_LIMIT = 56 * 1024 * 1024
FFN_SUB = 4
CHUNK_UNROLL = 16

NT_DIMS = (((1,), (1,)), ((), ()))
TN_DIMS = (((0,), (0,)), ((), ()))


def _tile(n, pref, mult=LANES):
    if n <= pref:
        return n
    t = (pref // mult) * mult
    while t >= mult:
        if n % t == 0:
            return t
        t -= mult
    return n


def _params(sem):
    return pltpu.CompilerParams(dimension_semantics=sem, vmem_limit_bytes=VMEM_LIMIT)


def _rms(x):
    return x * lax.rsqrt(jnp.mean(x * x, axis=-1, keepdims=True) + RMS_EPS)


def _ada_kernel(c_ref, w_ref, b_ref, o_ref):
    c = c_ref[...]
    a = (c * jax.nn.sigmoid(c)).astype(BF16)
    o_ref[...] = jnp.dot(a, w_ref[...].astype(BF16), preferred_element_type=F32) + b_ref[...]


def _ada(c_all, w_ada, b_ada):
    depth, d, n = w_ada.shape
    rows = c_all.shape[0]
    tn = _tile(n, 1024)
    return pl.pallas_call(
        _ada_kernel,
        grid=(depth, n // tn),
        in_specs=[
            pl.BlockSpec((rows, d), lambda l, j: (0, 0)),
            pl.BlockSpec((None, d, tn), lambda l, j: (l, 0, j)),
            pl.BlockSpec((None, 1, tn), lambda l, j: (l, 0, j)),
        ],
        out_specs=pl.BlockSpec((None, rows, tn), lambda l, j: (l, 0, j)),
        out_shape=jax.ShapeDtypeStruct((depth, rows, n), F32),
        compiler_params=_params(("arbitrary", "arbitrary")),
        name="ada",
    )(c_all, w_ada, b_ada.reshape(depth, 1, n))


def _norm_kernel(*refs, modulated):
    if modulated:
        x_ref, g_ref, sc_ref, sh_ref, o_ref = refs
    else:
        x_ref, g_ref, o_ref = refs
    y = _rms(x_ref[...]) * g_ref[...]
    if modulated:
        y = y * (1.0 + sc_ref[...]) + sh_ref[...]
    o_ref[...] = y.astype(o_ref.dtype)


def _mod_spec(mod, tm, tn, with_n):
    per_row = mod.shape[1] != 1
    rows = tm if per_row else 1
    if with_n:
        return pl.BlockSpec((None, rows, tn), lambda n, b, t: (b, t if per_row else 0, n))
    return pl.BlockSpec((None, rows, tn), lambda b, t: (b, t if per_row else 0, 0))


def _norm(x, g, l, sc=None, sh=None, out_dtype=BF16):
    bm, tmx, d = x.shape
    tm = _tile(tmx, 1024, SUBLANES)
    modulated = sc is not None
    g_spec = (pl.BlockSpec((None, 1, d), lambda b, t: (l, 0, 0)) if l is not None
              else pl.BlockSpec((1, d), lambda b, t: (0, 0)))
    in_specs = [pl.BlockSpec((None, tm, d), lambda b, t: (b, t, 0)), g_spec]
    args = [x, g.reshape(g.shape[0], 1, d) if l is not None else g.reshape(1, d)]
    if modulated:
        in_specs += [_mod_spec(sc, tm, d, False), _mod_spec(sh, tm, d, False)]
        args += [sc, sh]
    return pl.pallas_call(
        functools.partial(_norm_kernel, modulated=modulated),
        grid=(bm, tmx // tm),
        in_specs=in_specs,
        out_specs=pl.BlockSpec((None, tm, d), lambda b, t: (b, t, 0)),
        out_shape=jax.ShapeDtypeStruct((bm, tmx, d), out_dtype),
        compiler_params=_params(("arbitrary", "arbitrary")),
        name="norm",
    )(*args)


def _mm_kernel(*refs, row_offs, row_sizes, has_res, has_side, w_transposed):
    n_in = len(row_offs)
    it = iter(refs)
    main = [[next(it) for _ in range(n_in)]]
    w_ref = next(it)
    main += [next(it), next(it)] if has_res else [None, None]
    if has_side:
        side = [[next(it) for _ in range(n_in)]]
        side += [next(it), next(it)] if has_res else [None, None]
    wbf_ref = refs[-1]
    o_ref, side_o_ref = (refs[-3], refs[-2]) if has_side else (refs[-2], None)

    def project(x_refs, res_ref, gate_ref, out_ref):
        acc = None
        for x_ref, off, size in zip(x_refs, row_offs, row_sizes):
            if w_transposed:
                part = lax.dot_general(x_ref[...], wbf_ref[:, off:off + size], NT_DIMS,
                                       preferred_element_type=F32)
            else:
                part = jnp.dot(x_ref[...], wbf_ref[off:off + size, :], preferred_element_type=F32)
            acc = part if acc is None else acc + part
        if res_ref is not None:
            acc = res_ref[...] + gate_ref[...] * acc
        out_ref[...] = acc.astype(out_ref.dtype)

    @pl.when(jnp.logical_and(pl.program_id(1) == 0, pl.program_id(2) == 0))
    def _():
        wbf_ref[...] = w_ref[...].astype(BF16)
        if has_side:
            project(*side, side_o_ref)

    project(*main, o_ref)


def _matmul(xs, w, l, n_tiles, tn, tm_pref, res=None, gate=None, w_tile=lambda n: n,
            w_transposed=False, stack=None, side=None, name="mm"):
    bm, tmx = xs[0].shape[:2]
    k = w.shape[2] if w_transposed else w.shape[1]
    sizes = [x.shape[2] for x in xs]
    offs = [sum(sizes[:i]) for i in range(len(xs))]
    assert sum(sizes) == k
    tm = _tile(tmx, tm_pref, SUBLANES)
    in_specs = [pl.BlockSpec((None, tm, s), lambda n, b, t: (b, t, 0)) for s in sizes]
    if w_transposed:
        in_specs.append(pl.BlockSpec((None, tn, k), lambda n, b, t: (l, w_tile(n), 0)))
    else:
        in_specs.append(pl.BlockSpec((None, k, tn), lambda n, b, t: (l, 0, w_tile(n))))
    args = list(xs) + [w]
    if res is not None:
        in_specs.append(pl.BlockSpec((None, tm, tn), lambda n, b, t: (b, t, n)))
        in_specs.append(_mod_spec(gate, tm, tn, True))
        args += [res, gate]
    out_specs = [pl.BlockSpec((None, tm, tn), lambda n, b, t: (b, t, n))]
    out_shape = [jax.ShapeDtypeStruct((bm, tmx, n_tiles * tn), F32)]
    if side is not None:
        side_xs, side_res, side_gate = side
        ts = side_xs[0].shape[1]
        side_cols = pl.BlockSpec((None, ts, tn), lambda n, b, t: (0, 0, n))
        in_specs += [pl.BlockSpec((None, ts, s), lambda n, b, t: (0, 0, 0)) for s in sizes]
        args += list(side_xs)
        if res is not None:
            in_specs += [side_cols, side_cols]
            args += [side_res, side_gate]
        out_specs.append(side_cols)
        out_shape.append(jax.ShapeDtypeStruct((1, ts, n_tiles * tn), F32))
    aliases = {}
    if stack is not None:
        depth, buf = stack
        out_specs[0] = pl.BlockSpec((None, None, tm, tn), lambda n, b, t: (l, b, t, n))
        out_shape[0] = jax.ShapeDtypeStruct((depth, bm, tmx, n_tiles * tn), F32)
        if buf is not None:
            in_specs.append(pl.BlockSpec(memory_space=pl.ANY))
            args.append(buf)
            aliases = {len(args) - 1: 0}
    outs = pl.pallas_call(
        functools.partial(_mm_kernel, row_offs=tuple(offs), row_sizes=tuple(sizes),
                          has_res=res is not None, has_side=side is not None,
                          w_transposed=w_transposed),
        grid=(n_tiles, bm, tmx // tm),
        in_specs=in_specs,
        out_specs=out_specs,
        out_shape=out_shape,
        input_output_aliases=aliases,
        scratch_shapes=[pltpu.VMEM((tn, k) if w_transposed else (k, tn), BF16)],
        compiler_params=_params(("arbitrary", "arbitrary", "arbitrary")),
        name=name,
    )(*args)
    return outs if side is not None else outs[0]


def _gates_kernel(h_ref, wg_ref, o_ref):
    row = lax.broadcasted_iota(jnp.int32, wg_ref.shape, 0)
    wg = jnp.where(row < N_GATES, wg_ref[...], 0.0).astype(BF16)
    o_ref[...] = lax.dot_general(wg, h_ref[...], NT_DIMS, preferred_element_type=F32)


def _gates(h, w_in_t, l):
    bm, tmx, d = h.shape
    tm = _tile(tmx, 1024)
    assert W_COL_GATES % GATE_ROWS == 0 and w_in_t.shape[1] == W_COL_GATES + N_GATES
    return pl.pallas_call(
        _gates_kernel,
        grid=(bm, tmx // tm),
        in_specs=[pl.BlockSpec((None, tm, d), lambda b, t: (b, t, 0)),
                  pl.BlockSpec((None, GATE_ROWS, d), lambda b, t: (l, W_COL_GATES // GATE_ROWS, 0))],
        out_specs=pl.BlockSpec((None, GATE_ROWS, tm), lambda b, t: (b, 0, t)),
        out_shape=jax.ShapeDtypeStruct((bm, GATE_ROWS, tmx), F32),
        compiler_params=_params(("arbitrary", "arbitrary")),
        name="gates",
    )(h, w_in_t)


def _lambda(lam_ref, lam_init):
    lp = lam_ref[...]
    a = jnp.exp(jnp.sum(lp[0:1] * lp[1:2], axis=1, keepdims=True))
    b = jnp.exp(jnp.sum(lp[2:3] * lp[3:4], axis=1, keepdims=True))
    return a - b + lam_init


SCORE_SCALE_LOG2 = DA_QK ** -0.5 * math.log2(math.e)


def _online_softmax_step(s, v_bf, m_scr, l_scr, acc_scr):
    m_prev = m_scr[...]
    m_new = jnp.maximum(m_prev, jnp.max(s, axis=1, keepdims=True))
    alpha = jnp.exp2(m_prev - m_new)
    p = jnp.exp2(s - m_new)
    l_scr[...] = alpha * l_scr[...] + jnp.sum(p, axis=1, keepdims=True)
    acc_scr[...] = alpha * acc_scr[...] + jnp.dot(p.astype(BF16), v_bf, preferred_element_type=F32)
    m_scr[...] = m_new


def _attn_prompt_block(qi, q_ref, kbf, vbf, lam, g_ref, o_ref, *, tq, lam_init):
    n0 = qi * tq
    q = q_ref[n0:n0 + tq, :] * SCORE_SCALE_LOG2
    lane = lax.broadcasted_iota(jnp.int32, q.shape, 1)
    row = lax.broadcasted_iota(jnp.int32, (tq, tq), 0)
    col = lax.broadcasted_iota(jnp.int32, (tq, tq), 1)
    outs = []
    for first_map in (True, False):
        qm = jnp.where((lane < DA_QK) if first_map else (lane >= DA_QK), q, 0.0).astype(BF16)
        s_diag = lax.dot_general(qm, kbf[n0:n0 + tq, :], NT_DIMS, preferred_element_type=F32)
        s_diag = jnp.where(col <= row, s_diag, NEG_INF)
        m = jnp.max(s_diag, axis=1, keepdims=True)
        if qi > 0:
            s_past = lax.dot_general(qm, kbf[0:n0, :], NT_DIMS, preferred_element_type=F32)
            m = jnp.maximum(m, jnp.max(s_past, axis=1, keepdims=True))
        p_diag = jnp.exp2(s_diag - m)
        o = jnp.dot(p_diag.astype(BF16), vbf[n0:n0 + tq, :], preferred_element_type=F32)
        if qi > 0:
            p_past = jnp.exp2(s_past - m)
            o = o + jnp.dot(p_past.astype(BF16), vbf[0:n0, :], preferred_element_type=F32)
        outs.append(o[:, 0:HEAD_DIM] / o[:, HEAD_DIM:HEAD_DIM + 1])
    oa = outs[0] - lam * outs[1]
    o_ref[n0:n0 + tq, :] = (_rms(oa) * g_ref[...] * (1.0 - lam_init)).astype(o_ref.dtype)


def _attn_prompt_kernel(q_ref, k_ref, v_ref, lam_ref, g_ref, o_ref, kbf, vbf, *, tq, n_q, lam_init):
    kbf[...] = k_ref[...].astype(BF16)
    vbf[:, 0:HEAD_DIM] = v_ref[...].astype(BF16)
    vbf[:, HEAD_DIM:2 * HEAD_DIM] = jnp.ones(v_ref.shape, BF16)
    lam = _lambda(lam_ref, lam_init)
    for qi in range(n_q):
        _attn_prompt_block(qi, q_ref, kbf, vbf, lam, g_ref, o_ref, tq=tq, lam_init=lam_init)


def _attn_prompt(q_new, k_stack, v_stack, lam_p, da_g, l, lam_init):
    b, t, _ = q_new.shape
    tq = _tile(t, 256)
    kv_spec = pl.BlockSpec((None, None, t, HEAD_DIM), lambda b, h: (l, b, 0, h))
    return pl.pallas_call(
        functools.partial(_attn_prompt_kernel, tq=tq, n_q=t // tq, lam_init=lam_init),
        grid=(b, N_DA),
        in_specs=[
            pl.BlockSpec((None, t, HEAD_DIM), lambda b, h: (b, 0, h)),
            kv_spec, kv_spec,
            pl.BlockSpec((None, 4, DA_QK), lambda b, h: (l, 0, 0)),
            pl.BlockSpec((None, 1, HEAD_DIM), lambda b, h: (l, 0, h)),
        ],
        out_specs=pl.BlockSpec((None, t, HEAD_DIM), lambda b, h: (b, 0, h)),
        out_shape=jax.ShapeDtypeStruct((b, t, DA_W), BF16),
        scratch_shapes=[pltpu.VMEM((t, HEAD_DIM), BF16), pltpu.VMEM((t, 2 * HEAD_DIM), BF16)],
        compiler_params=_params(("arbitrary", "arbitrary")),
        name="attn_prompt",
    )(q_new, k_stack, v_stack, lam_p, da_g)


DEC_ROWS = N_DA * 2 * SUBLANES
PAGE_ROWS = PAGE * N_DA
LOG2_HEADS = N_DA.bit_length() - 1
LOG2_HEAD_ROWS = (2 * SUBLANES).bit_length() - 1


def _attn_decode_kernel(pt_ref, q_ref, kn_ref, vn_ref, lam_ref, g_ref, *rest,
                        pages, n_steps, t_new, lam_init):
    k_refs = rest[:pages]
    v_refs = rest[pages:2 * pages]
    o_ref, qm, bias, m_scr, l_scr, acc_scr, kpad, vpad = rest[2 * pages:]
    step = pl.program_id(1)

    @pl.when(step == 0)
    def _():
        q = q_ref[...] * SCORE_SCALE_LOG2
        lane = lax.broadcasted_iota(jnp.int32, (SUBLANES, HEAD_DIM), 1)
        for h in range(N_DA):
            qh = q[:, h * HEAD_DIM:(h + 1) * HEAD_DIM]
            r0 = h * 2 * SUBLANES
            qm[r0:r0 + SUBLANES, :] = jnp.where(lane < DA_QK, qh, 0.0)
            qm[r0 + SUBLANES:r0 + 2 * SUBLANES, :] = jnp.where(lane >= DA_QK, qh, 0.0)
        row = lax.broadcasted_iota(jnp.int32, bias.shape, 0)
        col = lax.broadcasted_iota(jnp.int32, bias.shape, 1)
        bias[...] = jnp.where((col & (N_DA - 1)) == (row >> LOG2_HEAD_ROWS), 0.0, NEG_INF)
        m_scr[...] = jnp.full(m_scr.shape, -jnp.inf, F32)
        l_scr[...] = jnp.zeros(l_scr.shape, F32)
        acc_scr[...] = jnp.zeros(acc_scr.shape, F32)

    qb = qm[...].astype(BF16)
    head_bias = bias[...]
    group = 2 if pages % 2 == 0 else 1

    def scores_of(g):
        return [lax.dot_general(qb, k_refs[p][...].astype(BF16), NT_DIMS,
                                preferred_element_type=F32) + head_bias
                for p in range(g * group, (g + 1) * group)]

    m_run, l_run, acc = m_scr[...], l_scr[...], acc_scr[...]
    upcoming = scores_of(0)
    for g in range(pages // group):
        scores = upcoming
        if (g + 1) * group < pages:
            upcoming = scores_of(g + 1)
        m_new = m_run
        for s in scores:
            m_new = jnp.maximum(m_new, jnp.max(s, axis=1, keepdims=True))
        alpha = jnp.exp2(m_run - m_new)
        l_run = alpha * l_run
        pv = None
        for i, s in enumerate(scores):
            e = jnp.exp2(s - m_new)
            l_run = l_run + jnp.sum(e, axis=1, keepdims=True)
            d = jnp.dot(e.astype(BF16), v_refs[g * group + i][...].astype(BF16),
                        preferred_element_type=F32)
            pv = d if pv is None else pv + d
        acc = alpha * acc + pv
        m_run = m_new
    m_scr[...] = m_run
    l_scr[...] = l_run
    acc_scr[...] = acc

    @pl.when(step == n_steps - 1)
    def _():
        kpad[...] = jnp.zeros(kpad.shape, F32)
        vpad[...] = jnp.zeros(vpad.shape, F32)
        kpad[0:t_new * N_DA, :] = kn_ref[...]
        vpad[0:t_new * N_DA, :] = vn_ref[...]
        s = lax.dot_general(qb, kpad[...].astype(BF16), NT_DIMS, preferred_element_type=F32)
        row = lax.broadcasted_iota(jnp.int32, s.shape, 0)
        col = lax.broadcasted_iota(jnp.int32, s.shape, 1)
        causal = (col >> LOG2_HEADS) <= (row & (SUBLANES - 1))
        s = jnp.where(causal, s + head_bias, NEG_INF)
        _online_softmax_step(s, vpad[...].astype(BF16), m_scr, l_scr, acc_scr)

        lam = _lambda(lam_ref, lam_init)
        o = acc_scr[...] / l_scr[...]
        for h in range(N_DA):
            r0 = h * 2 * SUBLANES
            cols = slice(h * HEAD_DIM, (h + 1) * HEAD_DIM)
            oa = o[r0:r0 + SUBLANES] - lam * o[r0 + SUBLANES:r0 + 2 * SUBLANES]
            o_ref[:, cols] = _rms(oa) * g_ref[:, cols] * (1.0 - lam_init)


def _attn_decode(q_new, k_new, v_new, cache_k, cache_v, page_table, lam_p, da_g, l, lam_init):
    b, t_new, _ = q_new.shape
    assert t_new == SUBLANES and cache_k.shape[2:] == (PAGE, N_DA, HEAD_DIM)
    n_pages = page_table.shape[1]
    pages = _tile(n_pages, 16, 1)
    n_steps = n_pages // pages
    n_pool = cache_k.shape[1]
    ck = cache_k.reshape(cache_k.shape[0], n_pool, PAGE_ROWS, HEAD_DIM)
    cv = cache_v.reshape(cache_v.shape[0], n_pool, PAGE_ROWS, HEAD_DIM)
    kn = k_new.reshape(b, t_new * N_DA, HEAD_DIM)
    vn = v_new.reshape(b, t_new * N_DA, HEAD_DIM)

    def page_spec(p):
        return pl.BlockSpec((None, None, PAGE_ROWS, HEAD_DIM),
                            lambda b, s, pt: (l, pt[b * n_pages + s * pages + p], 0, 0))

    new_spec = pl.BlockSpec((None, t_new * N_DA, HEAD_DIM), lambda b, s, pt: (b, 0, 0))
    grid_spec = pltpu.PrefetchScalarGridSpec(
        num_scalar_prefetch=1,
        grid=(b, n_steps),
        in_specs=[
            pl.BlockSpec((None, t_new, DA_W), lambda b, s, pt: (b, 0, 0)),
            new_spec, new_spec,
            pl.BlockSpec((None, 4, DA_QK), lambda b, s, pt: (l, 0, 0)),
            pl.BlockSpec((None, 1, DA_W), lambda b, s, pt: (l, 0, 0)),
        ] + [page_spec(p) for p in range(pages)] * 2,
        out_specs=pl.BlockSpec((None, t_new, DA_W), lambda b, s, pt: (b, 0, 0)),
        scratch_shapes=[
            pltpu.VMEM((DEC_ROWS, HEAD_DIM), F32),
            pltpu.VMEM((DEC_ROWS, PAGE_ROWS), F32),
            pltpu.VMEM((DEC_ROWS, 1), F32), pltpu.VMEM((DEC_ROWS, 1), F32),
            pltpu.VMEM((DEC_ROWS, HEAD_DIM), F32),
            pltpu.VMEM((PAGE_ROWS, HEAD_DIM), F32), pltpu.VMEM((PAGE_ROWS, HEAD_DIM), F32),
        ],
    )
    return pl.pallas_call(
        functools.partial(_attn_decode_kernel, pages=pages, n_steps=n_steps, t_new=t_new,
                          lam_init=lam_init),
        grid_spec=grid_spec,
        out_shape=jax.ShapeDtypeStruct((b, t_new, DA_W), F32),
        compiler_params=_params(("arbitrary", "arbitrary")),
        name="attn_decode",
    )(page_table.reshape(-1), q_new, kn, vn, lam_p, da_g, *([ck] * pages), *([cv] * pages))


def _rotary(x, cos2, sin2):
    return x * cos2 + pltpu.roll(x, DA_QK, 1) * sin2


def _cumsum_lanes(x, lane):
    n = x.shape[1]
    shift = 1
    while shift < n:
        x = x + jnp.where(lane >= shift, pltpu.roll(x, shift, 1), 0.0)
        shift *= 2
    return x


def _heads(x, n):
    return [x[:, h * HEAD_DIM:(h + 1) * HEAD_DIM] for h in range(n)]


def _block(x, h):
    return x[h * CHUNK:(h + 1) * CHUNK]


def _stack(xs):
    return jnp.concatenate(xs, axis=0)


def _per_head_col(v):
    return _stack([jnp.broadcast_to(v[h:h + 1, :], (CHUNK, 1)) for h in range(v.shape[0])])


def _chunk_rows(c):
    return pl.ds(c * CHUNK if isinstance(c, int) else pl.multiple_of(c * CHUNK, CHUNK), CHUNK)


def _chunk_loop(body, n_chunks):
    if n_chunks <= CHUNK_UNROLL:
        for c in range(n_chunks):
            body(c)
    else:
        lax.fori_loop(0, n_chunks, lambda c, carry: (body(c), carry)[1], 0, unroll=CHUNK_UNROLL)


def _retention_kernel(q_ref, k_ref, v_ref, gate_ref, cos_ref, sin_ref, lg_ref, s0_ref, gn_ref,
                      o_ref, sout_ref, s_scr, *, n_chunks, l_true):
    L, H = CHUNK, N_RET
    lg = lg_ref[...][:, 0:1]
    lg_col = _per_head_col(lg)
    ti = lax.broadcasted_iota(jnp.int32, (H * L, L), 0) & (L - 1)
    si = lax.broadcasted_iota(jnp.int32, (H * L, L), 1)
    rel = (ti - si).astype(F32)
    dmat = jnp.where(rel >= 0, jnp.exp(lg_col * jnp.maximum(rel, 0.0)), 0.0)
    idx = (lax.broadcasted_iota(jnp.int32, (H * L, 1), 0) & (L - 1)).astype(F32)
    q_dec = jnp.exp(lg_col * (idx + 1.0))
    k_dec = jnp.exp(lg_col * (l_true - 1.0 - idx))
    s_dec = jnp.exp(lg * float(l_true))
    s_scr[...] = s0_ref[...]

    def body(c):
        rows = _chunk_rows(c)
        cos2, sin2 = cos_ref[rows, :], sin_ref[rows, :]
        q_h = [_rotary(x, cos2, sin2) for x in _heads(q_ref[rows, :], H)]
        k_h = [_rotary(x, cos2, sin2) * (HEAD_DIM ** -0.5) for x in _heads(k_ref[rows, :], H)]
        q_bf = [x.astype(BF16) for x in q_h]
        k_bf = [x.astype(BF16) for x in k_h]
        v_bf = [x.astype(BF16) for x in _heads(v_ref[rows, :], H)]
        state = [s_scr[h] for h in range(H)]
        scores = _stack([lax.dot_general(q_bf[h], k_bf[h], NT_DIMS, preferred_element_type=F32)
                         for h in range(H)]) * dmat
        scores_bf = scores.astype(BF16)
        inner = _stack([jnp.dot(_block(scores_bf, h), v_bf[h], preferred_element_type=F32)
                        for h in range(H)])
        cross = _stack([jnp.dot(q_bf[h], state[h].astype(BF16), preferred_element_type=F32)
                        for h in range(H)]) * q_dec
        kd = _stack(k_h) * k_dec
        for h in range(H):
            s_scr[h] = state[h] * s_dec[h:h + 1, :] + jnp.dot(
                _block(kd, h).T.astype(BF16), v_bf[h], preferred_element_type=F32)
        y = _rms(inner + cross)
        g_h = _heads(gate_ref[rows, :], H)
        gn_h = _heads(gn_ref[...], H)
        for h in range(H):
            o_ref[rows, h * HEAD_DIM:(h + 1) * HEAD_DIM] = (
                (_block(y, h) * gn_h[h]) * (g_h[h] * jax.nn.sigmoid(g_h[h]))).astype(o_ref.dtype)

    _chunk_loop(body, n_chunks)
    sout_ref[...] = s_scr[...]


def _retention(ret_in, cos2, sin2, lg_tab, s0, gn, l, l_true):
    b, t, _ = ret_in.shape

    def seq(j):
        return pl.BlockSpec((None, t, RET_W), lambda b: (b, 0, j))

    st_spec = pl.BlockSpec((None, N_RET, HEAD_DIM, HEAD_DIM), lambda b: (b, 0, 0, 0))
    return pl.pallas_call(
        functools.partial(_retention_kernel, n_chunks=t // CHUNK, l_true=l_true),
        grid=(b,),
        in_specs=[
            seq(COL_RQ // RET_W), seq(COL_RK // RET_W), seq(COL_RV // RET_W), seq(COL_RG // RET_W),
            pl.BlockSpec((t, HEAD_DIM), lambda b: (0, 0)),
            pl.BlockSpec((t, HEAD_DIM), lambda b: (0, 0)),
            pl.BlockSpec((N_RET, HEAD_DIM), lambda b: (0, 0)),
            st_spec,
            pl.BlockSpec((None, 1, RET_W), lambda b: (l, 0, 0)),
        ],
        out_specs=[pl.BlockSpec((None, t, RET_W), lambda b: (b, 0, 0)), st_spec],
        out_shape=[jax.ShapeDtypeStruct((b, t, RET_W), BF16),
                   jax.ShapeDtypeStruct((b, N_RET, HEAD_DIM, HEAD_DIM), F32)],
        scratch_shapes=[pltpu.VMEM((N_RET, HEAD_DIM, HEAD_DIM), F32)],
        compiler_params=_params(("arbitrary",)),
        name="retention",
    )(ret_in, ret_in, ret_in, ret_in, cos2, sin2, lg_tab, s0, gn)


def _mlstm_kernel(u_ref, v_ref, og_ref, prev_ref, g_ref, gb_ref, cw_ref, cb_ref, wq_ref, wk_ref,
                  c0_ref, n0_ref, m0_ref, gn_ref, skip_ref,
                  o_ref, cout_ref, nout_ref, mout_ref,
                  xp_scr, uc_scr, c_scr, n_scr, m_scr, *, t, n_chunks, l_true):
    L, H = CHUNK, N_ML
    halo = SUBLANES

    xp_scr[0:halo, :] = prev_ref[...]
    xp_scr[halo:halo + t, :] = u_ref[...]
    conv = cb_ref[...]
    for j in range(ML_CONV):
        start = halo - (ML_CONV - 1) + j
        conv = conv + xp_scr[start:start + t, :] * cw_ref[j:j + 1, :]
    uc_scr[...] = conv * jax.nn.sigmoid(conv)

    wq_bf = [wq_ref[h].astype(BF16) for h in range(H)]
    wk_bf = [wk_ref[h].astype(BF16) for h in range(H)]
    c_scr[...] = c0_ref[...]
    n_scr[...] = n0_ref[...]
    m_scr[...] = m0_ref[...]

    ti = lax.broadcasted_iota(jnp.int32, (H * L, L), 0) & (L - 1)
    si = lax.broadcasted_iota(jnp.int32, (H * L, L), 1)
    eye = ti == si
    causal = si <= ti
    lane = lax.broadcasted_iota(jnp.int32, (H, L), 1)

    heads = functools.partial(_heads, n=H)
    block, stack, per_head_col = _block, _stack, _per_head_col

    def rows_to_blocks(r):
        return stack([jnp.broadcast_to(r[h:h + 1, :], (L, L)) for h in range(H)])

    def rows_to_col(r):
        return jnp.sum(jnp.where(eye, rows_to_blocks(r), 0.0), axis=1, keepdims=True)

    def body(c):
        rows = _chunk_rows(c)
        uc_h = heads(uc_scr[rows, :])
        q_h = [jnp.dot(uc_h[h].astype(BF16), wq_bf[h], preferred_element_type=F32) for h in range(H)]
        k_h = [jnp.dot(uc_h[h].astype(BF16), wk_bf[h], preferred_element_type=F32)
               * (HEAD_DIM ** -0.5) for h in range(H)]
        q_bf = [x.astype(BF16) for x in q_h]
        k_bf = [x.astype(BF16) for x in k_h]
        v_bf = [x.astype(BF16) for x in heads(v_ref[rows, :])]

        gates = g_ref[:, rows] + gb_ref[...]
        ig, fg = gates[0:H], gates[H:2 * H]
        lf = -(jnp.maximum(-fg, 0.0) + jnp.log(1.0 + jnp.exp(-jnp.abs(fg))))
        b_row = _cumsum_lanes(lf, lane)
        b_col = rows_to_col(b_row)
        m_old = m_scr[...]
        n_old = n_scr[...]
        c_old = [c_scr[h] for h in range(H)]

        dlog = jnp.where(causal, b_col + rows_to_blocks(ig - b_row), -jnp.inf)
        inter = b_col + per_head_col(m_old)
        m_row = jnp.maximum(inter, jnp.max(dlog, axis=1, keepdims=True))
        w = jnp.exp(dlog - m_row)
        s_in = jnp.exp(inter - m_row)
        qk = stack([lax.dot_general(q_bf[h], k_bf[h], NT_DIMS, preferred_element_type=F32)
                    for h in range(H)])
        a = w * qk
        a_bf = a.astype(BF16)
        num = stack([jnp.dot(block(a_bf, h), v_bf[h], preferred_element_type=F32)
                     + jnp.dot(q_bf[h], c_old[h].astype(BF16), preferred_element_type=F32)
                     * block(s_in, h) for h in range(H)])
        qn = stack([jnp.sum(q_bf[h].astype(F32) * n_old[h:h + 1, :].astype(BF16).astype(F32),
                            axis=1, keepdims=True) for h in range(H)])
        den = jnp.sum(a, axis=1, keepdims=True) + qn * s_in
        hn = _rms(num / jnp.maximum(jnp.abs(den), jnp.exp(-m_row)))

        og_h = heads(og_ref[rows, :])
        gn_h, skip_h = heads(gn_ref[...]), heads(skip_ref[...])
        for h in range(H):
            y = block(hn, h) * gn_h[h] + skip_h[h] * uc_h[h]
            o_ref[rows, h * HEAD_DIM:(h + 1) * HEAD_DIM] = (
                jax.nn.sigmoid(og_h[h]) * y).astype(o_ref.dtype)

        b_last = jnp.sum(jnp.where(lane == l_true - 1, b_row, 0.0), axis=1, keepdims=True)
        dlast = jnp.where(lane < l_true, b_last - b_row + ig, -jnp.inf)
        m_new = jnp.maximum(b_last + m_old, jnp.max(dlast, axis=1, keepdims=True))
        ws_col = rows_to_col(jnp.exp(dlast - m_new))
        s_old = jnp.exp(b_last + m_old - m_new)
        kw = stack(k_h) * ws_col
        for h in range(H):
            c_scr[h] = c_old[h] * s_old[h:h + 1, :] + jnp.dot(
                block(kw, h).T.astype(BF16), v_bf[h], preferred_element_type=F32)
        n_scr[...] = n_old * s_old + stack(
            [jnp.sum(block(kw, h), axis=0, keepdims=True) for h in range(H)])
        m_scr[...] = m_new

    _chunk_loop(body, n_chunks)
    cout_ref[...] = c_scr[...]
    nout_ref[...] = n_scr[...]
    mout_ref[...] = m_scr[...]


def _mlstm(ml_in, prev, gates, gbias, cw, cb, wq, wk, c0, n0, m0, gn, skip, l, l_true):
    b, t, _ = ml_in.shape

    def seq(j):
        return pl.BlockSpec((None, t, ML_W), lambda b: (b, 0, j))

    def layer(rows):
        return pl.BlockSpec((None, rows, ML_W), lambda b: (l, 0, 0))

    w_spec = pl.BlockSpec((None, N_ML, HEAD_DIM, HEAD_DIM), lambda b: (l, 0, 0, 0))
    c_spec = pl.BlockSpec((None, N_ML, HEAD_DIM, HEAD_DIM), lambda b: (b, 0, 0, 0))
    n_spec = pl.BlockSpec((None, N_ML, HEAD_DIM), lambda b: (b, 0, 0))
    m_spec = pl.BlockSpec((None, N_ML, 1), lambda b: (b, 0, 0))
    return pl.pallas_call(
        functools.partial(_mlstm_kernel, t=t, n_chunks=t // CHUNK, l_true=l_true),
        grid=(b,),
        in_specs=[
            seq(COL_MU // ML_W), seq(COL_MV // ML_W), seq(COL_MO // ML_W),
            pl.BlockSpec((None, SUBLANES, ML_W), lambda b: (b, 0, 0)),
            pl.BlockSpec((None, N_GATES, t), lambda b: (b, 0, 0)),
            pl.BlockSpec((None, N_GATES, 1), lambda b: (l, 0, 0)),
            layer(ML_CONV), layer(1), w_spec, w_spec, c_spec, n_spec, m_spec, layer(1), layer(1),
        ],
        out_specs=[pl.BlockSpec((None, t, ML_W), lambda b: (b, 0, 0)), c_spec, n_spec, m_spec],
        out_shape=[jax.ShapeDtypeStruct((b, t, ML_W), BF16),
                   jax.ShapeDtypeStruct((b, N_ML, HEAD_DIM, HEAD_DIM), F32),
                   jax.ShapeDtypeStruct((b, N_ML, HEAD_DIM), F32),
                   jax.ShapeDtypeStruct((b, N_ML, 1), F32)],
        scratch_shapes=[
            pltpu.VMEM((SUBLANES + t, ML_W), F32), pltpu.VMEM((t, ML_W), F32),
            pltpu.VMEM((N_ML, HEAD_DIM, HEAD_DIM), F32), pltpu.VMEM((N_ML, HEAD_DIM), F32),
            pltpu.VMEM((N_ML, 1), F32),
        ],
        compiler_params=_params(("arbitrary",)),
        name="mlstm",
    )(ml_in, ml_in, ml_in, prev, gates, gbias, cw, cb, wq, wk, c0, n0, m0, gn, skip)


def _conv_gate_kernel(ug_ref, uv_ref, hg_ref, hv_ref, pg_ref, pv_ref, wg_ref, wv_ref,
                      bg_ref, bv_ref, o_ref, sg, sv, *, tt):
    first = pl.program_id(1) == 0
    halo = SUBLANES

    def conv(u_ref, h_ref, p_ref, w_ref, b_ref, scr):
        @pl.when(first)
        def _():
            scr[0:halo, :] = p_ref[...]

        @pl.when(jnp.logical_not(first))
        def _():
            scr[0:halo, :] = h_ref[...]

        scr[halo:halo + tt, :] = u_ref[...]
        y = b_ref[...]
        for j in range(FFN_CONV):
            start = halo - (FFN_CONV - 1) + j
            y = y + scr[start:start + tt, :] * w_ref[j:j + 1, :]
        return y

    gate = conv(ug_ref, hg_ref, pg_ref, wg_ref, bg_ref, sg)
    val = conv(uv_ref, hv_ref, pv_ref, wv_ref, bv_ref, sv)
    o_ref[...] = (gate * jax.nn.sigmoid(gate) * val).astype(o_ref.dtype)


def _conv_gate(up, prev, cw, cb, l):
    b, t, f2 = up.shape
    f = f2 // 2
    tt = _tile(t, 512, SUBLANES)
    tn = _tile(f, max(512, 512 * 512 // tt))
    nf = f // tn
    rb = tt // SUBLANES

    def main(off):
        return pl.BlockSpec((None, tt, tn), lambda b, i, n: (b, i, n + off))

    def halo(off):
        return pl.BlockSpec((None, SUBLANES, tn),
                            lambda b, i, n: (b, jnp.maximum(i * rb - 1, 0), n + off))

    def prev_spec(off):
        return pl.BlockSpec((None, SUBLANES, tn), lambda b, i, n: (b, 0, n + off))

    def w_spec(rows, off):
        return pl.BlockSpec((None, rows, tn), lambda b, i, n: (l, 0, n + off))

    return pl.pallas_call(
        functools.partial(_conv_gate_kernel, tt=tt),
        grid=(b, t // tt, nf),
        in_specs=[main(0), main(nf), halo(0), halo(nf), prev_spec(0), prev_spec(nf),
                  w_spec(FFN_CONV, 0), w_spec(FFN_CONV, nf), w_spec(1, 0), w_spec(1, nf)],
        out_specs=pl.BlockSpec((None, tt, tn), lambda b, i, n: (b, i, n)),
        out_shape=jax.ShapeDtypeStruct((b, t, f), BF16),
        scratch_shapes=[pltpu.VMEM((SUBLANES + tt, tn), F32), pltpu.VMEM((SUBLANES + tt, tn), F32)],
        compiler_params=_params(("arbitrary", "arbitrary", "arbitrary")),
        name="conv_gate",
    )(up, up, up, up, prev, prev, cw, cw, cb, cb)


def _ffn_up_kernel(x_ref, wg_ref, wv_ref, pg_ref, pv_ref, cwg_ref, cwv_ref, cbg_ref, cbv_ref,
                   xs_ref, o_ref, sg_ref, sv_ref, ups_g_ref, ups_v_ref,
                   wbf_g, wbf_v, scr_g, scr_v, *, tm, n_sub):
    t = pl.program_id(2)
    halo = SUBLANES

    @pl.when(jnp.logical_and(pl.program_id(1) == 0, t == 0))
    def _():
        wbf_g[...] = wg_ref[...].astype(BF16)
        wbf_v[...] = wv_ref[...].astype(BF16)
        ups_g_ref[...] = jnp.dot(xs_ref[...], wbf_g[...], preferred_element_type=F32)
        ups_v_ref[...] = jnp.dot(xs_ref[...], wbf_v[...], preferred_element_type=F32)

    @pl.when(t == 0)
    def _():
        scr_g[0:halo, :] = pg_ref[...]
        scr_v[0:halo, :] = pv_ref[...]

    sub = tm // n_sub

    def project(c):
        rows = slice(c * sub, (c + 1) * sub)
        x = x_ref[rows, :]
        dst = slice(halo + c * sub, halo + (c + 1) * sub)
        scr_g[dst, :] = jnp.dot(x, wbf_g[...], preferred_element_type=F32)
        scr_v[dst, :] = jnp.dot(x, wbf_v[...], preferred_element_type=F32)

    def conv(c, scr, cw_ref, cb_ref):
        y = cb_ref[...]
        for j in range(FFN_CONV):
            start = halo - (FFN_CONV - 1) + j + c * sub
            y = y + scr[start:start + sub, :] * cw_ref[j:j + 1, :]
        return y

    project(0)
    for c in range(n_sub):
        if c + 1 < n_sub:
            project(c + 1)
        gate = conv(c, scr_g, cwg_ref, cbg_ref)
        val = conv(c, scr_v, cwv_ref, cbv_ref)
        o_ref[c * sub:(c + 1) * sub, :] = (gate * jax.nn.sigmoid(gate) * val).astype(o_ref.dtype)

    for scr, s_ref in ((scr_g, sg_ref), (scr_v, sv_ref)):
        tail = scr[tm:tm + halo, :]
        s_ref[...] = tail
        scr[0:halo, :] = tail


def _ffn_up_fused(x, w_up, prev, cw, cb, l, x_side):
    b, t, d = x.shape
    ts = x_side.shape[1]
    f = w_up.shape[2] // 2
    tm = _tile(t, 1024, SUBLANES)
    tn = _tile(f, 512)
    nf = f // tn

    def cols(rows, off, arr_l):
        return pl.BlockSpec((None, rows, tn), lambda n, b, i: (arr_l(b), 0, n + off))

    layer = lambda b: l
    batch = lambda b: b
    tail_spec = pl.BlockSpec((None, SUBLANES, tn), lambda n, b, i: (b, 0, n))
    side_spec = pl.BlockSpec((None, ts, tn), lambda n, b, i: (0, 0, n))
    return pl.pallas_call(
        functools.partial(_ffn_up_kernel, tm=tm, n_sub=FFN_SUB if tm % (FFN_SUB * 16) == 0 else 1),
        grid=(nf, b, t // tm),
        in_specs=[
            pl.BlockSpec((None, tm, d), lambda n, b, i: (b, i, 0)),
            cols(d, 0, layer), cols(d, nf, layer),
            cols(SUBLANES, 0, batch), cols(SUBLANES, nf, batch),
            cols(FFN_CONV, 0, layer), cols(FFN_CONV, nf, layer),
            cols(1, 0, layer), cols(1, nf, layer),
            pl.BlockSpec((None, ts, d), lambda n, b, i: (0, 0, 0)),
        ],
        out_specs=[pl.BlockSpec((None, tm, tn), lambda n, b, i: (b, i, n)), tail_spec, tail_spec,
                   side_spec, side_spec],
        out_shape=[jax.ShapeDtypeStruct((b, t, f), BF16),
                   jax.ShapeDtypeStruct((b, SUBLANES, f), F32),
                   jax.ShapeDtypeStruct((b, SUBLANES, f), F32),
                   jax.ShapeDtypeStruct((1, ts, f), F32),
                   jax.ShapeDtypeStruct((1, ts, f), F32)],
        scratch_shapes=[pltpu.VMEM((d, tn), BF16), pltpu.VMEM((d, tn), BF16),
                        pltpu.VMEM((SUBLANES + tm, tn), F32), pltpu.VMEM((SUBLANES + tm, tn), F32)],
        compiler_params=_params(("arbitrary", "arbitrary", "arbitrary")),
        name="ffn_up_conv",
    )(x, w_up, w_up, prev, prev, cw, cw, cb, cb, x_side)


def _rope_tables(pos, rows):
    half = HEAD_DIM // 2
    inv = ROPE_BASE ** (-jnp.arange(half, dtype=F32) / half)
    ang = pos.astype(F32)[:, None] * inv[None, :]
    cos, sin = jnp.cos(ang), jnp.sin(ang)
    cos2 = jnp.concatenate([cos, cos], axis=-1)
    sin2 = jnp.concatenate([-sin, sin], axis=-1)
    pad = rows - pos.shape[0]
    if pad:
        cos2 = jnp.pad(cos2, ((0, pad), (0, 0)))
        sin2 = jnp.pad(sin2, ((0, pad), (0, 0)))
    return cos2, sin2


def _prev_rows(buf):
    return jnp.pad(buf, ((0, 0), (SUBLANES - buf.shape[1], 0), (0, 0)))


def _mixers(l, b, t, ret_in, ml_in, gates, st, w, rope, lg_tab):
    t_pad = -(-t // CHUNK) * CHUNK
    l_true = CHUNK if t % CHUNK == 0 else t
    if t_pad != t:
        ret_in = jnp.pad(ret_in, ((0, 0), (0, t_pad - t), (0, 0)))
        ml_in = jnp.pad(ml_in, ((0, 0), (0, t_pad - t), (0, 0)))
        gates = jnp.pad(gates, ((0, 0), (0, 0), (0, t_pad - t)))
    ro, ret_s = _retention(ret_in, rope[0], rope[1], lg_tab, st["ret_s"], w["ret_norm_g"], l, l_true)
    mo, ml_c, ml_n, ml_m = _mlstm(
        ml_in, _prev_rows(st["ml_conv"]), gates, w["gbias"], w["ml_conv_w"], w["ml_conv_b"],
        w["ml_wq"], w["ml_wk"], st["ml_c"], st["ml_n"], st["ml_m"][:, :, None],
        w["ml_norm_g"], w["ml_skip"], l, l_true)
    new_st = dict(ret_s=ret_s, ml_c=ml_c, ml_n=ml_n, ml_m=ml_m[:, :, 0])
    return ro[:, :t], mo[:, :t], new_st


def _layer(l, lam_init, xp, xs, mods_p, mods_s, st_p, st_s, w, cache, rope_p, rope_s, lg_tab,
           kv_stacks):
    bp, tp, d = xp.shape
    bs = st_s["ret_s"].shape[0]
    ts = xs.shape[1] // bs
    sh1p, sc1p, g1p, sh2p, sc2p, g2p = mods_p
    sh1s, sc1s, g1s, sh2s, sc2s, g2s = mods_s
    depth = w["w_in_t"].shape[0]
    f2 = w["ffn_w_up"].shape[2]

    hp = _norm(xp, w["g_mix"], l, sc1p, sh1p)
    hs = _norm(xs, w["g_mix"], l, sc1s, sh1s)

    def in_proj(col0, width, tn, name, stack=None):
        assert col0 % tn == 0 and width % tn == 0
        return _matmul([hp], w["w_in_t"], l, width // tn, tn, 1024, name=name, w_transposed=True,
                       w_tile=lambda n: n + col0 // tn, stack=stack, side=([hs], None, None))

    q_p, q_s = in_proj(W_COL_Q, DA_W, 1024, "q_proj")
    k_stack, k_s = in_proj(W_COL_K, DA_W, 1024, "k_proj", stack=(depth, kv_stacks[0]))
    v_stack, v_s = in_proj(W_COL_V, DA_W, 1024, "v_proj", stack=(depth, kv_stacks[1]))
    ret_p, ret_s = in_proj(W_COL_RET, 4 * RET_W, 1024, "ret_proj")
    ml_p, ml_s = in_proj(W_COL_ML, 3 * ML_W, 512, "ml_proj")
    gates_p = _gates(hp, w["w_in_t"], l)
    gates_s = _gates(hs, w["w_in_t"], l).reshape(GATE_ROWS, bs, ts).transpose(1, 0, 2)
    ml_s = ml_s.reshape(bs, ts, 3 * ML_W)

    oa_p = _attn_prompt(q_p, k_stack, v_stack, w["lam_p"], w["da_norm_g"], l, lam_init)
    oa_s = _attn_decode(q_s.reshape(bs, ts, DA_W), k_s, v_s, *cache, w["lam_p"], w["da_norm_g"],
                        l, lam_init)
    ro_p, mo_p, new_p = _mixers(l, bp, tp, ret_p, ml_p, gates_p, st_p, w, rope_p, lg_tab)
    ro_s, mo_s, new_s = _mixers(l, bs, ts, ret_s.reshape(bs, ts, 4 * RET_W), ml_s, gates_s,
                                st_s, w, rope_s, lg_tab)
    mix_s = [a.astype(BF16).reshape(1, bs * ts, a.shape[-1]) for a in (oa_s, ro_s, mo_s)]
    xp, xs = _matmul([oa_p, ro_p, mo_p], w["w_out"], l, d // _tile(d, 1024), _tile(d, 1024), 1024,
                     res=xp, gate=g1p, side=(mix_s, xs, g1s), name="out_proj")

    h2p = _norm(xp, w["g_ffn"], l, sc2p, sh2p)
    h2s = _norm(xs, w["g_ffn"], l, sc2s, sh2s)
    act_p, tail_g, tail_v, up_g_s, up_v_s = _ffn_up_fused(
        h2p, w["ffn_w_up"], _prev_rows(st_p["ffn_conv"]), w["ffn_conv_w"], w["ffn_conv_b"], l, h2s)
    up_s = jnp.concatenate([up_g_s, up_v_s], axis=-1).reshape(bs, ts, f2)
    act_s = _conv_gate(up_s, _prev_rows(st_s["ffn_conv"]), w["ffn_conv_w"], w["ffn_conv_b"], l)
    xp, xs = _matmul([act_p], w["ffn_w_down"], l, d // _tile(d, 512), _tile(d, 512), 512,
                     res=xp, gate=g2p, side=([act_s.reshape(1, bs * ts, f2 // 2)], xs, g2s),
                     name="ffn_down")

    new_p.update(
        ml_conv=ml_p[:, tp - (ML_CONV - 1):, COL_MU:COL_MU + ML_W],
        ffn_conv=jnp.concatenate([tail_g, tail_v], axis=-1)[:, SUBLANES - (FFN_CONV - 1):, :])
    new_s.update(ml_conv=ml_s[:, ts - (ML_CONV - 1):, COL_MU:COL_MU + ML_W],
                 ffn_conv=up_s[:, ts - (FFN_CONV - 1):, :])
    return xp, xs, (k_stack, v_stack), (k_s, v_s), new_p, new_s


def kernel(x_prompt, x_sample, c_prompt, c_sample, cache_k, cache_v, page_table, state_ret, state_ml_c, state_ml_n, state_ml_m, state_ml_conv, state_ffn_conv, w_ada, b_ada, g_mix, w_in, lam_q1, lam_k1, lam_q2, lam_k2, da_norm_g, ret_norm_g, ml_conv_w, ml_conv_b, ml_wq, ml_wk, ml_b_i, ml_b_f, ml_norm_g, ml_skip, w_out, g_ffn, ffn_w_up, ffn_conv_w, ffn_conv_b, ffn_w_down, g_final):
    bp, tp, d = x_prompt.shape
    bs, ts, _ = x_sample.shape
    depth = w_in.shape[0]
    f2 = ffn_w_up.shape[2]
    past = page_table.shape[1] * PAGE

    def per_layer_rows(a):
        return a.reshape(depth, 1, a.shape[-1])

    zeros8 = jnp.zeros((depth, GATE_ROWS - N_GATES), F32)
    w = dict(
        g_mix=g_mix, g_ffn=g_ffn, w_out=w_out, ffn_w_up=ffn_w_up, ffn_w_down=ffn_w_down,
        w_in_t=jnp.swapaxes(w_in, 1, 2),
        gbias=jnp.concatenate([ml_b_i, ml_b_f, zeros8], axis=1)[:, :, None],
        lam_p=jnp.stack([lam_q1, lam_k1, lam_q2, lam_k2], axis=1),
        da_norm_g=per_layer_rows(da_norm_g), ret_norm_g=per_layer_rows(ret_norm_g),
        ml_conv_w=ml_conv_w, ml_conv_b=per_layer_rows(ml_conv_b), ml_wq=ml_wq, ml_wk=ml_wk,
        ml_norm_g=per_layer_rows(ml_norm_g), ml_skip=per_layer_rows(ml_skip),
        ffn_conv_w=ffn_conv_w, ffn_conv_b=per_layer_rows(ffn_conv_b))

    n_c = bp + bs
    c_rows = -(-n_c // SUBLANES) * SUBLANES
    c_all = jnp.pad(jnp.concatenate([c_prompt, c_sample], axis=0), ((0, c_rows - n_c), (0, 0)))
    mod = _ada(c_all, w_ada, b_ada)

    lg_np = np.log(1.0 - 2.0 ** (-5.0 - np.arange(N_RET, dtype=np.float64)))
    lg_tab = jnp.asarray(np.broadcast_to(lg_np[:, None], (N_RET, HEAD_DIM)), F32)
    tp_pad = -(-tp // CHUNK) * CHUNK
    ts_pad = -(-ts // CHUNK) * CHUNK
    rope_p = _rope_tables(jnp.arange(tp), tp_pad)
    rope_s = _rope_tables(past + jnp.arange(ts), ts_pad)

    zero_st = dict(
        ret_s=jnp.zeros((bp, N_RET, HEAD_DIM, HEAD_DIM), F32),
        ml_c=jnp.zeros((bp, N_ML, HEAD_DIM, HEAD_DIM), F32),
        ml_n=jnp.zeros((bp, N_ML, HEAD_DIM), F32),
        ml_m=jnp.zeros((bp, N_ML), F32),
        ml_conv=jnp.zeros((bp, ML_CONV - 1, ML_W), F32),
        ffn_conv=jnp.zeros((bp, FFN_CONV - 1, f2), F32))
    names = ("ret_s", "ml_c", "ml_n", "ml_m", "ml_conv", "ffn_conv")
    out_p = {n: [] for n in names}
    out_s = {n: [] for n in names}
    kv_p = (None, None)
    k_s, v_s = [], []
    yp = x_prompt
    ys = x_sample.reshape(1, bs * ts, d)
    for l in range(depth):
        lam_init = 0.8 - 0.6 * math.exp(-0.3 * l)
        parts = jnp.split(mod[l], 6, axis=-1)
        mods_p = [m[:bp, None, :] for m in parts]
        mods_s = [jnp.repeat(m[bp:n_c], ts, axis=0)[None] for m in parts]
        st_in = dict(ret_s=state_ret[l], ml_c=state_ml_c[l], ml_n=state_ml_n[l], ml_m=state_ml_m[l],
                     ml_conv=state_ml_conv[l], ffn_conv=state_ffn_conv[l])
        yp, ys, kv_p, kv_s, st_p, st_s = _layer(
            l, lam_init, yp, ys, mods_p, mods_s, zero_st, st_in, w,
            (cache_k, cache_v, page_table), rope_p, rope_s, lg_tab, kv_p)
        k_s.append(kv_s[0])
        v_s.append(kv_s[1])
        for n in names:
            out_p[n].append(st_p[n])
            out_s[n].append(st_s[n])

    y_prompt = _norm(yp, g_final, None, out_dtype=F32)
    y_sample = _norm(ys, g_final, None, out_dtype=F32).reshape(bs, ts, d)
    kv_shape_p = (depth, bp, tp, N_DA, HEAD_DIM)
    kv_shape_s = (depth, bs, ts, N_DA, HEAD_DIM)
    return (y_prompt, y_sample,
            kv_p[0].reshape(kv_shape_p), kv_p[1].reshape(kv_shape_p),
            jnp.stack(k_s).reshape(kv_shape_s), jnp.stack(v_s).reshape(kv_shape_s),
            jnp.stack(out_p["ret_s"]), jnp.stack(out_s["ret_s"]),
            jnp.stack(out_p["ml_c"]), jnp.stack(out_s["ml_c"]),
            jnp.stack(out_p["ml_n"]), jnp.stack(out_s["ml_n"]),
            jnp.stack(out_p["ml_m"]), jnp.stack(out_s["ml_m"]),
            jnp.stack(out_p["ml_conv"]), jnp.stack(out_s["ml_conv"]),
            jnp.stack(out_p["ffn_conv"]), jnp.stack(out_s["ffn_conv"]))
```
